```python
import math
import jax
import jax.numpy as jnp
from jax import lax
import numpy as np

D_MODEL = 1024
BATCH = 8
SEQ = 2048
DEPTH = 2
DEC_BATCH = 32
DEC_SEQ = 8
PAST_LEN = 16384
PAGE_SIZE = 128

HEAD_DIM = 64
MIX_WIDTH = D_MODEL
CONV_CH = MIX_WIDTH // 4
CONV_K = 31
NSA_HEADS = MIX_WIDTH // 2 // HEAD_DIM
NSA_KV_HEADS = max(1, NSA_HEADS // 4)
GQA_GROUP = NSA_HEADS // NSA_KV_HEADS
CMP_BLOCK = 64
SEL_BLOCK = 64
N_SEL = 16
WINDOW = 512
Q_BLOCK = 64
FORCE_SCORE = 1e4
RET_HEADS = MIX_WIDTH // 4 // HEAD_DIM
RET_CHUNK = 128
N_EXPERTS = 32
TOP_K = 4
D_FF = D_MODEL
SWIGLU_LIMIT = 7.0
SWIGLU_ALPHA = 1.702
MOE_BLOCK = 128
PLE_DIM = 256
DEEPNORM_ALPHA = (2 * DEPTH) ** 0.25
DEEPNORM_BETA = (8 * DEPTH) ** -0.25
LN_EPS = 1e-5
NEG_INF = -1e30
TINY = 1e-30
COL_CONV = 2 * CONV_CH
COL_Q = NSA_HEADS * HEAD_DIM
COL_KV = 6 * NSA_KV_HEADS * HEAD_DIM
COL_GATE = 3 * NSA_HEADS
COL_RET = 4 * RET_HEADS * HEAD_DIM
IN_COLS = COL_CONV + COL_Q + COL_KV + COL_GATE + COL_RET

kernel_name = 'hybrid_conv_nsa_retention_moe_decoder_step'


def layer_norm(x, g, b):
    xf = x.astype(jnp.float32)
    mu = xf.mean(-1, keepdims=True)
    var = jnp.square(xf - mu).mean(-1, keepdims=True)
    return ((xf - mu) * lax.rsqrt(var + LN_EPS) * g + b).astype(x.dtype)


def masked_softmax(s, mask):
    s = jnp.where(mask, s, NEG_INF)
    s = s - s.max(axis=-1, keepdims=True)
    p = jnp.where(mask, jnp.exp(s), 0.0)
    return p / jnp.maximum(p.sum(-1, keepdims=True), TINY)


def alibi_slopes(n):
    return jnp.exp2(-8.0 * jnp.arange(1, n + 1, dtype=jnp.float32) / n)


def conformer_conv(u, s0, conv_w, conv_b, ln_g, ln_b):
    a, g = jnp.split(u, 2, axis=-1)
    h = a * jax.nn.sigmoid(g)
    full = jnp.concatenate([s0.astype(h.dtype), h], axis=1)
    y = lax.conv_general_dilated(full, conv_w[:, None, :].astype(h.dtype), window_strides=(1,), padding='VALID',
                                 dimension_numbers=('NWC', 'WIO', 'NWC'), feature_group_count=CONV_CH)
    y = layer_norm(y + conv_b, ln_g, ln_b)
    return jax.nn.silu(y.astype(jnp.float32)).astype(u.dtype), full[:, -(CONV_K - 1):]


def retention(q, k, v, s0):
    B, T = q.shape[:2]
    C = math.gcd(T, RET_CHUNK)
    n = T // C
    log_g = jnp.log1p(-jnp.exp2(-5.0 - jnp.arange(RET_HEADS, dtype=jnp.float32)))
    i = jnp.arange(C, dtype=jnp.float32)
    diff = i[:, None] - i[None, :]
    dmask = jnp.exp(jnp.where(diff >= 0, log_g[:, None, None] * diff, -jnp.inf))
    q_dec = jnp.exp(log_g[None, :] * (i[:, None] + 1.0))
    k_dec = jnp.exp(log_g[None, :] * (C - 1.0 - i[:, None]))
    c_dec = jnp.exp(log_g * C)

    def step(S, inp):
        qc, kc, vc = inp
        att = jnp.einsum('bihd,bjhd->bhij', qc, kc) * dmask
        o = (jnp.einsum('bhij,bjhe->bihe', att, vc)
             + jnp.einsum('bihd,bhde->bihe', qc, S) * q_dec[None, :, :, None])
        S = S * c_dec[None, :, None, None] + jnp.einsum('bjhd,bjhe->bhde', kc * k_dec[None, :, :, None], vc)
        return S, o

    xs = tuple(a.reshape(B, n, C, RET_HEADS, HEAD_DIM).swapaxes(0, 1) for a in (q, k, v))
    S, o = lax.scan(step, s0, xs)
    return o.swapaxes(0, 1).reshape(B, T, RET_HEADS, HEAD_DIM), S


def nsa_attention(q, kv_full, win_full, gate_logits, q_start):
    f32 = jnp.float32
    B, T = q.shape[:2]
    L = kv_full.shape[1]
    KV, G, HD = NSA_KV_HEADS, GQA_GROUP, HEAD_DIM
    n_cmp = L // CMP_BLOCK
    n_blk = -(-L // SEL_BLOCK)
    k_sel = min(N_SEL, n_blk)
    qc = math.gcd(T, Q_BLOCK)
    n_qc = T // qc
    slopes = alibi_slopes(NSA_HEADS).reshape(KV, G)[None, :, :, None, None]
    cmp = kv_full[:, :n_cmp * CMP_BLOCK, :2].astype(f32).reshape(B, n_cmp, CMP_BLOCK, 2, KV, HD).mean(axis=2)
    k_cmp, v_cmp = cmp[:, :, 0], cmp[:, :, 1]
    cmp_end = (jnp.arange(n_cmp) + 1) * CMP_BLOCK - 1
    sel = jnp.pad(kv_full[:, :, 2:], ((0, 0), (0, n_blk * SEL_BLOCK - L), (0, 0), (0, 0), (0, 0)))
    sel = sel.reshape(B, n_blk, SEL_BLOCK, 2, KV, HD).transpose(3, 0, 4, 1, 2, 5)
    k_blk, v_blk = sel[0], sel[1]
    gather = jax.vmap(jax.vmap(lambda blocks, idx: blocks[idx]))

    def chunk(args):
        q_c, g_c, c = args
        t = q_start + c * qc + jnp.arange(qc)
        qg = (q_c.astype(f32) * HD ** -0.5).reshape(B, qc, KV, G, HD)
        s = jnp.einsum('bqkgd,bnkd->bkgqn', qg, k_cmp) - slopes * (t[:, None] - cmp_end[None, :]).astype(f32)
        p_cmp = masked_softmax(s, cmp_end[None, :] <= t[:, None])
        o_cmp = jnp.einsum('bkgqn,bnkd->bqkgd', p_cmp, v_cmp)
        imp = jnp.pad(p_cmp.sum(axis=2), ((0, 0), (0, 0), (0, 0), (0, n_blk - n_cmp)))
        blk = jnp.arange(n_blk)[None, :]
        cur = (t // SEL_BLOCK)[:, None]
        forced = (blk == 0) | (blk == cur) | (blk == cur - 1)
        score = jnp.where(blk <= cur, jnp.where(forced, FORCE_SCORE, imp), -1.0)
        top_s, top_i = lax.top_k(score, k_sel)
        kg = gather(k_blk, top_i).astype(f32)
        vg = gather(v_blk, top_i).astype(f32)
        pos = top_i[..., None] * SEL_BLOCK + jnp.arange(SEL_BLOCK)
        m_sel = (top_s[..., None] >= 0) & (pos <= t[:, None, None])
        dist = (t[:, None, None] - pos).astype(f32)[:, :, None]
        s = jnp.einsum('bqkgd,bkqnsd->bkgqns', qg, kg) - slopes[..., None] * dist
        p = masked_softmax(s.reshape(B, KV, G, qc, k_sel * SEL_BLOCK),
                           m_sel.reshape(B, KV, 1, qc, k_sel * SEL_BLOCK))
        o_sel = jnp.einsum('bkgqns,bkqnsd->bqkgd', p.reshape(B, KV, G, qc, k_sel, SEL_BLOCK), vg)
        wk = lax.dynamic_slice_in_dim(win_full, c * qc, qc + WINDOW, axis=1).astype(f32)
        pw = q_start - WINDOW + c * qc + jnp.arange(qc + WINDOW)
        dw = t[:, None] - pw[None, :]
        m_win = (dw >= 0) & (dw < WINDOW) & (pw[None, :] >= 0)
        s = jnp.einsum('bqkgd,bskd->bkgqs', qg, wk[:, :, 0]) - slopes * dw.astype(f32)
        p = masked_softmax(s, m_win)
        o_win = jnp.einsum('bkgqs,bskd->bqkgd', p, wk[:, :, 1])
        g = jax.nn.sigmoid(g_c.astype(f32)).reshape(B, qc, KV, G, 3)
        o = g[..., 0:1] * o_cmp + g[..., 1:2] * o_sel + g[..., 2:3] * o_win
        return o.reshape(B, qc, NSA_HEADS * HD)

    q_chunks = q.reshape(B, n_qc, qc, NSA_HEADS, HD).swapaxes(0, 1)
    g_chunks = gate_logits.reshape(B, n_qc, qc, NSA_HEADS, 3).swapaxes(0, 1)
    out = lax.map(chunk, (q_chunks, g_chunks, jnp.arange(n_qc)))
    return out.swapaxes(0, 1).reshape(B, T, NSA_HEADS * HD).astype(q.dtype)


def moe_ffn(h2d, w_router, b_router, w_up, b_up, w_down, b_down):
    N, D = h2d.shape
    logits = jnp.dot(h2d.astype(jnp.float32), w_router.astype(jnp.float32)) + b_router
    top_v, top_e = lax.top_k(logits, TOP_K)
    gate = jax.nn.softmax(top_v, axis=-1)
    NK = N * TOP_K
    flat_e = top_e.reshape(NK)
    order = jnp.argsort(flat_e)
    se = flat_e[order]
    stok = order // TOP_K
    sgate = gate.reshape(NK)[order]
    counts = jnp.bincount(flat_e, length=N_EXPERTS)
    starts = jnp.cumsum(counts) - counts
    padded = (counts + MOE_BLOCK - 1) // MOE_BLOCK * MOE_BLOCK
    pend = jnp.cumsum(padded)
    pstart = pend - padded
    dest = pstart[se] + jnp.arange(NK) - starts[se]
    n_blocks = -(-(NK + N_EXPERTS * (MOE_BLOCK - 1)) // MOE_BLOCK)
    R = n_blocks * MOE_BLOCK
    row_tok = jnp.full((R,), N, jnp.int32).at[dest].set(stok)
    row_gate = jnp.zeros((R,), jnp.float32).at[dest].set(sgate)
    block_e = jnp.minimum(jnp.searchsorted(pend, jnp.arange(n_blocks) * MOE_BLOCK, side='right'), N_EXPERTS - 1)
    xpad = jnp.concatenate([h2d, jnp.zeros((1, D), h2d.dtype)], axis=0)
    xb = xpad[row_tok].reshape(n_blocks, MOE_BLOCK, D)

    def expert_block(args):
        xblk, e = args
        hu = jnp.dot(xblk, w_up[e]) + b_up[e]
        g = jnp.minimum(hu[:, 0::2], SWIGLU_LIMIT)
        u = jnp.clip(hu[:, 1::2], -SWIGLU_LIMIT, SWIGLU_LIMIT)
        act = (u + 1.0) * g * jax.nn.sigmoid(SWIGLU_ALPHA * g)
        return jnp.dot(act, w_down[e]) + b_down[e]

    yb = lax.map(expert_block, (xb, block_e)).reshape(R, D)
    out = jnp.zeros((N + 1, D), jnp.float32).at[row_tok].add(yb.astype(jnp.float32) * row_gate[:, None])
    return out[:N].astype(h2d.dtype)


def decoder_layer(x, p, past_kv, win_prefix, ret_s0, conv_s0, q_start, n_win_keep,
                  w_in, w_out, conv_w, conv_b, conv_ln_g, conv_ln_b, ret_norm_g, ln1_g, ln1_b,
                  w_router, b_router, w_up, b_up, w_down, b_down, w_ple, w_plg, ln2_g, ln2_b):
    B, T, _ = x.shape
    u = jnp.einsum('btd,dc->btc', x, w_in)
    cuts = [COL_CONV, COL_CONV + COL_Q, COL_CONV + COL_Q + COL_KV, COL_CONV + COL_Q + COL_KV + COL_GATE]
    u_conv, u_q, u_kv, u_gate, u_ret = jnp.split(u, cuts, axis=-1)
    conv_out, conv_state = conformer_conv(u_conv, conv_s0, conv_w, conv_b, conv_ln_g, conv_ln_b)
    q = u_q.reshape(B, T, NSA_HEADS, HEAD_DIM)
    kv_new = u_kv.reshape(B, T, 6, NSA_KV_HEADS, HEAD_DIM)
    paged_new = kv_new[:, :, :4]
    kv_full = jnp.concatenate([past_kv.astype(kv_new.dtype), paged_new], axis=1)
    win_full = jnp.concatenate([win_prefix.astype(kv_new.dtype), kv_new[:, :, 4:]], axis=1)
    nsa_out = nsa_attention(q, kv_full, win_full, u_gate.reshape(B, T, NSA_HEADS, 3), q_start)
    rq, rk, rv, rg = jnp.split(u_ret.astype(jnp.float32), 4, axis=-1)
    rs = (B, T, RET_HEADS, HEAD_DIM)
    ret_o, ret_state = retention(rq.reshape(rs), rk.reshape(rs) * HEAD_DIM ** -0.5, rv.reshape(rs),
                                 ret_s0.astype(jnp.float32))
    mu = ret_o.mean(-1, keepdims=True)
    var = jnp.square(ret_o - mu).mean(-1, keepdims=True)
    ret_o = ((ret_o - mu) * lax.rsqrt(var + LN_EPS)).reshape(B, T, RET_HEADS * HEAD_DIM) * ret_norm_g
    ret_out = (jax.nn.silu(rg) * ret_o).astype(x.dtype)
    mixed = jnp.concatenate([conv_out, nsa_out, ret_out], axis=-1)
    h = layer_norm(DEEPNORM_ALPHA * x + jnp.einsum('btc,cd->btd', mixed, w_out), ln1_g, ln1_b)
    ffn = moe_ffn(h.reshape(B * T, D_MODEL), w_router, b_router, w_up, b_up, w_down, b_down).reshape(B, T, D_MODEL)
    ple = jnp.einsum('btp,pd->btd', p, w_ple) * jax.nn.sigmoid(jnp.einsum('btd,de->bte', h, w_plg))
    y = layer_norm(DEEPNORM_ALPHA * h + ffn + ple, ln2_g, ln2_b)
    return y, paged_new, win_full[:, -n_win_keep:], ret_state, conv_state


def setup_inputs(seed: int = 0) -> dict:
    key = jax.random.key(seed)
    ks = jax.random.split(key, 32)
    f32 = jnp.float32
    n_pages = PAST_LEN // PAGE_SIZE
    n_pool = (DEC_BATCH * n_pages * 5 + 3) // 4
    wbuf = min(WINDOW, PAST_LEN)

    def nrm(k, shape, scale):
        return jax.random.normal(k, shape, f32) * scale

    page_table = jax.random.permutation(ks[8], n_pool)[:DEC_BATCH * n_pages].reshape(DEC_BATCH, n_pages).astype(jnp.int32)
    return {
        'x_prompt': nrm(ks[0], (BATCH, SEQ, D_MODEL), 1.0),
        'x_sample': nrm(ks[1], (DEC_BATCH, DEC_SEQ, D_MODEL), 1.0),
        'p_prompt': nrm(ks[2], (DEPTH, BATCH, SEQ, PLE_DIM), 1.0),
        'p_sample': nrm(ks[3], (DEPTH, DEC_BATCH, DEC_SEQ, PLE_DIM), 1.0),
        'cache_nsa_kv': nrm(ks[4], (DEPTH, n_pool, PAGE_SIZE, 4, NSA_KV_HEADS, HEAD_DIM), 1.0),
        'cache_win_kv': nrm(ks[5], (DEPTH, DEC_BATCH, wbuf, 2, NSA_KV_HEADS, HEAD_DIM), 1.0),
        'state_ret': nrm(ks[6], (DEPTH, DEC_BATCH, RET_HEADS, HEAD_DIM, HEAD_DIM), 1.0),
        'state_conv': nrm(ks[7], (DEPTH, DEC_BATCH, CONV_K - 1, CONV_CH), 0.5),
        'page_table': page_table,
        'w_in': nrm(ks[9], (DEPTH, D_MODEL, IN_COLS), D_MODEL ** -0.5),
        'w_out': nrm(ks[10], (DEPTH, MIX_WIDTH, D_MODEL), MIX_WIDTH ** -0.5 * DEEPNORM_BETA),
        'conv_w': nrm(ks[11], (DEPTH, CONV_K, CONV_CH), CONV_K ** -0.5),
        'conv_b': nrm(ks[12], (DEPTH, CONV_CH), 0.02),
        'conv_ln_g': 1.0 + nrm(ks[13], (DEPTH, CONV_CH), 0.02),
        'conv_ln_b': nrm(ks[14], (DEPTH, CONV_CH), 0.02),
        'ret_norm_g': 1.0 + nrm(ks[15], (DEPTH, RET_HEADS * HEAD_DIM), 0.02),
        'ln1_g': 1.0 + nrm(ks[16], (DEPTH, D_MODEL), 0.02),
        'ln1_b': nrm(ks[17], (DEPTH, D_MODEL), 0.02),
        'w_router': nrm(ks[18], (DEPTH, D_MODEL, N_EXPERTS), D_MODEL ** -0.5),
        'b_router': nrm(ks[19], (DEPTH, N_EXPERTS), 0.01),
        'w_up': nrm(ks[20], (DEPTH, N_EXPERTS, D_MODEL, 2 * D_FF), D_MODEL ** -0.5),
        'b_up': nrm(ks[21], (DEPTH, N_EXPERTS, 2 * D_FF), 0.02),
        'w_down': nrm(ks[22], (DEPTH, N_EXPERTS, D_FF, D_MODEL), D_FF ** -0.5 * DEEPNORM_BETA),
        'b_down': nrm(ks[23], (DEPTH, N_EXPERTS, D_MODEL), 0.02),
        'w_ple': nrm(ks[24], (DEPTH, PLE_DIM, D_MODEL), PLE_DIM ** -0.5 * DEEPNORM_BETA),
        'w_plg': nrm(ks[25], (DEPTH, D_MODEL, D_MODEL), D_MODEL ** -0.5),
        'ln2_g': 1.0 + nrm(ks[26], (DEPTH, D_MODEL), 0.02),
        'ln2_b': nrm(ks[27], (DEPTH, D_MODEL), 0.02),
    }


def reference(x_prompt, x_sample, p_prompt, p_sample, cache_nsa_kv, cache_win_kv, state_ret, state_conv, page_table,
              w_in, w_out, conv_w, conv_b, conv_ln_g, conv_ln_b, ret_norm_g, ln1_g, ln1_b,
              w_router, b_router, w_up, b_up, w_down, b_down, w_ple, w_plg, ln2_g, ln2_b):
    bp, seq = x_prompt.shape[:2]
    dec_b, n_pages = page_table.shape
    past_len = n_pages * PAGE_SIZE
    wbuf = cache_win_kv.shape[2]
    dt = x_prompt.dtype
    zeros_kv = jnp.zeros((bp, 0, 4, NSA_KV_HEADS, HEAD_DIM), dt)
    zeros_win = jnp.zeros((bp, WINDOW, 2, NSA_KV_HEADS, HEAD_DIM), dt)
    zeros_ret = jnp.zeros((bp, RET_HEADS, HEAD_DIM, HEAD_DIM), jnp.float32)
    zeros_conv = jnp.zeros((bp, CONV_K - 1, CONV_CH), dt)
    hp, hs = x_prompt, x_sample
    kvp, kvs, wp, ws, rp, rs, cp, cs = [], [], [], [], [], [], [], []
    for i in range(DEPTH):
        lw = (w_in[i], w_out[i], conv_w[i], conv_b[i], conv_ln_g[i], conv_ln_b[i], ret_norm_g[i], ln1_g[i], ln1_b[i],
              w_router[i], b_router[i], w_up[i], b_up[i], w_down[i], b_down[i], w_ple[i], w_plg[i], ln2_g[i], ln2_b[i])
        hp, a_kv, a_win, a_ret, a_conv = decoder_layer(hp, p_prompt[i], zeros_kv, zeros_win, zeros_ret, zeros_conv,
                                                       0, min(WINDOW, seq), *lw)
        past = cache_nsa_kv[i, page_table].reshape(dec_b, past_len, 4, NSA_KV_HEADS, HEAD_DIM)
        win_prefix = jnp.pad(cache_win_kv[i], ((0, 0), (WINDOW - wbuf, 0), (0, 0), (0, 0), (0, 0)))
        hs, b_kv, b_win, b_ret, b_conv = decoder_layer(hs, p_sample[i], past, win_prefix, state_ret[i], state_conv[i],
                                                       past_len, wbuf, *lw)
        kvp.append(a_kv); kvs.append(b_kv); wp.append(a_win); ws.append(b_win)
        rp.append(a_ret); rs.append(b_ret); cp.append(a_conv); cs.append(b_conv)
    return (hp, hs, jnp.stack(kvp), jnp.stack(kvs), jnp.stack(wp), jnp.stack(ws),
            jnp.stack(rp), jnp.stack(rs), jnp.stack(cp), jnp.stack(cs))
```

```python
import functools

import jax
import jax.numpy as jnp
from jax import lax
from jax.experimental import pallas as pl
from jax.experimental.pallas import tpu as pltpu

F32 = jnp.float32
BF16 = jnp.bfloat16
I32 = jnp.int32

HEAD_DIM = 64
CONV_CH = 256
CONV_K = 31
NSA_HEADS = 8
NSA_KV = 2
GQA = 4
BLK = 64
N_SEL = 16
WINDOW = 512
FORCE_SCORE = 1e4
RET_HEADS = 4
RET_CHUNK = 128
N_EXPERTS = 32
TOP_K = 4
SWIGLU_LIMIT = 7.0
SWIGLU_ALPHA = 1.702
PAGE = 128
LN_EPS = 1e-5
NEG_INF = -1e30
TINY = 1e-30
LANES = 128

TOKEN_TILE = 256
MOE_TILE = 256
NSA_TQ = 128
SAMPLE_PAGES_PER_STEP = 16
VMEM_LIMIT = 56 * 1024 * 1024


def _cp(sem):
    return pltpu.CompilerParams(dimension_semantics=sem, vmem_limit_bytes=VMEM_LIMIT)


def _dot(a, b):
    return jnp.dot(a, b, preferred_element_type=F32)


def _dot_nt(a, b):
    return lax.dot_general(a, b, (((1,), (1,)), ((), ())), preferred_element_type=F32)


def _dot_tn(a, b):
    return lax.dot_general(a, b, (((0,), (0,)), ((), ())), preferred_element_type=F32)


def _layer_norm(x, g, b):
    mu = jnp.mean(x, axis=-1, keepdims=True)
    xc = x - mu
    var = jnp.mean(xc * xc, axis=-1, keepdims=True)
    return xc * lax.rsqrt(var + LN_EPS) * g + b


def _sigmoid(x):
    return 1.0 / (1.0 + jnp.exp(-x))


def _inproj_kernel(x_ref, wc_ref, wq_ref, wkv_ref, wg_ref, wr_ref, oc_ref, oq_ref, okv_ref, og_ref, or_ref):
    x = x_ref[...].astype(BF16)
    oc_ref[...] = _dot(x, wc_ref[...])
    oq_ref[...] = _dot(x, wq_ref[...])
    okv_ref[...] = _dot(x, wkv_ref[...])
    og_ref[...] = _dot(x, wg_ref[...])
    or_ref[...] = _dot(x, wr_ref[...])


def _inproj(x_all, ws):
    n, d = x_all.shape
    tm = TOKEN_TILE
    widths = [w.shape[1] for w in ws]
    return pl.pallas_call(
        _inproj_kernel,
        grid=(n // tm,),
        in_specs=[pl.BlockSpec((tm, d), lambda i: (i, 0))]
        + [pl.BlockSpec((d, c), lambda i: (0, 0)) for c in widths],
        out_specs=[pl.BlockSpec((tm, c), lambda i: (i, 0)) for c in widths],
        out_shape=[jax.ShapeDtypeStruct((n, c), F32) for c in widths],
        compiler_params=_cp(("parallel",)),
        name="inproj",
    )(x_all, *ws)


def _conv_kernel(u_ref, s0_ref, w_ref, cb_ref, g_ref, b_ref, o_ref, st_ref, ext, *, tt):
    i = pl.program_id(1)
    pad = 32 - (CONV_K - 1)

    @pl.when(i == 0)
    def _():
        ext[0:pad, :] = jnp.zeros((pad, CONV_CH), F32)
        ext[pad:32, :] = s0_ref[0]

    u = u_ref[...]
    h = u[:, :CONV_CH] * _sigmoid(u[:, CONV_CH:])
    ext[32:32 + tt, :] = h
    acc = jnp.zeros((tt, CONV_CH), F32)
    for j in range(CONV_K):
        acc = acc + ext[pad + j:pad + j + tt, :] * w_ref[j:j + 1, :]
    y = _layer_norm(acc + cb_ref[...], g_ref[...], b_ref[...])
    o_ref[...] = y * _sigmoid(y)
    tail = ext[tt + pad:tt + 32, :]

    @pl.when(i == pl.num_programs(1) - 1)
    def _():
        st_ref[0] = tail

    ext[pad:32, :] = tail


def _conv(u_conv, s0, w, cb, g, b, row0, nb, t):
    tt = min(t, 256)
    nt = t // tt
    base = row0 // tt
    in_specs = [
        pl.BlockSpec((tt, 2 * CONV_CH), lambda bb, i: (base + bb * nt + i, 0)),
        pl.BlockSpec((1, CONV_K - 1, CONV_CH), lambda bb, i: (bb, 0, 0)),
        pl.BlockSpec((32, CONV_CH), lambda bb, i: (0, 0)),
        pl.BlockSpec((1, CONV_CH), lambda bb, i: (0, 0)),
        pl.BlockSpec((1, CONV_CH), lambda bb, i: (0, 0)),
        pl.BlockSpec((1, CONV_CH), lambda bb, i: (0, 0)),
    ]
    args = [u_conv, s0, w, cb, g, b]
    return pl.pallas_call(
        functools.partial(_conv_kernel, tt=tt),
        grid=(nb, nt),
        in_specs=in_specs,
        out_specs=[
            pl.BlockSpec((tt, CONV_CH), lambda bb, i: (bb * nt + i, 0)),
            pl.BlockSpec((1, CONV_K - 1, CONV_CH), lambda bb, i: (bb, 0, 0)),
        ],
        out_shape=[
            jax.ShapeDtypeStruct((nb * t, CONV_CH), F32),
            jax.ShapeDtypeStruct((nb, CONV_K - 1, CONV_CH), F32),
        ],
        scratch_shapes=[pltpu.VMEM((32 + tt, CONV_CH), F32)],
        compiler_params=_cp(("parallel", "arbitrary")),
        name="conv",
    )(*args)


def _ret_kernel(u_ref, s0_ref, dm_ref, qd_ref, kd_ref, cd_ref, g_ref, o_ref, st_ref, s_scr):
    i = pl.program_id(1)

    @pl.when(i == 0)
    def _():
        s_scr[...] = s0_ref[0]

    u = u_ref[...]
    w = RET_HEADS * HEAD_DIM
    outs = []
    for h in range(RET_HEADS):
        lo = h * HEAD_DIM
        q = u[:, lo:lo + HEAD_DIM]
        k = u[:, w + lo:w + lo + HEAD_DIM] * (HEAD_DIM ** -0.5)
        v = u[:, 2 * w + lo:2 * w + lo + HEAD_DIM]
        rg = u[:, 3 * w + lo:3 * w + lo + HEAD_DIM]
        qb, kb, vb = q.astype(BF16), k.astype(BF16), v.astype(BF16)
        att = _dot_nt(qb, kb) * dm_ref[h]
        s_h = s_scr[h]
        o = _dot(att.astype(BF16), vb) + _dot(qb, s_h.astype(BF16)) * qd_ref[h]
        kdec = (k * kd_ref[h]).astype(BF16)
        s_scr[h] = s_h * cd_ref[h] + _dot_tn(kdec, vb)
        mu = jnp.mean(o, axis=-1, keepdims=True)
        oc = o - mu
        var = jnp.mean(oc * oc, axis=-1, keepdims=True)
        on = oc * lax.rsqrt(var + LN_EPS) * g_ref[:, lo:lo + HEAD_DIM]
        outs.append(rg * _sigmoid(rg) * on)
    o_ref[...] = jnp.concatenate(outs, axis=1)

    @pl.when(i == pl.num_programs(1) - 1)
    def _():
        st_ref[0] = s_scr[...]


def _ret_tables(c):
    log_g = jnp.log1p(-jnp.exp2(-5.0 - jnp.arange(RET_HEADS, dtype=F32)))
    i = jnp.arange(c, dtype=F32)
    diff = i[:, None] - i[None, :]
    dmask = jnp.exp(jnp.where(diff >= 0, log_g[:, None, None] * diff, -jnp.inf))
    q_dec = jnp.exp(log_g[:, None] * (i[None, :] + 1.0))
    k_dec = jnp.exp(log_g[:, None] * (c - 1.0 - i[None, :]))
    c_dec = jnp.exp(log_g * c)
    bc = lambda a: jnp.broadcast_to(a[:, :, None], a.shape + (HEAD_DIM,))
    cd = jnp.broadcast_to(c_dec[:, None, None], (RET_HEADS, HEAD_DIM, HEAD_DIM))
    return dmask, bc(q_dec), bc(k_dec), cd


def _retention(u_ret, s0, gain, row0, nb, t):
    c = min(t, RET_CHUNK)
    while t % c:
        c -= 1
    nt = t // c
    base = row0 // c
    dm, qd, kd, cd = _ret_tables(c)
    w = RET_HEADS * HEAD_DIM
    full = lambda shape: pl.BlockSpec(shape, lambda bb, i: (0,) * len(shape))
    in_specs = [
        pl.BlockSpec((c, 4 * w), lambda bb, i: (base + bb * nt + i, 0)),
        pl.BlockSpec((1, RET_HEADS, HEAD_DIM, HEAD_DIM), lambda bb, i: (bb, 0, 0, 0)),
        full((RET_HEADS, c, c)),
        full((RET_HEADS, c, HEAD_DIM)),
        full((RET_HEADS, c, HEAD_DIM)),
        full((RET_HEADS, HEAD_DIM, HEAD_DIM)),
        full((1, w)),
    ]
    args = [u_ret, s0, dm, qd, kd, cd, gain]
    return pl.pallas_call(
        _ret_kernel,
        grid=(nb, nt),
        in_specs=in_specs,
        out_specs=[
            pl.BlockSpec((c, w), lambda bb, i: (bb * nt + i, 0)),
            pl.BlockSpec((1, RET_HEADS, HEAD_DIM, HEAD_DIM), lambda bb, i: (bb, 0, 0, 0)),
        ],
        out_shape=[
            jax.ShapeDtypeStruct((nb * t, w), F32),
            jax.ShapeDtypeStruct((nb, RET_HEADS, HEAD_DIM, HEAD_DIM), F32),
        ],
        scratch_shapes=[pltpu.VMEM((RET_HEADS, HEAD_DIM, HEAD_DIM), F32)],
        compiler_params=_cp(("parallel", "arbitrary")),
        name="retention",
    )(*args)


def _slope(h):
    return 2.0 ** -(h + 1)


def _build_q2(uq, tq):
    lane = lax.broadcasted_iota(I32, (tq, LANES), 1)
    parts = []
    for h in range(NSA_HEADS):
        kvh = h // GQA
        p = uq[:, (h // 2) * LANES:(h // 2 + 1) * LANES]
        if (h % 2) != kvh:
            p = pltpu.roll(p, HEAD_DIM, 1)
        keep = (lane >= HEAD_DIM) if kvh == 1 else (lane < HEAD_DIM)
        parts.append(jnp.where(keep, p * (HEAD_DIM ** -0.5), 0.0))
    return jnp.concatenate(parts, axis=0).astype(BF16)


def _masked_softmax_heads(s3, mask_of_head, dist):
    ps = []
    for h in range(NSA_HEADS):
        mk = mask_of_head(h)
        sh = jnp.where(mk, s3[h] - _slope(h) * dist, NEG_INF)
        mx = jnp.max(sh, axis=-1, keepdims=True)
        e = jnp.where(mk, jnp.exp(sh - mx), 0.0)
        den = jnp.maximum(jnp.sum(e, axis=-1, keepdims=True), TINY)
        ps.append(e / den)
    return ps


def _online_step(s3, mask_of_head, dist, v_bf, m, l, acc, tq):
    ps, alphas, ms, ls = [], [], [], []
    for h in range(NSA_HEADS):
        mk = mask_of_head(h)
        rows = slice(h * tq, (h + 1) * tq)
        sh = jnp.where(mk, s3[h] - _slope(h) * dist, NEG_INF)
        m_old = m[rows]
        m_new = jnp.maximum(m_old, jnp.max(sh, axis=-1, keepdims=True))
        p = jnp.where(mk, jnp.exp(sh - m_new), 0.0)
        alpha = jnp.exp(m_old - m_new)
        ls.append(alpha * l[rows] + jnp.sum(p, axis=-1, keepdims=True))
        ms.append(m_new)
        alphas.append(alpha)
        ps.append(p)
    p_all = jnp.concatenate(ps, axis=0).astype(BF16)
    alpha_all = jnp.concatenate(alphas, axis=0)
    acc = acc * alpha_all + _dot(p_all, v_bf)
    return jnp.concatenate(ms, axis=0), jnp.concatenate(ls, axis=0), acc


def _combine_heads(gl, o_cmp, o_sel, o_win, tq):
    gs = _sigmoid(gl)
    lane = lax.broadcasted_iota(I32, (tq, LANES), 1)
    pairs = []
    for mpair in range(NSA_HEADS // 2):
        halves = []
        for h in (2 * mpair, 2 * mpair + 1):
            rows = slice(h * tq, (h + 1) * tq)
            o = (gs[:, 3 * h:3 * h + 1] * o_cmp[rows] + gs[:, 3 * h + 1:3 * h + 2] * o_sel[rows]
                 + gs[:, 3 * h + 2:3 * h + 3] * o_win[rows])
            if (h // GQA) != (h % 2):
                o = pltpu.roll(o, HEAD_DIM, 1)
            halves.append(o)
        pairs.append(jnp.where(lane < HEAD_DIM, halves[0], halves[1]))
    return jnp.concatenate(pairs, axis=1)


def _cmp_branch(q2, kc_bf, vc_bf, t0, tq, n_lanes):
    s3 = _dot_nt(q2, kc_bf).reshape(NSA_HEADS, tq, n_lanes)
    tpos = t0 + lax.broadcasted_iota(I32, (tq, n_lanes), 0)
    cend = lax.broadcasted_iota(I32, (tq, n_lanes), 1) * BLK + (BLK - 1)
    valid = cend <= tpos
    dist = (tpos - cend).astype(F32)
    ps = _masked_softmax_heads(s3, lambda h: valid, dist)
    o_cmp = _dot(jnp.concatenate(ps, axis=0).astype(BF16), vc_bf)
    return o_cmp, ps


def _block_scores(ps, kv, t0, tq, n_lanes):
    tpos = t0 + lax.broadcasted_iota(I32, (tq, n_lanes), 0)
    nb = lax.broadcasted_iota(I32, (tq, n_lanes), 1)
    cur = tpos >> 6
    imp = ps[kv * GQA] + ps[kv * GQA + 1] + ps[kv * GQA + 2] + ps[kv * GQA + 3]
    forced = (nb == 0) | (nb == cur) | (nb == cur - 1)
    allowed = nb <= cur
    return jnp.where(allowed, jnp.where(forced, FORCE_SCORE, imp), -1.0), allowed, nb


def _nsa_prompt_kernel(q_ref, g_ref, kv_ref, o_ref, kb_ref, kc_ref, e_ref, selm_ref, m_ref, l_ref, acc_ref,
                       *, t, tq):
    i = pl.program_id(1)
    n_blk = t // BLK
    rows_all = NSA_HEADS * tq

    @pl.when(i == 0)
    def _():
        kc_ref[...] = jnp.zeros(kc_ref.shape, F32)
        step = 256
        for c in range(t // step):
            sl = slice(c * step, (c + 1) * step)
            kb_ref[sl, :] = kv_ref[sl, 2 * LANES:6 * LANES].astype(BF16)
            kc_ref[c * (step // BLK):(c + 1) * (step // BLK), :] = (
                kv_ref[sl, 0:2 * LANES].reshape(step // BLK, BLK, 2 * LANES).sum(axis=1) * (1.0 / BLK))
        blk_of_key = lax.broadcasted_iota(I32, (LANES, t), 1) >> 6
        e_ref[...] = jnp.where(blk_of_key == lax.broadcasted_iota(I32, (LANES, t), 0), 1.0, 0.0).astype(BF16)

    t0 = i * tq
    q2 = _build_q2(q_ref[...], tq)
    kc = kc_ref[...]
    o_cmp, ps = _cmp_branch(q2, kc[:, :LANES].astype(BF16), kc[:, LANES:].astype(BF16), t0, tq, LANES)

    e_bf = e_ref[...]
    for kv in range(NSA_KV):
        score, allowed, nb = _block_scores(ps, kv, t0, tq, LANES)
        rank = jnp.zeros((tq, LANES), F32)
        for mblk in range(n_blk):
            c = score[:, mblk:mblk + 1]
            beats = (c > score) | ((c == score) & (nb > mblk))
            rank = rank + jnp.where(beats, 1.0, 0.0)
        sel = jnp.where((rank < N_SEL) & allowed, 1.0, 0.0).astype(BF16)
        selm_ref[kv] = _dot(sel, e_bf)

    m_ref[...] = jnp.full((rows_all, LANES), NEG_INF, F32)
    l_ref[...] = jnp.zeros((rows_all, LANES), F32)
    acc_ref[...] = jnp.zeros((rows_all, LANES), F32)
    tpos = t0 + lax.broadcasted_iota(I32, (tq, tq), 0)

    def body(kt, carry):
        k0 = pl.multiple_of(kt * tq, tq)
        k_bf = kb_ref[pl.ds(k0, tq), 0:LANES]
        v_bf = kb_ref[pl.ds(k0, tq), LANES:2 * LANES]
        s3 = _dot_nt(q2, k_bf).reshape(NSA_HEADS, tq, tq)
        pos = k0 + lax.broadcasted_iota(I32, (tq, tq), 1)
        causal = pos <= tpos
        dist = (tpos - pos).astype(F32)
        masks = [(selm_ref[kv, :, pl.ds(k0, tq)] > 0.5) & causal for kv in range(NSA_KV)]
        m, l, acc = _online_step(s3, lambda h: masks[h // GQA], dist, v_bf,
                                 m_ref[...], l_ref[...], acc_ref[...], tq)
        m_ref[...] = m
        l_ref[...] = l
        acc_ref[...] = acc
        return carry

    lax.fori_loop(0, i + 1, body, 0)
    o_sel = acc_ref[...] / jnp.maximum(l_ref[...], TINY)

    wl = WINDOW + tq
    ws = pl.multiple_of(jnp.maximum(t0 - WINDOW, 0), tq)
    kw = kb_ref[pl.ds(ws, wl), 2 * LANES:3 * LANES]
    vw = kb_ref[pl.ds(ws, wl), 3 * LANES:4 * LANES]
    s3 = _dot_nt(q2, kw).reshape(NSA_HEADS, tq, wl)
    dw = (t0 + lax.broadcasted_iota(I32, (tq, wl), 0)) - (ws + lax.broadcasted_iota(I32, (tq, wl), 1))
    mw = (dw >= 0) & (dw < WINDOW)
    pw = _masked_softmax_heads(s3, lambda h: mw, dw.astype(F32))
    o_win = _dot(jnp.concatenate(pw, axis=0).astype(BF16), vw)

    o_ref[...] = _combine_heads(g_ref[...], o_cmp, o_sel, o_win, tq)


def _nsa_prompt(u_q, u_gate, u_kv, nb, t):
    tq = NSA_TQ
    nt = t // tq
    wq = NSA_HEADS * HEAD_DIM
    in_specs = [
        pl.BlockSpec((tq, wq), lambda bb, i: (bb * nt + i, 0)),
        pl.BlockSpec((tq, LANES), lambda bb, i: (bb * nt + i, 0)),
        pl.BlockSpec((t, 6 * LANES), lambda bb, i: (bb, 0)),
    ]
    args = [u_q, u_gate, u_kv]
    rows_all = NSA_HEADS * tq
    return pl.pallas_call(
        functools.partial(_nsa_prompt_kernel, t=t, tq=tq),
        grid=(nb, nt),
        in_specs=in_specs,
        out_specs=pl.BlockSpec((tq, wq), lambda bb, i: (bb * nt + i, 0)),
        out_shape=jax.ShapeDtypeStruct((nb * t, wq), F32),
        scratch_shapes=[
            pltpu.VMEM((t, 4 * LANES), BF16),
            pltpu.VMEM((LANES, 2 * LANES), F32),
            pltpu.VMEM((LANES, t), BF16),
            pltpu.VMEM((NSA_KV, tq, t), F32),
            pltpu.VMEM((rows_all, LANES), F32),
            pltpu.VMEM((rows_all, LANES), F32),
            pltpu.VMEM((rows_all, LANES), F32),
        ],
        compiler_params=_cp(("parallel", "arbitrary")),
        name="nsa_prompt",
    )(*args)


def _nsa_sample_cmp_kernel(pt_ref, *refs, pp, tq, past_len):
    page_refs = refs[:pp]
    q_ref, ocmp_ref, selm_ref, cmp_scr = refs[pp:]
    j = pl.program_id(1)
    n_cmp = past_len // BLK
    rows = []
    for k in range(pp):
        pg = page_refs[k][...]
        rows.append(pg.reshape(PAGE // BLK, BLK, 2 * LANES).sum(axis=1) * (1.0 / BLK))
    per_step = pp * (PAGE // BLK)
    cmp_scr[pl.ds(pl.multiple_of(j * per_step, per_step), per_step), :] = jnp.concatenate(rows, axis=0)

    @pl.when(j == pl.num_programs(1) - 1)
    def _():
        q2 = _build_q2(q_ref[...], tq)
        kc = cmp_scr[...]
        o_cmp, ps = _cmp_branch(q2, kc[:, :LANES].astype(BF16), kc[:, LANES:].astype(BF16), past_len, tq, n_cmp)
        ocmp_ref[0] = o_cmp
        scores = [_block_scores(ps, kv, past_len, tq, n_cmp)[0] for kv in range(NSA_KV)]
        sc = jnp.concatenate(scores, axis=0)
        idx = lax.broadcasted_iota(I32, sc.shape, 1)
        sel = jnp.zeros(sc.shape, F32)
        for _ in range(N_SEL - 1):
            mx = jnp.max(sc, axis=-1, keepdims=True)
            am = jnp.min(jnp.where(sc == mx, idx, n_cmp), axis=-1, keepdims=True)
            hit = idx == am
            sel = jnp.where(hit, 1.0, sel)
            sc = jnp.where(hit, -2.0, sc)
        selm_ref[0] = sel


def _nsa_sample_cmp(cache, layer, page_table, u_q, row0, nb, tq):
    n_pages = page_table.shape[1]
    past_len = n_pages * PAGE
    n_cmp = past_len // BLK
    pp = min(SAMPLE_PAGES_PER_STEP, n_pages)
    steps = n_pages // pp
    wq = NSA_HEADS * HEAD_DIM
    base = row0 // tq

    def page_spec(k):
        return pl.BlockSpec((None, None, PAGE, 2 * LANES),
                            lambda bb, j, pt: (layer, pt[bb, j * pp + k], 0, 0))

    grid_spec = pltpu.PrefetchScalarGridSpec(
        num_scalar_prefetch=1,
        grid=(nb, steps),
        in_specs=[page_spec(k) for k in range(pp)]
        + [pl.BlockSpec((tq, wq), lambda bb, j, pt: (base + bb, 0))],
        out_specs=[
            pl.BlockSpec((1, NSA_HEADS * tq, LANES), lambda bb, j, pt: (bb, 0, 0)),
            pl.BlockSpec((1, NSA_KV * tq, n_cmp), lambda bb, j, pt: (bb, 0, 0)),
        ],
        scratch_shapes=[pltpu.VMEM((n_cmp, 2 * LANES), F32)],
    )
    return pl.pallas_call(
        functools.partial(_nsa_sample_cmp_kernel, pp=pp, tq=tq, past_len=past_len),
        grid_spec=grid_spec,
        out_shape=[
            jax.ShapeDtypeStruct((nb, NSA_HEADS * tq, LANES), F32),
            jax.ShapeDtypeStruct((nb, NSA_KV * tq, n_cmp), F32),
        ],
        compiler_params=_cp(("parallel", "arbitrary")),
        name="nsa_sample_cmp",
    )(page_table, *([cache] * pp), u_q)


def _nsa_sample_sel_kernel(pages_ref, lpage_ref, cnt_ref, cache_ref, selm_ref, q_ref, g_ref, kvn_ref, ocmp_ref,
                           win_ref, o_ref, buf, sem, *, layer, tq, past_len):
    b = pl.program_id(0)
    n = cnt_ref[b]
    n_cmp = past_len // BLK
    rows_all = NSA_HEADS * tq

    def page_copy(slot, idx):
        return pltpu.make_async_copy(
            cache_ref.at[layer, pages_ref[b, idx], :, pl.ds(2 * LANES, 2 * LANES)], buf.at[slot], sem.at[slot])

    @pl.when(n > 0)
    def _():
        page_copy(0, 0).start()

    q2 = _build_q2(q_ref[...], tq)
    selm = selm_ref[0]
    blk_idx = lax.broadcasted_iota(I32, selm.shape, 1)
    key = lax.broadcasted_iota(I32, (tq, LANES), 1)
    tpos = past_len + lax.broadcasted_iota(I32, (tq, LANES), 0)
    key16 = lax.broadcasted_iota(I32, (NSA_KV * tq, LANES), 1)

    def body(it, carry):
        m, l, acc = carry
        slot = it & 1

        @pl.when(it + 1 < n)
        def _():
            page_copy(1 - slot, it + 1).start()

        page_copy(slot, it).wait()
        pg = buf[slot]
        lp = lpage_ref[b, it]
        pos = lp * PAGE + key
        dist = (tpos - pos).astype(F32)
        sel_lo = jnp.sum(jnp.where(blk_idx == 2 * lp, selm, 0.0), axis=-1, keepdims=True)
        sel_hi = jnp.sum(jnp.where(blk_idx == 2 * lp + 1, selm, 0.0), axis=-1, keepdims=True)
        mk16 = jnp.where(key16 < BLK, sel_lo, sel_hi) > 0.5
        masks = [mk16[kv * tq:(kv + 1) * tq] for kv in range(NSA_KV)]
        s3 = _dot_nt(q2, pg[:, :LANES].astype(BF16)).reshape(NSA_HEADS, tq, LANES)
        return _online_step(s3, lambda h: masks[h // GQA], dist, pg[:, LANES:].astype(BF16), m, l, acc, tq)

    init = (jnp.full((rows_all, LANES), NEG_INF, F32), jnp.zeros((rows_all, LANES), F32),
            jnp.zeros((rows_all, LANES), F32))
    m, l, acc = lax.fori_loop(0, n, body, init)

    kvn = kvn_ref[...]
    zpad = jnp.zeros((LANES - tq, LANES), F32)
    k_new = jnp.concatenate([kvn[:, 2 * LANES:3 * LANES], zpad], axis=0).astype(BF16)
    v_new = jnp.concatenate([kvn[:, 3 * LANES:4 * LANES], zpad], axis=0).astype(BF16)
    pos = past_len + key
    mk_new = (key < tq) & (pos <= tpos)
    s3 = _dot_nt(q2, k_new).reshape(NSA_HEADS, tq, LANES)
    m, l, acc = _online_step(s3, lambda h: mk_new, (tpos - pos).astype(F32), v_new, m, l, acc, tq)
    o_sel = acc / jnp.maximum(l, TINY)

    win = win_ref[0]
    wl = WINDOW + LANES
    kw = jnp.concatenate([win[:, :LANES], kvn[:, 4 * LANES:5 * LANES], zpad], axis=0).astype(BF16)
    vw = jnp.concatenate([win[:, LANES:], kvn[:, 5 * LANES:6 * LANES], zpad], axis=0).astype(BF16)
    widx = lax.broadcasted_iota(I32, (tq, wl), 1)
    pw = past_len - WINDOW + widx
    dw = (past_len + lax.broadcasted_iota(I32, (tq, wl), 0)) - pw
    mw = (dw >= 0) & (dw < WINDOW) & (pw >= 0) & (widx < WINDOW + tq)
    s3 = _dot_nt(q2, kw).reshape(NSA_HEADS, tq, wl)
    pws = _masked_softmax_heads(s3, lambda h: mw, dw.astype(F32))
    o_win = _dot(jnp.concatenate(pws, axis=0).astype(BF16), vw)

    o_ref[...] = _combine_heads(g_ref[...], ocmp_ref[0], o_sel, o_win, tq)


def _nsa_sample_sel(cache, layer, pages, lpages, cnt, selm, u_q, u_gate, u_kv, o_cmp, win_cache,
                    row0, nb, tq, past_len):
    wq = NSA_HEADS * HEAD_DIM
    base = row0 // tq
    n_cmp = past_len // BLK
    grid_spec = pltpu.PrefetchScalarGridSpec(
        num_scalar_prefetch=3,
        grid=(nb,),
        in_specs=[
            pl.BlockSpec(memory_space=pl.ANY),
            pl.BlockSpec((1, NSA_KV * tq, n_cmp), lambda bb, *_: (bb, 0, 0)),
            pl.BlockSpec((tq, wq), lambda bb, *_: (base + bb, 0)),
            pl.BlockSpec((tq, LANES), lambda bb, *_: (base + bb, 0)),
            pl.BlockSpec((tq, 6 * LANES), lambda bb, *_: (base + bb, 0)),
            pl.BlockSpec((1, NSA_HEADS * tq, LANES), lambda bb, *_: (bb, 0, 0)),
            pl.BlockSpec((1, WINDOW, 2 * LANES), lambda bb, *_: (bb, 0, 0)),
        ],
        out_specs=pl.BlockSpec((tq, wq), lambda bb, *_: (bb, 0)),
        scratch_shapes=[pltpu.VMEM((2, PAGE, 2 * LANES), F32), pltpu.SemaphoreType.DMA((2,))],
    )
    return pl.pallas_call(
        functools.partial(_nsa_sample_sel_kernel, layer=layer, tq=tq, past_len=past_len),
        grid_spec=grid_spec,
        out_shape=jax.ShapeDtypeStruct((nb * tq, wq), F32),
        compiler_params=_cp(("arbitrary",)),
        name="nsa_sample_sel",
    )(pages, lpages, cnt, cache, selm, u_q, u_gate, u_kv, o_cmp, win_cache)


def _outproj_kernel(x_ref, cp_ref, cs_ref, ap_ref, as_ref, rp_ref, rs_ref, wo_ref, g_ref, b_ref, wr_ref, br_ref,
                    h_ref, hb_ref, te_ref, tg_ref, *, alpha, n_prompt_tiles):
    is_sample = pl.program_id(0) >= n_prompt_tiles
    pick = lambda p_ref, s_ref: jnp.where(is_sample, s_ref[...], p_ref[...]).astype(BF16)
    mix = (_dot(pick(cp_ref, cs_ref), wo_ref[0:CONV_CH, :])
           + _dot(pick(ap_ref, as_ref), wo_ref[CONV_CH:CONV_CH + NSA_HEADS * HEAD_DIM, :])
           + _dot(pick(rp_ref, rs_ref), wo_ref[CONV_CH + NSA_HEADS * HEAD_DIM:, :]))
    h = _layer_norm(alpha * x_ref[...] + mix, g_ref[...], b_ref[...])
    h_ref[...] = h
    hb_ref[...] = h.astype(BF16)
    logits = jnp.dot(h, wr_ref[...], preferred_element_type=F32, precision=lax.Precision.HIGHEST) + br_ref[...]
    idx = lax.broadcasted_iota(I32, logits.shape, 1)
    col = lax.broadcasted_iota(I32, (logits.shape[0], TOP_K), 1)
    vals = jnp.zeros((logits.shape[0], TOP_K), F32)
    ids = jnp.zeros((logits.shape[0], TOP_K), I32)
    cur = logits
    for k in range(TOP_K):
        mx = jnp.max(cur, axis=-1, keepdims=True)
        am = jnp.min(jnp.where(cur == mx, idx, N_EXPERTS), axis=-1, keepdims=True)
        vals = jnp.where(col == k, mx, vals)
        ids = jnp.where(col == k, am, ids)
        cur = jnp.where(idx == am, -jnp.inf, cur)
    e = jnp.exp(vals - vals[:, 0:1])
    tg_ref[...] = e / jnp.sum(e, axis=-1, keepdims=True)
    te_ref[...] = ids


def _outproj(x_all, mixers, w_out_bf, g, b, w_router, b_router, alpha):
    n, d = x_all.shape
    tm = TOKEN_TILE
    npt = mixers[0][0].shape[0] // tm
    row = lambda c: pl.BlockSpec((tm, c), lambda i: (i, 0))
    row_p = lambda c: pl.BlockSpec((tm, c), lambda i: (jnp.minimum(i, npt - 1), 0))
    row_s = lambda c: pl.BlockSpec((tm, c), lambda i: (jnp.maximum(i - npt, 0), 0))
    full = lambda r, c: pl.BlockSpec((r, c), lambda i: (0, 0))
    mix_specs, mix_args = [], []
    for a_p, a_s in mixers:
        mix_specs += [row_p(a_p.shape[1]), row_s(a_s.shape[1])]
        mix_args += [a_p, a_s]
    return pl.pallas_call(
        functools.partial(_outproj_kernel, alpha=alpha, n_prompt_tiles=npt),
        grid=(n // tm,),
        in_specs=[row(d)] + mix_specs + [
                  full(d, d), full(1, d), full(1, d), full(d, N_EXPERTS), full(1, N_EXPERTS)],
        out_specs=[row(d), row(d), row(TOP_K), row(TOP_K)],
        out_shape=[jax.ShapeDtypeStruct((n, d), F32), jax.ShapeDtypeStruct((n, d), BF16),
                   jax.ShapeDtypeStruct((n, TOP_K), I32), jax.ShapeDtypeStruct((n, TOP_K), F32)],
        compiler_params=_cp(("parallel",)),
        name="outproj",
    )(x_all, *mix_args, w_out_bf, g, b, w_router, b_router)


def _moe_kernel(be_ref, nu_ref, x_ref, wg_ref, wu_ref, bg_ref, bu_ref, wd_ref, bd_ref, rg_ref, o_ref):
    i = pl.program_id(0)

    @pl.when(i < nu_ref[0])
    def _():
        x = x_ref[...]
        g = jnp.minimum(_dot(x, wg_ref[...]) + bg_ref[...], SWIGLU_LIMIT)
        u = jnp.clip(_dot(x, wu_ref[...]) + bu_ref[...], -SWIGLU_LIMIT, SWIGLU_LIMIT)
        act = (u + 1.0) * g * _sigmoid(SWIGLU_ALPHA * g)
        y = _dot(act.astype(BF16), wd_ref[...]) + bd_ref[...]
        o_ref[...] = y * rg_ref[...]

    @pl.when(i >= nu_ref[0])
    def _():
        o_ref[...] = jnp.zeros(o_ref.shape, F32)


def _moe_blocks(xs, block_e, n_used, wg, wu, bg, bu, wd, bd, row_gate):
    r, d = xs.shape
    bm = MOE_TILE
    dff = wg.shape[2]
    n_blocks = r // bm
    wspec = lambda a, c: pl.BlockSpec((None, a, c), lambda i, be, nu: (be[i], 0, 0))
    grid_spec = pltpu.PrefetchScalarGridSpec(
        num_scalar_prefetch=2,
        grid=(n_blocks,),
        in_specs=[
            pl.BlockSpec((bm, d), lambda i, be, nu: (i, 0)),
            wspec(d, dff), wspec(d, dff), wspec(1, dff), wspec(1, dff), wspec(dff, d), wspec(1, d),
            pl.BlockSpec((bm, 1), lambda i, be, nu: (i, 0)),
        ],
        out_specs=pl.BlockSpec((bm, d), lambda i, be, nu: (i, 0)),
    )
    return pl.pallas_call(
        _moe_kernel,
        grid_spec=grid_spec,
        out_shape=jax.ShapeDtypeStruct((r, d), F32),
        compiler_params=_cp(("arbitrary",)),
        name="moe_experts",
    )(block_e, n_used, xs, wg, wu, bg, bu, wd, bd, row_gate)


def _moe(h_bf, top_e, gate, wg, wu, bg, bu, wd, bd):
    n, d = h_bf.shape
    bm = MOE_TILE
    nk = n * TOP_K
    flat_e = top_e.reshape(nk)
    order = jnp.argsort(flat_e)
    se = flat_e[order]
    counts = jnp.bincount(flat_e, length=N_EXPERTS)
    starts = jnp.cumsum(counts) - counts
    padded = (counts + bm - 1) // bm * bm
    pend = jnp.cumsum(padded)
    pstart = pend - padded
    dest = (pstart[se] + jnp.arange(nk) - starts[se]).astype(I32)
    n_blocks = -(-(nk + N_EXPERTS * (bm - 1)) // bm)
    r = n_blocks * bm
    row_tok = jnp.full((r,), n, I32).at[dest].set((order // TOP_K).astype(I32))
    row_gate = jnp.zeros((r,), F32).at[dest].set(gate.reshape(nk)[order])
    block_e = jnp.minimum(jnp.searchsorted(pend, jnp.arange(n_blocks) * bm, side='right'),
                          N_EXPERTS - 1).astype(I32)
    n_used = (pend[-1] // bm).astype(I32).reshape(1)
    slot = jnp.zeros((nk,), I32).at[order].set(dest)
    xpad = jnp.concatenate([h_bf, jnp.zeros((1, d), h_bf.dtype)], axis=0)
    xs = xpad[row_tok]
    yb = _moe_blocks(xs, block_e, n_used, wg, wu, bg, bu, wd, bd, row_gate.reshape(r, 1))
    return yb[slot.reshape(n, TOP_K)].sum(axis=1)


def _final_kernel(h_ref, f_ref, p_ref, wple_ref, wplg_ref, g_ref, b_ref, y_ref, *, alpha):
    h = h_ref[...]
    ple = _dot(p_ref[...].astype(BF16), wple_ref[...]) * _sigmoid(_dot(h.astype(BF16), wplg_ref[...]))
    y_ref[...] = _layer_norm(alpha * h + f_ref[...] + ple, g_ref[...], b_ref[...])


def _final(h, ffn, p_all, w_ple_bf, w_plg_bf, g, b, alpha):
    n, d = h.shape
    tm = TOKEN_TILE
    pd = p_all.shape[1]
    row = lambda c: pl.BlockSpec((tm, c), lambda i: (i, 0))
    full = lambda r, c: pl.BlockSpec((r, c), lambda i: (0, 0))
    return pl.pallas_call(
        functools.partial(_final_kernel, alpha=alpha),
        grid=(n // tm,),
        in_specs=[row(d), row(d), row(pd), full(pd, d), full(d, d), full(1, d), full(1, d)],
        out_specs=row(d),
        out_shape=jax.ShapeDtypeStruct((n, d), F32),
        compiler_params=_cp(("parallel",)),
        name="final",
    )(h, ffn, p_all, w_ple_bf, w_plg_bf, g, b)


def kernel(x_prompt, x_sample, p_prompt, p_sample, cache_nsa_kv, cache_win_kv, state_ret, state_conv, page_table,
           w_in, w_out, conv_w, conv_b, conv_ln_g, conv_ln_b, ret_norm_g, ln1_g, ln1_b, w_router, b_router,
           w_up, b_up, w_down, b_down, w_ple, w_plg, ln2_g, ln2_b):
    bp, seq, d = x_prompt.shape
    bs, tdec, _ = x_sample.shape
    depth = w_in.shape[0]
    n_pages = page_table.shape[1]
    past_len = n_pages * PAGE
    wbuf = cache_win_kv.shape[2]
    n_p, n_s = bp * seq, bs * tdec
    assert wbuf == WINDOW and tdec <= BLK and tdec % 8 == 0 and past_len // BLK >= N_SEL
    assert seq % NSA_TQ == 0 and seq >= WINDOW + NSA_TQ and n_p % TOKEN_TILE == 0 and n_s % TOKEN_TILE == 0
    alpha = (2 * depth) ** 0.25
    n_pool = cache_nsa_kv.shape[1]
    cache = cache_nsa_kv.reshape(depth, n_pool, PAGE, 4 * LANES)
    page_table = page_table.astype(I32)

    c_conv = 2 * CONV_CH
    c_q = NSA_HEADS * HEAD_DIM
    c_kv = 6 * NSA_KV * HEAD_DIM
    c_gate = 3 * NSA_HEADS
    cuts = [0, c_conv, c_conv + c_q, c_conv + c_q + c_kv, c_conv + c_q + c_kv + c_gate, w_in.shape[2]]

    x_all = jnp.concatenate([x_prompt.reshape(n_p, d), x_sample.reshape(n_s, d)], axis=0)
    outs = {k: [] for k in ("kvp", "kvs", "wp", "ws", "rp", "rs", "cp", "cs")}
    zeros_ret = jnp.zeros((bp, RET_HEADS, HEAD_DIM, HEAD_DIM), F32)
    zeros_conv = jnp.zeros((bp, CONV_K - 1, CONV_CH), F32)

    for i in range(depth):
        wi = w_in[i].astype(BF16)
        ws = [wi[:, cuts[j]:cuts[j + 1]] for j in range(5)]
        ws[3] = jnp.pad(ws[3], ((0, 0), (0, LANES - c_gate)))
        u_conv, u_q, u_kv, u_gate, u_ret = _inproj(x_all, ws)

        cw = jnp.pad(conv_w[i], ((0, 32 - CONV_K), (0, 0)))
        cargs = (cw, conv_b[i][None], conv_ln_g[i][None], conv_ln_b[i][None])
        conv_p, conv_sp = _conv(u_conv, zeros_conv, *cargs, 0, bp, seq)
        conv_s, conv_ss = _conv(u_conv, state_conv[i], *cargs, n_p, bs, tdec)

        rgain = ret_norm_g[i][None]
        ret_p, ret_sp = _retention(u_ret, zeros_ret, rgain, 0, bp, seq)
        ret_s, ret_ss = _retention(u_ret, state_ret[i], rgain, n_p, bs, tdec)

        nsa_p = _nsa_prompt(u_q, u_gate, u_kv, bp, seq)
        o_cmp, selm = _nsa_sample_cmp(cache, i, page_table, u_q, n_p, bs, tdec)
        need = selm.reshape(bs, NSA_KV * tdec, n_pages, PAGE // BLK).max(axis=(1, 3)) > 0.5
        order = jnp.argsort(jnp.logical_not(need), axis=1, stable=True).astype(I32)
        cnt = need.sum(axis=1).astype(I32)
        pages = jnp.take_along_axis(page_table, order, axis=1)
        win_cache = cache_win_kv[i].reshape(bs, wbuf, 2 * LANES)
        nsa_s = _nsa_sample_sel(cache, i, pages, order, cnt, selm, u_q, u_gate, u_kv, o_cmp, win_cache,
                                n_p, bs, tdec, past_len)

        h, h_bf, top_e, gate = _outproj(x_all, ((conv_p, conv_s), (nsa_p, nsa_s), (ret_p, ret_s)),
                                        w_out[i].astype(BF16),
                                        ln1_g[i][None], ln1_b[i][None], w_router[i], b_router[i][None], alpha)

        wup = w_up[i].astype(BF16)
        ffn = _moe(h_bf, top_e, gate, wup[:, :, 0::2], wup[:, :, 1::2],
                   b_up[i][:, None, 0::2], b_up[i][:, None, 1::2],
                   w_down[i].astype(BF16), b_down[i][:, None, :])

        p_all = jnp.concatenate([p_prompt[i].reshape(n_p, -1), p_sample[i].reshape(n_s, -1)], axis=0)
        x_all = _final(h, ffn, p_all, w_ple[i].astype(BF16), w_plg[i].astype(BF16),
                       ln2_g[i][None], ln2_b[i][None], alpha)

        kv_shape = (4, NSA_KV, HEAD_DIM)
        outs["kvp"].append(u_kv[:n_p, :4 * LANES].reshape(bp, seq, *kv_shape))
        outs["kvs"].append(u_kv[n_p:, :4 * LANES].reshape(bs, tdec, *kv_shape))
        win_new_p = u_kv[:n_p, 4 * LANES:].reshape(bp, seq, 2, NSA_KV, HEAD_DIM)
        outs["wp"].append(win_new_p[:, seq - min(WINDOW, seq):])
        win_new_s = u_kv[n_p:, 4 * LANES:].reshape(bs, tdec, 2, NSA_KV, HEAD_DIM)
        outs["ws"].append(jnp.concatenate([cache_win_kv[i][:, tdec:], win_new_s], axis=1))
        outs["rp"].append(ret_sp)
        outs["rs"].append(ret_ss)
        outs["cp"].append(conv_sp)
        outs["cs"].append(conv_ss)

    y_p = x_all[:n_p].reshape(bp, seq, d)
    y_s = x_all[n_p:].reshape(bs, tdec, d)
    st = lambda k: jnp.stack(outs[k])
    return (y_p, y_s, st("kvp"), st("kvs"), st("wp"), st("ws"), st("rp"), st("rs"), st("cp"), st("cs"))
```

```python
import functools

import jax
import jax.numpy as jnp
from jax import lax
from jax.experimental import pallas as pl
from jax.experimental.pallas import tpu as pltpu

F32 = jnp.float32
BF16 = jnp.bfloat16
I32 = jnp.int32

HEAD_DIM = 64
CONV_CH = 256
CONV_K = 31
NSA_HEADS = 8
NSA_KV = 2
GQA = 4
BLK = 64
N_SEL = 16
WINDOW = 512
FORCE_SCORE = 1e4
RET_HEADS = 4
RET_CHUNK = 128
N_EXPERTS = 32
TOP_K = 4
SWIGLU_LIMIT = 7.0
SWIGLU_ALPHA = 1.702
PAGE = 128
LN_EPS = 1e-5
NEG_INF = -1e30
TINY = 1e-30
LANES = 128

TOKEN_TILE = 256
MOE_TILE = 256
NSA_TQ = 128
SAMPLE_PAGES_PER_STEP = 16
VMEM_LIMIT = 56 * 1024 * 1024


def _cp(sem):
    return pltpu.CompilerParams(dimension_semantics=sem, vmem_limit_bytes=VMEM_LIMIT)


def _dot(a, b):
    return jnp.dot(a, b, preferred_element_type=F32)


def _dot_nt(a, b):
    return lax.dot_general(a, b, (((1,), (1,)), ((), ())), preferred_element_type=F32)


def _dot_tn(a, b):
    return lax.dot_general(a, b, (((0,), (0,)), ((), ())), preferred_element_type=F32)


def _layer_norm(x, g, b):
    mu = jnp.mean(x, axis=-1, keepdims=True)
    xc = x - mu
    var = jnp.mean(xc * xc, axis=-1, keepdims=True)
    return xc * lax.rsqrt(var + LN_EPS) * g + b


def _sigmoid(x):
    return 1.0 / (1.0 + jnp.exp(-x))


def _inproj_kernel(x_ref, wc_ref, wq_ref, wkv_ref, wg_ref, wr_ref, oc_ref, oq_ref, okv_ref, og_ref, or_ref):
    x = x_ref[...].astype(BF16)
    oc_ref[...] = _dot(x, wc_ref[...])
    oq_ref[...] = _dot(x, wq_ref[...])
    okv_ref[...] = _dot(x, wkv_ref[...])
    og_ref[...] = _dot(x, wg_ref[...])
    or_ref[...] = _dot(x, wr_ref[...])


def _inproj(x_all, ws):
    n, d = x_all.shape
    tm = TOKEN_TILE
    widths = [w.shape[1] for w in ws]
    return pl.pallas_call(
        _inproj_kernel,
        grid=(n // tm,),
        in_specs=[pl.BlockSpec((tm, d), lambda i: (i, 0))]
        + [pl.BlockSpec((d, c), lambda i: (0, 0)) for c in widths],
        out_specs=[pl.BlockSpec((tm, c), lambda i: (i, 0)) for c in widths],
        out_shape=[jax.ShapeDtypeStruct((n, c), F32) for c in widths],
        compiler_params=_cp(("parallel",)),
        name="inproj",
    )(x_all, *ws)


def _conv_kernel(u_ref, s0_ref, w_ref, cb_ref, g_ref, b_ref, o_ref, st_ref, ext, *, tt):
    i = pl.program_id(1)
    pad = 32 - (CONV_K - 1)

    @pl.when(i == 0)
    def _():
        ext[0:pad, :] = jnp.zeros((pad, CONV_CH), F32)
        ext[pad:32, :] = s0_ref[0]

    u = u_ref[...]
    h = u[:, :CONV_CH] * _sigmoid(u[:, CONV_CH:])
    ext[32:32 + tt, :] = h
    acc = jnp.zeros((tt, CONV_CH), F32)
    for j in range(CONV_K):
        acc = acc + ext[pad + j:pad + j + tt, :] * w_ref[j:j + 1, :]
    y = _layer_norm(acc + cb_ref[...], g_ref[...], b_ref[...])
    o_ref[...] = y * _sigmoid(y)
    tail = ext[tt + pad:tt + 32, :]

    @pl.when(i == pl.num_programs(1) - 1)
    def _():
        st_ref[0] = tail

    ext[pad:32, :] = tail


def _conv(u_conv, s0, w, cb, g, b, row0, nb, t):
    tt = min(t, 256)
    nt = t // tt
    base = row0 // tt
    in_specs = [
        pl.BlockSpec((tt, 2 * CONV_CH), lambda bb, i: (base + bb * nt + i, 0)),
        pl.BlockSpec((1, CONV_K - 1, CONV_CH), lambda bb, i: (bb, 0, 0)),
        pl.BlockSpec((32, CONV_CH), lambda bb, i: (0, 0)),
        pl.BlockSpec((1, CONV_CH), lambda bb, i: (0, 0)),
        pl.BlockSpec((1, CONV_CH), lambda bb, i: (0, 0)),
        pl.BlockSpec((1, CONV_CH), lambda bb, i: (0, 0)),
    ]
    args = [u_conv, s0, w, cb, g, b]
    return pl.pallas_call(
        functools.partial(_conv_kernel, tt=tt),
        grid=(nb, nt),
        in_specs=in_specs,
        out_specs=[
            pl.BlockSpec((tt, CONV_CH), lambda bb, i: (bb * nt + i, 0)),
            pl.BlockSpec((1, CONV_K - 1, CONV_CH), lambda bb, i: (bb, 0, 0)),
        ],
        out_shape=[
            jax.ShapeDtypeStruct((nb * t, CONV_CH), F32),
            jax.ShapeDtypeStruct((nb, CONV_K - 1, CONV_CH), F32),
        ],
        scratch_shapes=[pltpu.VMEM((32 + tt, CONV_CH), F32)],
        compiler_params=_cp(("parallel", "arbitrary")),
        name="conv",
    )(*args)


def _ret_kernel(u_ref, s0_ref, dm_ref, qd_ref, kd_ref, cd_ref, g_ref, o_ref, st_ref, s_scr):
    i = pl.program_id(1)

    @pl.when(i == 0)
    def _():
        s_scr[...] = s0_ref[0]

    u = u_ref[...]
    w = RET_HEADS * HEAD_DIM
    outs = []
    for h in range(RET_HEADS):
        lo = h * HEAD_DIM
        q = u[:, lo:lo + HEAD_DIM]
        k = u[:, w + lo:w + lo + HEAD_DIM] * (HEAD_DIM ** -0.5)
        v = u[:, 2 * w + lo:2 * w + lo + HEAD_DIM]
        rg = u[:, 3 * w + lo:3 * w + lo + HEAD_DIM]
        qb, kb, vb = q.astype(BF16), k.astype(BF16), v.astype(BF16)
        att = _dot_nt(qb, kb) * dm_ref[h]
        s_h = s_scr[h]
        o = _dot(att.astype(BF16), vb) + _dot(qb, s_h.astype(BF16)) * qd_ref[h]
        kdec = (k * kd_ref[h]).astype(BF16)
        s_scr[h] = s_h * cd_ref[h] + _dot_tn(kdec, vb)
        mu = jnp.mean(o, axis=-1, keepdims=True)
        oc = o - mu
        var = jnp.mean(oc * oc, axis=-1, keepdims=True)
        on = oc * lax.rsqrt(var + LN_EPS) * g_ref[:, lo:lo + HEAD_DIM]
        outs.append(rg * _sigmoid(rg) * on)
    o_ref[...] = jnp.concatenate(outs, axis=1)

    @pl.when(i == pl.num_programs(1) - 1)
    def _():
        st_ref[0] = s_scr[...]


def _ret_tables(c):
    log_g = jnp.log1p(-jnp.exp2(-5.0 - jnp.arange(RET_HEADS, dtype=F32)))
    i = jnp.arange(c, dtype=F32)
    diff = i[:, None] - i[None, :]
    dmask = jnp.exp(jnp.where(diff >= 0, log_g[:, None, None] * diff, -jnp.inf))
    q_dec = jnp.exp(log_g[:, None] * (i[None, :] + 1.0))
    k_dec = jnp.exp(log_g[:, None] * (c - 1.0 - i[None, :]))
    c_dec = jnp.exp(log_g * c)
    bc = lambda a: jnp.broadcast_to(a[:, :, None], a.shape + (HEAD_DIM,))
    cd = jnp.broadcast_to(c_dec[:, None, None], (RET_HEADS, HEAD_DIM, HEAD_DIM))
    return dmask, bc(q_dec), bc(k_dec), cd


def _retention(u_ret, s0, gain, row0, nb, t):
    c = min(t, RET_CHUNK)
    while t % c:
        c -= 1
    nt = t // c
    base = row0 // c
    dm, qd, kd, cd = _ret_tables(c)
    w = RET_HEADS * HEAD_DIM
    full = lambda shape: pl.BlockSpec(shape, lambda bb, i: (0,) * len(shape))
    in_specs = [
        pl.BlockSpec((c, 4 * w), lambda bb, i: (base + bb * nt + i, 0)),
        pl.BlockSpec((1, RET_HEADS, HEAD_DIM, HEAD_DIM), lambda bb, i: (bb, 0, 0, 0)),
        full((RET_HEADS, c, c)),
        full((RET_HEADS, c, HEAD_DIM)),
        full((RET_HEADS, c, HEAD_DIM)),
        full((RET_HEADS, HEAD_DIM, HEAD_DIM)),
        full((1, w)),
    ]
    args = [u_ret, s0, dm, qd, kd, cd, gain]
    return pl.pallas_call(
        _ret_kernel,
        grid=(nb, nt),
        in_specs=in_specs,
        out_specs=[
            pl.BlockSpec((c, w), lambda bb, i: (bb * nt + i, 0)),
            pl.BlockSpec((1, RET_HEADS, HEAD_DIM, HEAD_DIM), lambda bb, i: (bb, 0, 0, 0)),
        ],
        out_shape=[
            jax.ShapeDtypeStruct((nb * t, w), F32),
            jax.ShapeDtypeStruct((nb, RET_HEADS, HEAD_DIM, HEAD_DIM), F32),
        ],
        scratch_shapes=[pltpu.VMEM((RET_HEADS, HEAD_DIM, HEAD_DIM), F32)],
        compiler_params=_cp(("parallel", "arbitrary")),
        name="retention",
    )(*args)


def _slope(h):
    return 2.0 ** -(h + 1)


def _build_q2(uq, tq):
    lane = lax.broadcasted_iota(I32, (tq, LANES), 1)
    parts = []
    for h in range(NSA_HEADS):
        kvh = h // GQA
        p = uq[:, (h // 2) * LANES:(h // 2 + 1) * LANES]
        if (h % 2) != kvh:
            p = pltpu.roll(p, HEAD_DIM, 1)
        keep = (lane >= HEAD_DIM) if kvh == 1 else (lane < HEAD_DIM)
        parts.append(jnp.where(keep, p * (HEAD_DIM ** -0.5), 0.0))
    return jnp.concatenate(parts, axis=0).astype(BF16)


def _masked_softmax_heads(s3, mask_of_head, dist):
    ps = []
    for h in range(NSA_HEADS):
        mk = mask_of_head(h)
        sh = jnp.where(mk, s3[h] - _slope(h) * dist, NEG_INF)
        mx = jnp.max(sh, axis=-1, keepdims=True)
        e = jnp.where(mk, jnp.exp(sh - mx), 0.0)
        den = jnp.maximum(jnp.sum(e, axis=-1, keepdims=True), TINY)
        ps.append(e / den)
    return ps


def _online_step(s3, mask_of_head, dist, pv, m, l, acc, tq):
    ps, alphas, ms, ls = [], [], [], []
    for h in range(NSA_HEADS):
        mk = mask_of_head(h)
        rows = slice(h * tq, (h + 1) * tq)
        sh = jnp.where(mk, s3[h] - _slope(h) * dist, NEG_INF)
        m_old = m[rows]
        m_new = jnp.maximum(m_old, jnp.max(sh, axis=-1, keepdims=True))
        p = jnp.where(mk, jnp.exp(sh - m_new), 0.0)
        alpha = jnp.exp(m_old - m_new)
        ls.append(alpha * l[rows] + jnp.sum(p, axis=-1, keepdims=True))
        ms.append(m_new)
        alphas.append(alpha)
        ps.append(p)
    p_all = jnp.concatenate(ps, axis=0).astype(BF16)
    alpha_all = jnp.concatenate(alphas, axis=0)
    acc = acc * alpha_all + pv(p_all)
    return jnp.concatenate(ms, axis=0), jnp.concatenate(ls, axis=0), acc


def _alibi_query_lanes(tq):
    lane = lax.broadcasted_iota(I32, (tq, LANES), 1)
    parts = [jnp.where(lane == 0, _slope(h), jnp.where(lane == 1, _slope(h) * LANES, 0.0))
             for h in range(NSA_HEADS)]
    return jnp.concatenate(parts, axis=0).astype(BF16)


def _alibi_key_lanes(pos):
    lane = lax.broadcasted_iota(I32, pos.shape, 1)
    return jnp.where(lane == 0, pos & (LANES - 1), jnp.where(lane == 1, pos >> 7, 0)).astype(F32).astype(BF16)


def _softmax_heads_bias(s3, bias):
    ps = []
    for h in range(NSA_HEADS):
        sh = s3[h] + bias
        e = jnp.exp(sh - jnp.max(sh, axis=-1, keepdims=True))
        ps.append(e / jnp.sum(e, axis=-1, keepdims=True))
    return ps


def _online_step_bias(s3, bias_of_head, pv, m, l, acc, tq):
    ps, alphas, ms, ls = [], [], [], []
    for h in range(NSA_HEADS):
        rows = slice(h * tq, (h + 1) * tq)
        sh = s3[h] + bias_of_head(h)
        m_old = m[rows]
        m_new = jnp.maximum(m_old, jnp.max(sh, axis=-1, keepdims=True))
        p = jnp.exp(sh - m_new)
        alpha = jnp.exp(m_old - m_new)
        ls.append(alpha * l[rows] + jnp.sum(p, axis=-1, keepdims=True))
        ms.append(m_new)
        alphas.append(alpha)
        ps.append(p)
    p_all = jnp.concatenate(ps, axis=0).astype(BF16)
    acc = acc * jnp.concatenate(alphas, axis=0) + pv(p_all)
    return jnp.concatenate(ms, axis=0), jnp.concatenate(ls, axis=0), acc


def _combine_heads(gl, o_cmp, o_sel, o_win, tq):
    gs = _sigmoid(gl)
    lane = lax.broadcasted_iota(I32, (tq, LANES), 1)
    pairs = []
    for mpair in range(NSA_HEADS // 2):
        halves = []
        for h in (2 * mpair, 2 * mpair + 1):
            rows = slice(h * tq, (h + 1) * tq)
            o = (gs[:, 3 * h:3 * h + 1] * o_cmp[rows] + gs[:, 3 * h + 1:3 * h + 2] * o_sel[rows]
                 + gs[:, 3 * h + 2:3 * h + 3] * o_win[rows])
            if (h // GQA) != (h % 2):
                o = pltpu.roll(o, HEAD_DIM, 1)
            halves.append(o)
        pairs.append(jnp.where(lane < HEAD_DIM, halves[0], halves[1]))
    return jnp.concatenate(pairs, axis=1)


def _cmp_probs(s, t0, tq, n_lanes):
    s3 = s.reshape(NSA_HEADS, tq, n_lanes)
    tpos = t0 + lax.broadcasted_iota(I32, (tq, n_lanes), 0)
    cend = lax.broadcasted_iota(I32, (tq, n_lanes), 1) * BLK + (BLK - 1)
    valid = cend <= tpos
    dist = (tpos - cend).astype(F32)
    return _masked_softmax_heads(s3, lambda h: valid, dist)


def _block_scores(ps, kv, t0, tq, n_lanes):
    tpos = t0 + lax.broadcasted_iota(I32, (tq, n_lanes), 0)
    nb = lax.broadcasted_iota(I32, (tq, n_lanes), 1)
    cur = tpos >> 6
    imp = ps[kv * GQA] + ps[kv * GQA + 1] + ps[kv * GQA + 2] + ps[kv * GQA + 3]
    forced = (nb == 0) | (nb == cur) | (nb == cur - 1)
    allowed = nb <= cur
    return jnp.where(allowed, jnp.where(forced, FORCE_SCORE, imp), -1.0), allowed, nb


def _nsa_prompt_kernel(q_ref, g_ref, kv_ref, o_ref, kb_ref, kc_ref, e_ref, selm_ref, m_ref, l_ref, acc_ref,
                       *, t, tq):
    i = pl.program_id(1)
    n_blk = t // BLK
    rows_all = NSA_HEADS * tq

    @pl.when(i == 0)
    def _():
        kc_ref[...] = jnp.zeros(kc_ref.shape, F32)
        step = 256
        for c in range(t // step):
            sl = slice(c * step, (c + 1) * step)
            pos_lanes = _alibi_key_lanes(c * step + lax.broadcasted_iota(I32, (step, LANES), 0))
            kb_ref[sl, 0:LANES] = kv_ref[sl, 2 * LANES:3 * LANES].astype(BF16)
            kb_ref[sl, LANES:2 * LANES] = pos_lanes
            kb_ref[sl, 2 * LANES:3 * LANES] = kv_ref[sl, 4 * LANES:5 * LANES].astype(BF16)
            kb_ref[sl, 3 * LANES:4 * LANES] = pos_lanes
            kb_ref[sl, 4 * LANES:5 * LANES] = kv_ref[sl, 3 * LANES:4 * LANES].astype(BF16)
            kb_ref[sl, 5 * LANES:6 * LANES] = kv_ref[sl, 5 * LANES:6 * LANES].astype(BF16)
            kc_ref[c * (step // BLK):(c + 1) * (step // BLK), :] = (
                kv_ref[sl, 0:2 * LANES].reshape(step // BLK, BLK, 2 * LANES).sum(axis=1) * (1.0 / BLK))
        blk_of_key = lax.broadcasted_iota(I32, (LANES, t), 1) >> 6
        e_ref[...] = jnp.where(blk_of_key == lax.broadcasted_iota(I32, (LANES, t), 0), 1.0, 0.0).astype(BF16)

    t0 = i * tq
    q2 = _build_q2(q_ref[...], tq)
    kc = kc_ref[...]
    ps = _cmp_probs(_dot_nt(q2, kc[:, :LANES].astype(BF16)), t0, tq, LANES)
    o_cmp = _dot(jnp.concatenate(ps, axis=0).astype(BF16), kc[:, LANES:].astype(BF16))

    e_bf = e_ref[...]
    every_allowed_block_fits = (t0 + tq - 1) // BLK < N_SEL
    for kv in range(NSA_KV):
        score, allowed, nb = _block_scores(ps, kv, t0, tq, LANES)

        def expand(sel, kv=kv):
            selm_ref[kv] = jnp.where(_dot(sel.astype(BF16), e_bf) > 0.5, 0.0, NEG_INF)

        @pl.when(every_allowed_block_fits)
        def _():
            expand(jnp.where(allowed, 1.0, 0.0))

        @pl.when(jnp.logical_not(every_allowed_block_fits))
        def _():
            rank = jnp.zeros((tq, LANES), F32)
            for mblk in range(n_blk):
                c = score[:, mblk:mblk + 1]
                tie = jnp.where(nb > mblk, 1.0, 0.0)
                rank = rank + jnp.where(c > score, 1.0, jnp.where(c == score, tie, 0.0))
            expand(jnp.where(allowed, jnp.where(rank < N_SEL, 1.0, 0.0), 0.0))

    q2e = jnp.concatenate([q2, _alibi_query_lanes(tq)], axis=1)

    m_ref[...] = jnp.full((rows_all, LANES), NEG_INF, F32)
    l_ref[...] = jnp.zeros((rows_all, LANES), F32)
    acc_ref[...] = jnp.zeros((rows_all, LANES), F32)
    tpos = t0 + lax.broadcasted_iota(I32, (tq, tq), 0)

    def body(kt, carry):
        k0 = pl.multiple_of(kt * tq, tq)
        k_bf = kb_ref[pl.ds(k0, tq), 0:2 * LANES]
        v_bf = kb_ref[pl.ds(k0, tq), 4 * LANES:5 * LANES]
        s3 = _dot_nt(q2e, k_bf).reshape(NSA_HEADS, tq, tq)
        causal = (k0 + lax.broadcasted_iota(I32, (tq, tq), 1)) <= tpos
        biases = [jnp.where(causal, selm_ref[kv, :, pl.ds(k0, tq)], NEG_INF) for kv in range(NSA_KV)]
        m, l, acc = _online_step_bias(s3, lambda h: biases[h // GQA], lambda p: _dot(p, v_bf),
                                      m_ref[...], l_ref[...], acc_ref[...], tq)
        m_ref[...] = m
        l_ref[...] = l
        acc_ref[...] = acc
        return carry

    lax.fori_loop(0, i + 1, body, 0)
    o_sel = acc_ref[...] / jnp.maximum(l_ref[...], TINY)

    wl = WINDOW + tq
    ws = pl.multiple_of(jnp.maximum(t0 - WINDOW, 0), tq)
    kw = kb_ref[pl.ds(ws, wl), 2 * LANES:4 * LANES]
    vw = kb_ref[pl.ds(ws, wl), 5 * LANES:6 * LANES]
    s3 = _dot_nt(q2e, kw).reshape(NSA_HEADS, tq, wl)
    dw = (t0 + lax.broadcasted_iota(I32, (tq, wl), 0)) - (ws + lax.broadcasted_iota(I32, (tq, wl), 1))
    bias_w = jnp.where(dw >= 0, jnp.where(dw < WINDOW, 0.0, NEG_INF), NEG_INF)
    pw = _softmax_heads_bias(s3, bias_w)
    o_win = _dot(jnp.concatenate(pw, axis=0).astype(BF16), vw)

    o_ref[...] = _combine_heads(g_ref[...], o_cmp, o_sel, o_win, tq)


def _nsa_prompt(u_q, u_gate, u_kv, nb, t):
    tq = NSA_TQ
    nt = t // tq
    wq = NSA_HEADS * HEAD_DIM
    in_specs = [
        pl.BlockSpec((tq, wq), lambda bb, i: (bb * nt + i, 0)),
        pl.BlockSpec((tq, LANES), lambda bb, i: (bb * nt + i, 0)),
        pl.BlockSpec((t, 6 * LANES), lambda bb, i: (bb, 0)),
    ]
    args = [u_q, u_gate, u_kv]
    rows_all = NSA_HEADS * tq
    return pl.pallas_call(
        functools.partial(_nsa_prompt_kernel, t=t, tq=tq),
        grid=(nb, nt),
        in_specs=in_specs,
        out_specs=pl.BlockSpec((tq, wq), lambda bb, i: (bb * nt + i, 0)),
        out_shape=jax.ShapeDtypeStruct((nb * t, wq), F32),
        scratch_shapes=[
            pltpu.VMEM((t, 6 * LANES), BF16),
            pltpu.VMEM((LANES, 2 * LANES), F32),
            pltpu.VMEM((LANES, t), BF16),
            pltpu.VMEM((NSA_KV, tq, t), F32),
            pltpu.VMEM((rows_all, LANES), F32),
            pltpu.VMEM((rows_all, LANES), F32),
            pltpu.VMEM((rows_all, LANES), F32),
        ],
        compiler_params=_cp(("parallel", "arbitrary")),
        name="nsa_prompt",
    )(*args)


def _nsa_sample_cmp_kernel(pt_ref, *refs, pp, tq, past_len):
    page_refs = refs[:pp]
    q_ref, ocmp_ref, selm_ref, cmp_scr = refs[pp:]
    j = pl.program_id(1)
    n_cmp = past_len // BLK
    per_step = pp * (PAGE // BLK)
    assert LANES % per_step == 0
    steps_per_tile = LANES // per_step

    @pl.when(j == 0)
    def _():
        cmp_scr[...] = jnp.zeros(cmp_scr.shape, F32)

    lane = lax.broadcasted_iota(I32, (2 * LANES, LANES), 1)
    off = (j % steps_per_tile) * per_step
    cols = jnp.zeros((2 * LANES, LANES), F32)
    for k in range(pp):
        pg = page_refs[k][...]
        lo = jnp.sum(pg[:, :BLK], axis=-1, keepdims=True)
        hi = jnp.sum(pg[:, BLK:], axis=-1, keepdims=True)
        cols = jnp.where(lane == off + 2 * k, lo, cols)
        cols = jnp.where(lane == off + 2 * k + 1, hi, cols)
    tile = pl.ds(pl.multiple_of((j // steps_per_tile) * LANES, LANES), LANES)
    cmp_scr[:, tile] = cmp_scr[:, tile] + cols * (1.0 / BLK)

    @pl.when(j == pl.num_programs(1) - 1)
    def _():
        q2 = _build_q2(q_ref[...], tq)
        kct = cmp_scr[0:LANES, 0:n_cmp].astype(BF16)
        vct = cmp_scr[LANES:2 * LANES, 0:n_cmp].astype(BF16)
        ps = _cmp_probs(_dot(q2, kct), past_len, tq, n_cmp)
        ocmp_ref[0] = _dot_nt(jnp.concatenate(ps, axis=0).astype(BF16), vct)
        scores = [_block_scores(ps, kv, past_len, tq, n_cmp)[0] for kv in range(NSA_KV)]
        sc = jnp.concatenate(scores, axis=0)
        idx = lax.broadcasted_iota(I32, sc.shape, 1).astype(F32)
        sel = jnp.zeros(sc.shape, F32)
        for _ in range(N_SEL - 1):
            mx = jnp.max(sc, axis=-1, keepdims=True)
            am = jnp.min(jnp.where(sc == mx, idx, float(n_cmp)), axis=-1, keepdims=True)
            hit = idx == am
            sel = jnp.where(hit, 1.0, sel)
            sc = jnp.where(hit, -2.0, sc)
        selm_ref[0] = sel


def _nsa_sample_cmp(cache, layer, page_table, u_q, row0, nb, tq):
    n_pages = page_table.shape[1]
    past_len = n_pages * PAGE
    n_cmp = past_len // BLK
    pp = min(SAMPLE_PAGES_PER_STEP, n_pages)
    steps = n_pages // pp
    wq = NSA_HEADS * HEAD_DIM
    base = row0 // tq

    def page_spec(k):
        return pl.BlockSpec((None, None, 2 * LANES, PAGE),
                            lambda bb, j, pt: (layer, pt[bb, j * pp + k], 0, 0))

    grid_spec = pltpu.PrefetchScalarGridSpec(
        num_scalar_prefetch=1,
        grid=(nb, steps),
        in_specs=[page_spec(k) for k in range(pp)]
        + [pl.BlockSpec((tq, wq), lambda bb, j, pt: (base + bb, 0))],
        out_specs=[
            pl.BlockSpec((1, NSA_HEADS * tq, LANES), lambda bb, j, pt: (bb, 0, 0)),
            pl.BlockSpec((1, NSA_KV * tq, n_cmp), lambda bb, j, pt: (bb, 0, 0)),
        ],
        scratch_shapes=[pltpu.VMEM((2 * LANES, -(-n_cmp // LANES) * LANES), F32)],
    )
    return pl.pallas_call(
        functools.partial(_nsa_sample_cmp_kernel, pp=pp, tq=tq, past_len=past_len),
        grid_spec=grid_spec,
        out_shape=[
            jax.ShapeDtypeStruct((nb, NSA_HEADS * tq, LANES), F32),
            jax.ShapeDtypeStruct((nb, NSA_KV * tq, n_cmp), F32),
        ],
        compiler_params=_cp(("parallel", "arbitrary")),
        name="nsa_sample_cmp",
    )(page_table, *([cache] * pp), u_q)


def _nsa_sample_sel_kernel(pages_ref, lpage_ref, cnt_ref, cache_ref, selm_ref, q_ref, g_ref, kvn_ref, ocmp_ref,
                           win_ref, o_ref, buf, sem, *, layer, tq, past_len):
    b = pl.program_id(0)
    n = cnt_ref[b]
    n_cmp = past_len // BLK
    rows_all = NSA_HEADS * tq

    def page_copy(slot, idx):
        return pltpu.make_async_copy(
            cache_ref.at[layer, pages_ref[b, idx], pl.ds(2 * LANES, 2 * LANES), :], buf.at[slot], sem.at[slot])

    @pl.when(n > 0)
    def _():
        page_copy(0, 0).start()

    q2 = _build_q2(q_ref[...], tq)
    selm = selm_ref[0]
    blk_idx = lax.broadcasted_iota(I32, selm.shape, 1)
    key = lax.broadcasted_iota(I32, (tq, LANES), 1)
    tpos = past_len + lax.broadcasted_iota(I32, (tq, LANES), 0)
    key16 = lax.broadcasted_iota(I32, (NSA_KV * tq, LANES), 1)

    def body(it, carry):
        m, l, acc = carry
        slot = it & 1

        @pl.when(it + 1 < n)
        def _():
            page_copy(1 - slot, it + 1).start()

        page_copy(slot, it).wait()
        pg = buf[slot]
        lp = lpage_ref[b, it]
        pos = lp * PAGE + key
        dist = (tpos - pos).astype(F32)
        sel_lo = jnp.sum(jnp.where(blk_idx == 2 * lp, selm, 0.0), axis=-1, keepdims=True)
        sel_hi = jnp.sum(jnp.where(blk_idx == 2 * lp + 1, selm, 0.0), axis=-1, keepdims=True)
        mk16 = jnp.where(key16 < BLK, sel_lo, sel_hi) > 0.5
        masks = [mk16[kv * tq:(kv + 1) * tq] for kv in range(NSA_KV)]
        s3 = _dot(q2, pg[:LANES, :].astype(BF16)).reshape(NSA_HEADS, tq, LANES)
        vt_bf = pg[LANES:, :].astype(BF16)
        return _online_step(s3, lambda h: masks[h // GQA], dist, lambda p: _dot_nt(p, vt_bf), m, l, acc, tq)

    init = (jnp.full((rows_all, LANES), NEG_INF, F32), jnp.zeros((rows_all, LANES), F32),
            jnp.zeros((rows_all, LANES), F32))
    m, l, acc = lax.fori_loop(0, n, body, init)

    kvn = kvn_ref[...]
    zpad = jnp.zeros((LANES - tq, LANES), F32)
    k_new = jnp.concatenate([kvn[:, 2 * LANES:3 * LANES], zpad], axis=0).astype(BF16)
    v_new = jnp.concatenate([kvn[:, 3 * LANES:4 * LANES], zpad], axis=0).astype(BF16)
    pos = past_len + key
    mk_new = (key < tq) & (pos <= tpos)
    s3 = _dot_nt(q2, k_new).reshape(NSA_HEADS, tq, LANES)
    m, l, acc = _online_step(s3, lambda h: mk_new, (tpos - pos).astype(F32), lambda p: _dot(p, v_new),
                             m, l, acc, tq)
    o_sel = acc / jnp.maximum(l, TINY)

    win = win_ref[0]
    wl = WINDOW + LANES
    kw_new = jnp.concatenate([kvn[:, 4 * LANES:5 * LANES], zpad], axis=0).astype(BF16)
    vw_new = jnp.concatenate([kvn[:, 5 * LANES:6 * LANES], zpad], axis=0).astype(BF16)
    widx = lax.broadcasted_iota(I32, (tq, wl), 1)
    pw = past_len - WINDOW + widx
    dw = (past_len + lax.broadcasted_iota(I32, (tq, wl), 0)) - pw
    mw = (dw >= 0) & (dw < WINDOW) & (pw >= 0) & (widx < WINDOW + tq)
    s = jnp.concatenate([_dot(q2, win[:LANES, :].astype(BF16)), _dot_nt(q2, kw_new)], axis=1)
    pws = _masked_softmax_heads(s.reshape(NSA_HEADS, tq, wl), lambda h: mw, dw.astype(F32))
    p_win = jnp.concatenate(pws, axis=0).astype(BF16)
    o_win = _dot_nt(p_win[:, :WINDOW], win[LANES:, :].astype(BF16)) + _dot(p_win[:, WINDOW:], vw_new)

    o_ref[...] = _combine_heads(g_ref[...], ocmp_ref[0], o_sel, o_win, tq)


def _nsa_sample_sel(cache, layer, pages, lpages, cnt, selm, u_q, u_gate, u_kv, o_cmp, win_cache,
                    row0, nb, tq, past_len):
    wq = NSA_HEADS * HEAD_DIM
    base = row0 // tq
    n_cmp = past_len // BLK
    grid_spec = pltpu.PrefetchScalarGridSpec(
        num_scalar_prefetch=3,
        grid=(nb,),
        in_specs=[
            pl.BlockSpec(memory_space=pl.ANY),
            pl.BlockSpec((1, NSA_KV * tq, n_cmp), lambda bb, *_: (bb, 0, 0)),
            pl.BlockSpec((tq, wq), lambda bb, *_: (base + bb, 0)),
            pl.BlockSpec((tq, LANES), lambda bb, *_: (base + bb, 0)),
            pl.BlockSpec((tq, 6 * LANES), lambda bb, *_: (base + bb, 0)),
            pl.BlockSpec((1, NSA_HEADS * tq, LANES), lambda bb, *_: (bb, 0, 0)),
            pl.BlockSpec((None, 1, 2 * LANES, WINDOW), lambda bb, *_: (layer, bb, 0, 0)),
        ],
        out_specs=pl.BlockSpec((tq, wq), lambda bb, *_: (bb, 0)),
        scratch_shapes=[pltpu.VMEM((2, 2 * LANES, PAGE), F32), pltpu.SemaphoreType.DMA((2,))],
    )
    return pl.pallas_call(
        functools.partial(_nsa_sample_sel_kernel, layer=layer, tq=tq, past_len=past_len),
        grid_spec=grid_spec,
        out_shape=jax.ShapeDtypeStruct((nb * tq, wq), F32),
        compiler_params=_cp(("arbitrary",)),
        name="nsa_sample_sel",
    )(pages, lpages, cnt, cache, selm, u_q, u_gate, u_kv, o_cmp, win_cache)


def _outproj_kernel(x_ref, cp_ref, cs_ref, ap_ref, as_ref, rp_ref, rs_ref, wo_ref, g_ref, b_ref, wr_ref, br_ref,
                    h_ref, te_ref, tg_ref, *, alpha, n_prompt_tiles):
    is_sample = pl.program_id(0) >= n_prompt_tiles
    pick = lambda p_ref, s_ref: jnp.where(is_sample, s_ref[...], p_ref[...]).astype(BF16)
    mix = (_dot(pick(cp_ref, cs_ref), wo_ref[0:CONV_CH, :])
           + _dot(pick(ap_ref, as_ref), wo_ref[CONV_CH:CONV_CH + NSA_HEADS * HEAD_DIM, :])
           + _dot(pick(rp_ref, rs_ref), wo_ref[CONV_CH + NSA_HEADS * HEAD_DIM:, :]))
    h = _layer_norm(alpha * x_ref[...] + mix, g_ref[...], b_ref[...])
    h_ref[...] = h
    h_hi = h.astype(BF16)
    h_lo = (h - h_hi.astype(F32)).astype(BF16)
    r = _dot(h_hi, wr_ref[...]) + _dot(h_lo, wr_ref[...])
    logits = r[:, 0:N_EXPERTS] + r[:, N_EXPERTS:2 * N_EXPERTS] + br_ref[...]
    idx = lax.broadcasted_iota(I32, logits.shape, 1).astype(F32)
    col = lax.broadcasted_iota(I32, (logits.shape[0], TOP_K), 1)
    vals = jnp.zeros((logits.shape[0], TOP_K), F32)
    ids = jnp.zeros((logits.shape[0], TOP_K), F32)
    cur = logits
    for k in range(TOP_K):
        mx = jnp.max(cur, axis=-1, keepdims=True)
        am = jnp.min(jnp.where(cur == mx, idx, float(N_EXPERTS)), axis=-1, keepdims=True)
        vals = jnp.where(col == k, mx, vals)
        ids = jnp.where(col == k, am, ids)
        cur = jnp.where(idx == am, -jnp.inf, cur)
    e = jnp.exp(vals - vals[:, 0:1])
    tg_ref[...] = e / jnp.sum(e, axis=-1, keepdims=True)
    te_ref[...] = ids.astype(I32)


def _outproj(x_all, mixers, w_out_bf, g, b, w_router, b_router, alpha):
    n, d = x_all.shape
    tm = TOKEN_TILE
    npt = mixers[0][0].shape[0] // tm
    row = lambda c: pl.BlockSpec((tm, c), lambda i: (i, 0))
    row_p = lambda c: pl.BlockSpec((tm, c), lambda i: (jnp.minimum(i, npt - 1), 0))
    row_s = lambda c: pl.BlockSpec((tm, c), lambda i: (jnp.maximum(i - npt, 0), 0))
    full = lambda r, c: pl.BlockSpec((r, c), lambda i: (0, 0))
    mix_specs, mix_args = [], []
    for a_p, a_s in mixers:
        mix_specs += [row_p(a_p.shape[1]), row_s(a_s.shape[1])]
        mix_args += [a_p, a_s]
    return pl.pallas_call(
        functools.partial(_outproj_kernel, alpha=alpha, n_prompt_tiles=npt),
        grid=(n // tm,),
        in_specs=[row(d)] + mix_specs + [
                  full(d, d), full(1, d), full(1, d), full(d, LANES), full(1, N_EXPERTS)],
        out_specs=[row(d), row(TOP_K), row(TOP_K)],
        out_shape=[jax.ShapeDtypeStruct((n, d), F32),
                   jax.ShapeDtypeStruct((n, TOP_K), I32), jax.ShapeDtypeStruct((n, TOP_K), F32)],
        compiler_params=_cp(("parallel",)),
        name="outproj",
    )(x_all, *mix_args, w_out_bf, g, b, w_router, b_router)


def _moe_kernel(be_ref, nu_ref, x_ref, wup_ref, bg_ref, bu_ref, wd_ref, bd_ref, rg_ref, o_ref,
                wg_scr, wu_scr, wd_scr):
    i = pl.program_id(0)
    dff = wg_scr.shape[1]

    @pl.when((i == 0) | (be_ref[i] != be_ref[jnp.maximum(i - 1, 0)]))
    def _():
        w2 = 2 * LANES
        r = lax.broadcasted_iota(I32, (w2, w2), 0)
        c = lax.broadcasted_iota(I32, (w2, w2), 1)
        perm = jnp.where(r == jnp.where(c < LANES, 2 * c, 2 * (c - LANES) + 1), 1.0, 0.0).astype(BF16)
        for k in range(2 * dff // w2):
            split = _dot(wup_ref[:, k * w2:(k + 1) * w2].astype(BF16), perm).astype(BF16)
            wg_scr[:, k * LANES:(k + 1) * LANES] = split[:, :LANES]
            wu_scr[:, k * LANES:(k + 1) * LANES] = split[:, LANES:]
        wd_scr[...] = wd_ref[...].astype(BF16)

    @pl.when(i < nu_ref[0])
    def _():
        x = x_ref[...].astype(BF16)
        g = jnp.minimum(_dot(x, wg_scr[...]) + bg_ref[...], SWIGLU_LIMIT)
        u = jnp.clip(_dot(x, wu_scr[...]) + bu_ref[...], -SWIGLU_LIMIT, SWIGLU_LIMIT)
        act = (u + 1.0) * g * _sigmoid(SWIGLU_ALPHA * g)
        y = _dot(act.astype(BF16), wd_scr[...]) + bd_ref[...]
        o_ref[...] = y * rg_ref[...]

    @pl.when(i >= nu_ref[0])
    def _():
        o_ref[...] = jnp.zeros(o_ref.shape, F32)


def _moe_blocks(xs, block_e, n_used, w_up, bg, bu, w_down, bd, row_gate):
    r, d = xs.shape
    bm = MOE_TILE
    dff = w_down.shape[1]
    n_blocks = r // bm
    wspec = lambda a, c: pl.BlockSpec((None, a, c), lambda i, be, nu: (be[i], 0, 0))
    grid_spec = pltpu.PrefetchScalarGridSpec(
        num_scalar_prefetch=2,
        grid=(n_blocks,),
        in_specs=[
            pl.BlockSpec((bm, d), lambda i, be, nu: (i, 0)),
            wspec(d, 2 * dff), wspec(1, dff), wspec(1, dff), wspec(dff, d), wspec(1, d),
            pl.BlockSpec((bm, 1), lambda i, be, nu: (i, 0)),
        ],
        out_specs=pl.BlockSpec((bm, d), lambda i, be, nu: (i, 0)),
        scratch_shapes=[pltpu.VMEM((d, dff), BF16), pltpu.VMEM((d, dff), BF16), pltpu.VMEM((dff, d), BF16)],
    )
    return pl.pallas_call(
        _moe_kernel,
        grid_spec=grid_spec,
        out_shape=jax.ShapeDtypeStruct((r, d), F32),
        compiler_params=_cp(("arbitrary",)),
        name="moe_experts",
    )(block_e, n_used, xs, w_up, bg, bu, w_down, bd, row_gate)


def _moe(h, top_e, gate, w_up, bg, bu, w_down, bd):
    n, d = h.shape
    bm = MOE_TILE
    nk = n * TOP_K
    experts = jnp.arange(N_EXPERTS, dtype=I32)
    flat_e = top_e.reshape(nk)
    order = jnp.argsort(flat_e).astype(I32)
    rank = jnp.argsort(order).astype(I32)
    counts = jnp.sum((flat_e[:, None] == experts[None, :]).astype(I32), axis=0)
    starts = jnp.cumsum(counts) - counts
    padded = (counts + bm - 1) // bm * bm
    pend = jnp.cumsum(padded)
    pstart = pend - padded
    n_blocks = -(-(nk + N_EXPERTS * (bm - 1)) // bm)
    r = n_blocks * bm
    block_start = jnp.arange(n_blocks, dtype=I32) * bm
    block_e = jnp.minimum(jnp.sum((pend[None, :] <= block_start[:, None]).astype(I32), axis=1), N_EXPERTS - 1)
    n_used = (pend[-1] // bm).astype(I32).reshape(1)
    row_e = jnp.repeat(block_e, bm)
    off = jnp.arange(r, dtype=I32) - pstart[row_e]
    valid = off < counts[row_e]
    assign = order[jnp.clip(starts[row_e] + off, 0, nk - 1)]
    row_tok = jnp.where(valid, assign // TOP_K, 0)
    row_gate = jnp.where(valid, gate.reshape(nk)[assign], 0.0)
    slot = pstart[flat_e] + rank - starts[flat_e]
    xs = jnp.take(h, row_tok, axis=0)
    yb = _moe_blocks(xs, block_e.astype(I32), n_used, w_up, bg, bu, w_down, bd, row_gate.reshape(r, 1))
    return jnp.take(yb, slot, axis=0).reshape(n, TOP_K, d).sum(axis=1)


def _final_kernel(h_ref, f_ref, p_ref, wple_ref, wplg_ref, g_ref, b_ref, y_ref, *, alpha):
    h = h_ref[...]
    ple = _dot(p_ref[...].astype(BF16), wple_ref[...]) * _sigmoid(_dot(h.astype(BF16), wplg_ref[...]))
    y_ref[...] = _layer_norm(alpha * h + f_ref[...] + ple, g_ref[...], b_ref[...])


def _final(h, ffn, p_all, w_ple_bf, w_plg_bf, g, b, alpha):
    n, d = h.shape
    tm = TOKEN_TILE
    pd = p_all.shape[1]
    row = lambda c: pl.BlockSpec((tm, c), lambda i: (i, 0))
    full = lambda r, c: pl.BlockSpec((r, c), lambda i: (0, 0))
    return pl.pallas_call(
        functools.partial(_final_kernel, alpha=alpha),
        grid=(n // tm,),
        in_specs=[row(d), row(d), row(pd), full(pd, d), full(d, d), full(1, d), full(1, d)],
        out_specs=row(d),
        out_shape=jax.ShapeDtypeStruct((n, d), F32),
        compiler_params=_cp(("parallel",)),
        name="final",
    )(h, ffn, p_all, w_ple_bf, w_plg_bf, g, b)


def kernel(x_prompt, x_sample, p_prompt, p_sample, cache_nsa_kv, cache_win_kv, state_ret, state_conv, page_table,
           w_in, w_out, conv_w, conv_b, conv_ln_g, conv_ln_b, ret_norm_g, ln1_g, ln1_b, w_router, b_router,
           w_up, b_up, w_down, b_down, w_ple, w_plg, ln2_g, ln2_b):
    bp, seq, d = x_prompt.shape
    bs, tdec, _ = x_sample.shape
    depth = w_in.shape[0]
    n_pages = page_table.shape[1]
    past_len = n_pages * PAGE
    wbuf = cache_win_kv.shape[2]
    n_p, n_s = bp * seq, bs * tdec
    assert wbuf == WINDOW and tdec <= BLK and tdec % 8 == 0 and past_len // BLK >= N_SEL
    assert seq % NSA_TQ == 0 and seq >= WINDOW + NSA_TQ and n_p % TOKEN_TILE == 0 and n_s % TOKEN_TILE == 0
    alpha = (2 * depth) ** 0.25
    n_pool = cache_nsa_kv.shape[1]
    cache = jnp.transpose(cache_nsa_kv, (0, 1, 3, 4, 5, 2)).reshape(depth, n_pool, 4 * LANES, PAGE)
    win_t = jnp.transpose(cache_win_kv, (0, 1, 3, 4, 5, 2)).reshape(depth, bs, 2 * LANES, wbuf)
    page_table = page_table.astype(I32)

    c_conv = 2 * CONV_CH
    c_q = NSA_HEADS * HEAD_DIM
    c_kv = 6 * NSA_KV * HEAD_DIM
    c_gate = 3 * NSA_HEADS
    cuts = [0, c_conv, c_conv + c_q, c_conv + c_q + c_kv, c_conv + c_q + c_kv + c_gate, w_in.shape[2]]

    x_all = jnp.concatenate([x_prompt.reshape(n_p, d), x_sample.reshape(n_s, d)], axis=0)
    outs = {k: [] for k in ("kvp", "kvs", "wp", "ws", "rp", "rs", "cp", "cs")}
    zeros_ret = jnp.zeros((bp, RET_HEADS, HEAD_DIM, HEAD_DIM), F32)
    zeros_conv = jnp.zeros((bp, CONV_K - 1, CONV_CH), F32)

    for i in range(depth):
        wi = w_in[i].astype(BF16)
        ws = [wi[:, cuts[j]:cuts[j + 1]] for j in range(5)]
        ws[3] = jnp.pad(ws[3], ((0, 0), (0, LANES - c_gate)))
        u_conv, u_q, u_kv, u_gate, u_ret = _inproj(x_all, ws)

        cw = jnp.pad(conv_w[i], ((0, 32 - CONV_K), (0, 0)))
        cargs = (cw, conv_b[i][None], conv_ln_g[i][None], conv_ln_b[i][None])
        conv_p, conv_sp = _conv(u_conv, zeros_conv, *cargs, 0, bp, seq)
        conv_s, conv_ss = _conv(u_conv, state_conv[i], *cargs, n_p, bs, tdec)

        rgain = ret_norm_g[i][None]
        ret_p, ret_sp = _retention(u_ret, zeros_ret, rgain, 0, bp, seq)
        ret_s, ret_ss = _retention(u_ret, state_ret[i], rgain, n_p, bs, tdec)

        nsa_p = _nsa_prompt(u_q, u_gate, u_kv, bp, seq)
        o_cmp, selm = _nsa_sample_cmp(cache, i, page_table, u_q, n_p, bs, tdec)
        need = selm.reshape(bs, NSA_KV * tdec, n_pages, PAGE // BLK).max(axis=(1, 3)) > 0.5
        order = jnp.argsort(jnp.logical_not(need), axis=1, stable=True).astype(I32)
        cnt = need.sum(axis=1).astype(I32)
        pages = jnp.take_along_axis(page_table, order, axis=1)
        nsa_s = _nsa_sample_sel(cache, i, pages, order, cnt, selm, u_q, u_gate, u_kv, o_cmp, win_t,
                                n_p, bs, tdec, past_len)

        wr_hi = w_router[i].astype(BF16)
        wr_lo = (w_router[i] - wr_hi.astype(F32)).astype(BF16)
        wr_cat = jnp.pad(jnp.concatenate([wr_hi, wr_lo], axis=1), ((0, 0), (0, LANES - 2 * N_EXPERTS)))
        h, top_e, gate = _outproj(x_all, ((conv_p, conv_s), (nsa_p, nsa_s), (ret_p, ret_s)),
                                  w_out[i].astype(BF16),
                                  ln1_g[i][None], ln1_b[i][None], wr_cat, b_router[i][None], alpha)

        ffn = _moe(h, top_e, gate, w_up[i], b_up[i][:, None, 0::2], b_up[i][:, None, 1::2],
                   w_down[i], b_down[i][:, None, :])

        p_all = jnp.concatenate([p_prompt[i].reshape(n_p, -1), p_sample[i].reshape(n_s, -1)], axis=0)
        x_all = _final(h, ffn, p_all, w_ple[i].astype(BF16), w_plg[i].astype(BF16),
                       ln2_g[i][None], ln2_b[i][None], alpha)

        kv_shape = (4, NSA_KV, HEAD_DIM)
        outs["kvp"].append(u_kv[:n_p, :4 * LANES].reshape(bp, seq, *kv_shape))
        outs["kvs"].append(u_kv[n_p:, :4 * LANES].reshape(bs, tdec, *kv_shape))
        win_new_p = u_kv[:n_p, 4 * LANES:].reshape(bp, seq, 2, NSA_KV, HEAD_DIM)
        outs["wp"].append(win_new_p[:, seq - min(WINDOW, seq):])
        win_new_s = u_kv[n_p:, 4 * LANES:].reshape(bs, tdec, 2, NSA_KV, HEAD_DIM)
        outs["ws"].append(jnp.concatenate([cache_win_kv[i][:, tdec:], win_new_s], axis=1))
        outs["rp"].append(ret_sp)
        outs["rs"].append(ret_ss)
        outs["cp"].append(conv_sp)
        outs["cs"].append(conv_ss)

    y_p = x_all[:n_p].reshape(bp, seq, d)
    y_s = x_all[n_p:].reshape(bs, tdec, d)
    st = lambda k: jnp.stack(outs[k])
    return (y_p, y_s, st("kvp"), st("kvs"), st("wp"), st("ws"), st("rp"), st("rs"), st("cp"), st("cs"))
```

```python
import functools

import jax
import jax.numpy as jnp
from jax import lax
from jax.experimental import pallas as pl
from jax.experimental.pallas import tpu as pltpu

F32 = jnp.float32
BF16 = jnp.bfloat16
I32 = jnp.int32

HEAD_DIM = 64
CONV_CH = 256
CONV_K = 31
NSA_HEADS = 8
NSA_KV = 2
GQA = 4
BLK = 64
N_SEL = 16
WINDOW = 512
FORCE_SCORE = 1e4
RET_HEADS = 4
RET_CHUNK = 128
N_EXPERTS = 32
TOP_K = 4
SWIGLU_LIMIT = 7.0
SWIGLU_ALPHA = 1.702
PAGE = 128
LN_EPS = 1e-5
NEG_INF = -1e30
TINY = 1e-30
LANES = 128

TOKEN_TILE = 256
MOE_TILE = 256
NSA_TQ = 128
SAMPLE_PAGES_PER_STEP = 16
VMEM_LIMIT = 56 * 1024 * 1024


def _cp(sem):
    return pltpu.CompilerParams(dimension_semantics=sem, vmem_limit_bytes=VMEM_LIMIT)


def _dot(a, b):
    return jnp.dot(a, b, preferred_element_type=F32)


def _dot_nt(a, b):
    return lax.dot_general(a, b, (((1,), (1,)), ((), ())), preferred_element_type=F32)


def _dot_tn(a, b):
    return lax.dot_general(a, b, (((0,), (0,)), ((), ())), preferred_element_type=F32)


def _layer_norm(x, g, b):
    mu = jnp.mean(x, axis=-1, keepdims=True)
    xc = x - mu
    var = jnp.mean(xc * xc, axis=-1, keepdims=True)
    return xc * lax.rsqrt(var + LN_EPS) * g + b


def _sigmoid(x):
    return 1.0 / (1.0 + jnp.exp(-x))


def _inproj_kernel(x_ref, wc_ref, wq_ref, wkv_ref, wg_ref, wr_ref, oc_ref, oq_ref, okv_ref, og_ref, or_ref, okvt_ref,
                   *, n_prompt_tiles):
    x = x_ref[...].astype(BF16)
    oc_ref[...] = _dot(x, wc_ref[...])
    oq_ref[...] = _dot(x, wq_ref[...])
    ukv = _dot(x, wkv_ref[...])
    okv_ref[...] = ukv
    og_ref[...] = _dot(x, wg_ref[...])
    or_ref[...] = _dot(x, wr_ref[...])

    @pl.when(pl.program_id(0) < n_prompt_tiles)
    def _():
        okvt_ref[...] = ukv.T


def _inproj(x_all, ws, bp, seq):
    n, d = x_all.shape
    tm = TOKEN_TILE
    widths = [w.shape[1] for w in ws]
    c_kv = widths[2]
    tiles_per_seq = seq // tm
    npt = bp * tiles_per_seq
    kvt_spec = pl.BlockSpec((None, c_kv, tm), lambda i: (jnp.minimum(i, npt - 1) // tiles_per_seq, 0,
                                                          jnp.minimum(i, npt - 1) % tiles_per_seq))
    return pl.pallas_call(
        functools.partial(_inproj_kernel, n_prompt_tiles=npt),
        grid=(n // tm,),
        in_specs=[pl.BlockSpec((tm, d), lambda i: (i, 0))]
        + [pl.BlockSpec((d, c), lambda i: (0, 0)) for c in widths],
        out_specs=[pl.BlockSpec((tm, c), lambda i: (i, 0)) for c in widths] + [kvt_spec],
        out_shape=[jax.ShapeDtypeStruct((n, c), F32) for c in widths]
        + [jax.ShapeDtypeStruct((bp, c_kv, seq), F32)],
        compiler_params=_cp(("arbitrary",)),
        name="inproj",
    )(x_all, *ws)


def _conv_kernel(u_ref, s0_ref, w_ref, cb_ref, g_ref, b_ref, o_ref, st_ref, ext, *, tt):
    i = pl.program_id(1)
    pad = 32 - (CONV_K - 1)

    @pl.when(i == 0)
    def _():
        ext[0:pad, :] = jnp.zeros((pad, CONV_CH), F32)
        ext[pad:32, :] = s0_ref[0]

    u = u_ref[...]
    h = u[:, :CONV_CH] * _sigmoid(u[:, CONV_CH:])
    ext[32:32 + tt, :] = h
    acc = jnp.zeros((tt, CONV_CH), F32)
    for j in range(CONV_K):
        acc = acc + ext[pad + j:pad + j + tt, :] * w_ref[j:j + 1, :]
    y = _layer_norm(acc + cb_ref[...], g_ref[...], b_ref[...])
    o_ref[...] = y * _sigmoid(y)
    tail = ext[tt + pad:tt + 32, :]

    @pl.when(i == pl.num_programs(1) - 1)
    def _():
        st_ref[0] = tail

    ext[pad:32, :] = tail


def _conv(u_conv, s0, w, cb, g, b, row0, nb, t):
    tt = min(t, 256)
    nt = t // tt
    base = row0 // tt
    in_specs = [
        pl.BlockSpec((tt, 2 * CONV_CH), lambda bb, i: (base + bb * nt + i, 0)),
        pl.BlockSpec((1, CONV_K - 1, CONV_CH), lambda bb, i: (bb, 0, 0)),
        pl.BlockSpec((32, CONV_CH), lambda bb, i: (0, 0)),
        pl.BlockSpec((1, CONV_CH), lambda bb, i: (0, 0)),
        pl.BlockSpec((1, CONV_CH), lambda bb, i: (0, 0)),
        pl.BlockSpec((1, CONV_CH), lambda bb, i: (0, 0)),
    ]
    args = [u_conv, s0, w, cb, g, b]
    return pl.pallas_call(
        functools.partial(_conv_kernel, tt=tt),
        grid=(nb, nt),
        in_specs=in_specs,
        out_specs=[
            pl.BlockSpec((tt, CONV_CH), lambda bb, i: (bb * nt + i, 0)),
            pl.BlockSpec((1, CONV_K - 1, CONV_CH), lambda bb, i: (bb, 0, 0)),
        ],
        out_shape=[
            jax.ShapeDtypeStruct((nb * t, CONV_CH), F32),
            jax.ShapeDtypeStruct((nb, CONV_K - 1, CONV_CH), F32),
        ],
        scratch_shapes=[pltpu.VMEM((32 + tt, CONV_CH), F32)],
        compiler_params=_cp(("parallel", "arbitrary")),
        name="conv",
    )(*args)


def _ret_kernel(u_ref, s0_ref, dm_ref, qd_ref, kd_ref, cd_ref, g_ref, o_ref, st_ref, s_scr):
    i = pl.program_id(1)

    @pl.when(i == 0)
    def _():
        s_scr[...] = s0_ref[0]

    u = u_ref[...]
    w = RET_HEADS * HEAD_DIM
    outs = []
    for h in range(RET_HEADS):
        lo = h * HEAD_DIM
        q = u[:, lo:lo + HEAD_DIM]
        k = u[:, w + lo:w + lo + HEAD_DIM] * (HEAD_DIM ** -0.5)
        v = u[:, 2 * w + lo:2 * w + lo + HEAD_DIM]
        rg = u[:, 3 * w + lo:3 * w + lo + HEAD_DIM]
        qb, kb, vb = q.astype(BF16), k.astype(BF16), v.astype(BF16)
        att = _dot_nt(qb, kb) * dm_ref[h]
        s_h = s_scr[h]
        o = _dot(att.astype(BF16), vb) + _dot(qb, s_h.astype(BF16)) * qd_ref[h]
        kdec = (k * kd_ref[h]).astype(BF16)
        s_scr[h] = s_h * cd_ref[h] + _dot_tn(kdec, vb)
        mu = jnp.mean(o, axis=-1, keepdims=True)
        oc = o - mu
        var = jnp.mean(oc * oc, axis=-1, keepdims=True)
        on = oc * lax.rsqrt(var + LN_EPS) * g_ref[:, lo:lo + HEAD_DIM]
        outs.append(rg * _sigmoid(rg) * on)
    o_ref[...] = jnp.concatenate(outs, axis=1)

    @pl.when(i == pl.num_programs(1) - 1)
    def _():
        st_ref[0] = s_scr[...]


def _ret_tables(c):
    log_g = jnp.log1p(-jnp.exp2(-5.0 - jnp.arange(RET_HEADS, dtype=F32)))
    i = jnp.arange(c, dtype=F32)
    diff = i[:, None] - i[None, :]
    dmask = jnp.exp(jnp.where(diff >= 0, log_g[:, None, None] * diff, -jnp.inf))
    q_dec = jnp.exp(log_g[:, None] * (i[None, :] + 1.0))
    k_dec = jnp.exp(log_g[:, None] * (c - 1.0 - i[None, :]))
    c_dec = jnp.exp(log_g * c)
    bc = lambda a: jnp.broadcast_to(a[:, :, None], a.shape + (HEAD_DIM,))
    cd = jnp.broadcast_to(c_dec[:, None, None], (RET_HEADS, HEAD_DIM, HEAD_DIM))
    return dmask, bc(q_dec), bc(k_dec), cd


def _retention(u_ret, s0, gain, row0, nb, t):
    c = min(t, RET_CHUNK)
    while t % c:
        c -= 1
    nt = t // c
    base = row0 // c
    dm, qd, kd, cd = _ret_tables(c)
    w = RET_HEADS * HEAD_DIM
    full = lambda shape: pl.BlockSpec(shape, lambda bb, i: (0,) * len(shape))
    in_specs = [
        pl.BlockSpec((c, 4 * w), lambda bb, i: (base + bb * nt + i, 0)),
        pl.BlockSpec((1, RET_HEADS, HEAD_DIM, HEAD_DIM), lambda bb, i: (bb, 0, 0, 0)),
        full((RET_HEADS, c, c)),
        full((RET_HEADS, c, HEAD_DIM)),
        full((RET_HEADS, c, HEAD_DIM)),
        full((RET_HEADS, HEAD_DIM, HEAD_DIM)),
        full((1, w)),
    ]
    args = [u_ret, s0, dm, qd, kd, cd, gain]
    return pl.pallas_call(
        _ret_kernel,
        grid=(nb, nt),
        in_specs=in_specs,
        out_specs=[
            pl.BlockSpec((c, w), lambda bb, i: (bb * nt + i, 0)),
            pl.BlockSpec((1, RET_HEADS, HEAD_DIM, HEAD_DIM), lambda bb, i: (bb, 0, 0, 0)),
        ],
        out_shape=[
            jax.ShapeDtypeStruct((nb * t, w), F32),
            jax.ShapeDtypeStruct((nb, RET_HEADS, HEAD_DIM, HEAD_DIM), F32),
        ],
        scratch_shapes=[pltpu.VMEM((RET_HEADS, HEAD_DIM, HEAD_DIM), F32)],
        compiler_params=_cp(("parallel", "arbitrary")),
        name="retention",
    )(*args)


def _slope(h):
    return 2.0 ** -(h + 1)


def _build_q2(uq, tq):
    lane = lax.broadcasted_iota(I32, (tq, LANES), 1)
    parts = []
    for h in range(NSA_HEADS):
        kvh = h // GQA
        p = uq[:, (h // 2) * LANES:(h // 2 + 1) * LANES]
        if (h % 2) != kvh:
            p = pltpu.roll(p, HEAD_DIM, 1)
        keep = (lane >= HEAD_DIM) if kvh == 1 else (lane < HEAD_DIM)
        parts.append(jnp.where(keep, p * (HEAD_DIM ** -0.5), 0.0))
    return jnp.concatenate(parts, axis=0).astype(BF16)


def _masked_softmax_heads(s3, mask_of_head, dist):
    ps = []
    for h in range(NSA_HEADS):
        mk = mask_of_head(h)
        sh = jnp.where(mk, s3[h] - _slope(h) * dist, NEG_INF)
        mx = jnp.max(sh, axis=-1, keepdims=True)
        e = jnp.where(mk, jnp.exp(sh - mx), 0.0)
        den = jnp.maximum(jnp.sum(e, axis=-1, keepdims=True), TINY)
        ps.append(e / den)
    return ps


def _online_step(s3, mask_of_head, dist, pv, m, l, acc, tq):
    ps, alphas, ms, ls = [], [], [], []
    for h in range(NSA_HEADS):
        mk = mask_of_head(h)
        rows = slice(h * tq, (h + 1) * tq)
        sh = jnp.where(mk, s3[h] - _slope(h) * dist, NEG_INF)
        m_old = m[rows]
        m_new = jnp.maximum(m_old, jnp.max(sh, axis=-1, keepdims=True))
        p = jnp.where(mk, jnp.exp(sh - m_new), 0.0)
        alpha = jnp.exp(m_old - m_new)
        ls.append(alpha * l[rows] + jnp.sum(p, axis=-1, keepdims=True))
        ms.append(m_new)
        alphas.append(alpha)
        ps.append(p)
    p_all = jnp.concatenate(ps, axis=0).astype(BF16)
    alpha_all = jnp.concatenate(alphas, axis=0)
    acc = acc * alpha_all + pv(p_all)
    return jnp.concatenate(ms, axis=0), jnp.concatenate(ls, axis=0), acc


def _alibi_query_lanes(tq):
    lane = lax.broadcasted_iota(I32, (tq, LANES), 1)
    parts = [jnp.where(lane == 0, _slope(h), jnp.where(lane == 1, _slope(h) * LANES, 0.0))
             for h in range(NSA_HEADS)]
    return jnp.concatenate(parts, axis=0).astype(BF16)


def _alibi_key_lanes(pos):
    lane = lax.broadcasted_iota(I32, pos.shape, 1)
    return jnp.where(lane == 0, pos & (LANES - 1), jnp.where(lane == 1, pos >> 7, 0)).astype(F32).astype(BF16)


def _softmax_heads_bias(s3, bias):
    ps = []
    for h in range(NSA_HEADS):
        sh = s3[h] + bias
        e = jnp.exp(sh - jnp.max(sh, axis=-1, keepdims=True))
        ps.append(e / jnp.sum(e, axis=-1, keepdims=True))
    return ps


def _online_step_bias(s3, bias_of_head, pv, m, l, acc, tq):
    ps, alphas, ms, ls = [], [], [], []
    for h in range(NSA_HEADS):
        rows = slice(h * tq, (h + 1) * tq)
        sh = s3[h] + bias_of_head(h)
        m_old = m[rows]
        m_new = jnp.maximum(m_old, jnp.max(sh, axis=-1, keepdims=True))
        p = jnp.exp(sh - jnp.concatenate([m_new, m_new], axis=1))
        alpha = jnp.exp(m_old - m_new)
        ls.append(alpha * l[rows] + p[:, :LANES] + p[:, LANES:])
        ms.append(m_new)
        alphas.append(alpha)
        ps.append(p)
    p_all = jnp.concatenate(ps, axis=0).astype(BF16)
    acc = acc * jnp.concatenate(alphas, axis=0) + pv(p_all)
    return jnp.concatenate(ms, axis=0), jnp.concatenate(ls, axis=0), acc


def _combine_heads(gl, o_cmp, o_sel, o_win, tq):
    gs = _sigmoid(gl)
    lane = lax.broadcasted_iota(I32, (tq, LANES), 1)
    pairs = []
    for mpair in range(NSA_HEADS // 2):
        halves = []
        for h in (2 * mpair, 2 * mpair + 1):
            rows = slice(h * tq, (h + 1) * tq)
            o = (gs[:, 3 * h:3 * h + 1] * o_cmp[rows] + gs[:, 3 * h + 1:3 * h + 2] * o_sel[rows]
                 + gs[:, 3 * h + 2:3 * h + 3] * o_win[rows])
            if (h // GQA) != (h % 2):
                o = pltpu.roll(o, HEAD_DIM, 1)
            halves.append(o)
        pairs.append(jnp.where(lane < HEAD_DIM, halves[0], halves[1]))
    return jnp.concatenate(pairs, axis=1)


def _cmp_probs(s, t0, tq, n_lanes):
    s3 = s.reshape(NSA_HEADS, tq, n_lanes)
    tpos = t0 + lax.broadcasted_iota(I32, (tq, n_lanes), 0)
    cend = lax.broadcasted_iota(I32, (tq, n_lanes), 1) * BLK + (BLK - 1)
    valid = cend <= tpos
    dist = (tpos - cend).astype(F32)
    return _masked_softmax_heads(s3, lambda h: valid, dist)


def _block_scores(ps, kv, t0, tq, n_lanes):
    tpos = t0 + lax.broadcasted_iota(I32, (tq, n_lanes), 0)
    nb = lax.broadcasted_iota(I32, (tq, n_lanes), 1)
    cur = tpos >> 6
    imp = ps[kv * GQA] + ps[kv * GQA + 1] + ps[kv * GQA + 2] + ps[kv * GQA + 3]
    forced = (nb == 0) | (nb == cur) | (nb == cur - 1)
    allowed = nb <= cur
    return jnp.where(allowed, jnp.where(forced, FORCE_SCORE, imp), -1.0), allowed, nb


def _nsa_prompt_kernel(q_ref, g_ref, kv_ref, o_ref, kb_ref, kc_ref, e_ref, sel_ref, m_ref, l_ref, acc_ref,
                       *, t, tq):
    i = pl.program_id(1)
    n_blk = t // BLK
    rows_all = NSA_HEADS * tq

    @pl.when(i == 0)
    def _():
        kc_ref[...] = jnp.zeros(kc_ref.shape, F32)
        step = 256
        for c in range(t // step):
            sl = slice(c * step, (c + 1) * step)
            pos_lanes = _alibi_key_lanes(c * step + lax.broadcasted_iota(I32, (step, LANES), 0))
            kb_ref[sl, 0:LANES] = kv_ref[sl, 2 * LANES:3 * LANES].astype(BF16)
            kb_ref[sl, LANES:2 * LANES] = pos_lanes
            kb_ref[sl, 2 * LANES:3 * LANES] = kv_ref[sl, 4 * LANES:5 * LANES].astype(BF16)
            kb_ref[sl, 3 * LANES:4 * LANES] = pos_lanes
            kb_ref[sl, 4 * LANES:5 * LANES] = kv_ref[sl, 3 * LANES:4 * LANES].astype(BF16)
            kb_ref[sl, 5 * LANES:6 * LANES] = kv_ref[sl, 5 * LANES:6 * LANES].astype(BF16)
            kc_ref[c * (step // BLK):(c + 1) * (step // BLK), :] = (
                kv_ref[sl, 0:2 * LANES].reshape(step // BLK, BLK, 2 * LANES).sum(axis=1) * (1.0 / BLK))
        blk_of_key = lax.broadcasted_iota(I32, (LANES, t), 1) >> 6
        e_ref[...] = jnp.where(blk_of_key == lax.broadcasted_iota(I32, (LANES, t), 0), 1.0, 0.0).astype(BF16)

    t0 = i * tq
    q2 = _build_q2(q_ref[...], tq)
    kc = kc_ref[...]
    ps = _cmp_probs(_dot_nt(q2, kc[:, :LANES].astype(BF16)), t0, tq, LANES)
    o_cmp = _dot(jnp.concatenate(ps, axis=0).astype(BF16), kc[:, LANES:].astype(BF16))

    every_allowed_block_fits = (t0 + tq - 1) // BLK < N_SEL
    for kv in range(NSA_KV):
        score, allowed, nb = _block_scores(ps, kv, t0, tq, LANES)

        @pl.when(every_allowed_block_fits)
        def _():
            sel_ref[kv] = jnp.where(allowed, 1.0, 0.0).astype(BF16)

        @pl.when(jnp.logical_not(every_allowed_block_fits))
        def _():
            rank = jnp.zeros((tq, LANES), F32)
            for mblk in range(n_blk):
                c = score[:, mblk:mblk + 1]
                tie = jnp.where(nb > mblk, 1.0, 0.0)
                rank = rank + jnp.where(c > score, 1.0, jnp.where(c == score, tie, 0.0))
            sel_ref[kv] = jnp.where(allowed, jnp.where(rank < N_SEL, 1.0, 0.0), 0.0).astype(BF16)

    q2e = jnp.concatenate([q2, _alibi_query_lanes(tq)], axis=1)

    m_ref[...] = jnp.full((rows_all, LANES), NEG_INF, F32)
    l_ref[...] = jnp.zeros((rows_all, LANES), F32)
    acc_ref[...] = jnp.zeros((rows_all, LANES), F32)
    tk = 2 * tq
    tpos = t0 + lax.broadcasted_iota(I32, (tq, tk), 0)

    def body(kt, carry):
        k0 = pl.multiple_of(kt * tk, tk)
        k_bf = kb_ref[pl.ds(k0, tk), 0:2 * LANES]
        v_bf = kb_ref[pl.ds(k0, tk), 4 * LANES:5 * LANES]
        s3 = _dot_nt(q2e, k_bf).reshape(NSA_HEADS, tq, tk)
        causal = (k0 + lax.broadcasted_iota(I32, (tq, tk), 1)) <= tpos
        e_tile = e_ref[:, pl.ds(k0, tk)]
        biases = [jnp.where(causal, jnp.where(_dot(sel_ref[kv], e_tile) > 0.5, 0.0, NEG_INF), NEG_INF)
                  for kv in range(NSA_KV)]
        m, l, acc = _online_step_bias(s3, lambda h: biases[h // GQA], lambda p: _dot(p, v_bf),
                                      m_ref[...], l_ref[...], acc_ref[...], tq)
        m_ref[...] = m
        l_ref[...] = l
        acc_ref[...] = acc
        return carry

    lax.fori_loop(0, (i + 2) // 2, body, 0)
    o_sel = acc_ref[...] / jnp.sum(l_ref[...], axis=-1, keepdims=True)

    wl = WINDOW + tq
    ws = pl.multiple_of(jnp.maximum(t0 - WINDOW, 0), tq)
    kw = kb_ref[pl.ds(ws, wl), 2 * LANES:4 * LANES]
    vw = kb_ref[pl.ds(ws, wl), 5 * LANES:6 * LANES]
    s3 = _dot_nt(q2e, kw).reshape(NSA_HEADS, tq, wl)
    dw = (t0 + lax.broadcasted_iota(I32, (tq, wl), 0)) - (ws + lax.broadcasted_iota(I32, (tq, wl), 1))
    bias_w = jnp.where(dw >= 0, jnp.where(dw < WINDOW, 0.0, NEG_INF), NEG_INF)
    pw = _softmax_heads_bias(s3, bias_w)
    o_win = _dot(jnp.concatenate(pw, axis=0).astype(BF16), vw)

    o_ref[...] = _combine_heads(g_ref[...], o_cmp, o_sel, o_win, tq)


def _nsa_prompt(u_q, u_gate, u_kv, nb, t):
    tq = NSA_TQ
    nt = t // tq
    wq = NSA_HEADS * HEAD_DIM
    in_specs = [
        pl.BlockSpec((tq, wq), lambda bb, i: (bb * nt + i, 0)),
        pl.BlockSpec((tq, LANES), lambda bb, i: (bb * nt + i, 0)),
        pl.BlockSpec((t, 6 * LANES), lambda bb, i: (bb, 0)),
    ]
    args = [u_q, u_gate, u_kv]
    rows_all = NSA_HEADS * tq
    return pl.pallas_call(
        functools.partial(_nsa_prompt_kernel, t=t, tq=tq),
        grid=(nb, nt),
        in_specs=in_specs,
        out_specs=pl.BlockSpec((tq, wq), lambda bb, i: (bb * nt + i, 0)),
        out_shape=jax.ShapeDtypeStruct((nb * t, wq), F32),
        scratch_shapes=[
            pltpu.VMEM((t, 6 * LANES), BF16),
            pltpu.VMEM((LANES, 2 * LANES), F32),
            pltpu.VMEM((LANES, t), BF16),
            pltpu.VMEM((NSA_KV, tq, LANES), BF16),
            pltpu.VMEM((rows_all, LANES), F32),
            pltpu.VMEM((rows_all, LANES), F32),
            pltpu.VMEM((rows_all, LANES), F32),
        ],
        compiler_params=_cp(("parallel", "arbitrary")),
        name="nsa_prompt",
    )(*args)


def _nsa_sample_cmp_kernel(pt_ref, *refs, pp, tq, past_len):
    page_refs = refs[:pp]
    q_ref, ocmp_ref, selm_ref, cmp_scr = refs[pp:]
    j = pl.program_id(1)
    n_cmp = past_len // BLK
    per_step = pp * (PAGE // BLK)
    assert LANES % per_step == 0
    steps_per_tile = LANES // per_step

    @pl.when(j == 0)
    def _():
        cmp_scr[...] = jnp.zeros(cmp_scr.shape, F32)

    lane = lax.broadcasted_iota(I32, (2 * LANES, LANES), 1)
    off = (j % steps_per_tile) * per_step
    cols = jnp.zeros((2 * LANES, LANES), F32)
    first_half = lax.broadcasted_iota(I32, (2 * LANES, PAGE), 1) < BLK
    for k in range(pp):
        pg = page_refs[k][...]
        lo = jnp.sum(jnp.where(first_half, pg, 0.0), axis=-1, keepdims=True)
        hi = jnp.sum(jnp.where(first_half, 0.0, pg), axis=-1, keepdims=True)
        cols = jnp.where(lane == off + 2 * k, lo, cols)
        cols = jnp.where(lane == off + 2 * k + 1, hi, cols)
    tile = pl.ds(pl.multiple_of((j // steps_per_tile) * LANES, LANES), LANES)
    cmp_scr[:, tile] = cmp_scr[:, tile] + cols * (1.0 / BLK)

    @pl.when(j == pl.num_programs(1) - 1)
    def _():
        q2 = _build_q2(q_ref[...], tq)
        kct = cmp_scr[0:LANES, 0:n_cmp].astype(BF16)
        vct = cmp_scr[LANES:2 * LANES, 0:n_cmp].astype(BF16)
        ps = _cmp_probs(_dot(q2, kct), past_len, tq, n_cmp)
        ocmp_ref[0] = _dot_nt(jnp.concatenate(ps, axis=0).astype(BF16), vct)
        scores = [_block_scores(ps, kv, past_len, tq, n_cmp)[0] for kv in range(NSA_KV)]
        sc = jnp.concatenate(scores, axis=0)
        idx = lax.broadcasted_iota(I32, sc.shape, 1).astype(F32)
        sel = jnp.zeros(sc.shape, F32)
        for _ in range(N_SEL - 1):
            mx = jnp.max(sc, axis=-1, keepdims=True)
            am = jnp.min(jnp.where(sc == mx, idx, float(n_cmp)), axis=-1, keepdims=True)
            hit = idx == am
            sel = jnp.where(hit, 1.0, sel)
            sc = jnp.where(hit, -2.0, sc)
        selm_ref[0] = sel


def _nsa_sample_cmp(cache, layer, page_table, u_q, row0, nb, tq):
    n_pages = page_table.shape[1]
    past_len = n_pages * PAGE
    n_cmp = past_len // BLK
    pp = min(SAMPLE_PAGES_PER_STEP, n_pages)
    steps = n_pages // pp
    wq = NSA_HEADS * HEAD_DIM
    base = row0 // tq

    def page_spec(k):
        return pl.BlockSpec((None, None, 2 * LANES, PAGE),
                            lambda bb, j, pt: (layer, pt[bb, j * pp + k], 0, 0))

    grid_spec = pltpu.PrefetchScalarGridSpec(
        num_scalar_prefetch=1,
        grid=(nb, steps),
        in_specs=[page_spec(k) for k in range(pp)]
        + [pl.BlockSpec((tq, wq), lambda bb, j, pt: (base + bb, 0))],
        out_specs=[
            pl.BlockSpec((1, NSA_HEADS * tq, LANES), lambda bb, j, pt: (bb, 0, 0)),
            pl.BlockSpec((1, NSA_KV * tq, n_cmp), lambda bb, j, pt: (bb, 0, 0)),
        ],
        scratch_shapes=[pltpu.VMEM((2 * LANES, -(-n_cmp // LANES) * LANES), F32)],
    )
    return pl.pallas_call(
        functools.partial(_nsa_sample_cmp_kernel, pp=pp, tq=tq, past_len=past_len),
        grid_spec=grid_spec,
        out_shape=[
            jax.ShapeDtypeStruct((nb, NSA_HEADS * tq, LANES), F32),
            jax.ShapeDtypeStruct((nb, NSA_KV * tq, n_cmp), F32),
        ],
        compiler_params=_cp(("parallel", "arbitrary")),
        name="nsa_sample_cmp",
    )(page_table, *([cache] * pp), u_q)


def _nsa_sample_sel_kernel(pages_ref, lpage_ref, cnt_ref, cache_ref, selm_ref, q_ref, g_ref, kvn_ref, ocmp_ref,
                           win_ref, o_ref, buf, sem, *, layer, tq, past_len):
    b = pl.program_id(0)
    n = cnt_ref[b]
    n_cmp = past_len // BLK
    rows_all = NSA_HEADS * tq

    def page_copy(slot, idx):
        return pltpu.make_async_copy(
            cache_ref.at[layer, pages_ref[b, idx], pl.ds(2 * LANES, 2 * LANES), :], buf.at[slot], sem.at[slot])

    @pl.when(n > 0)
    def _():
        page_copy(0, 0).start()

    q2 = _build_q2(q_ref[...], tq)
    selm = selm_ref[0]
    blk_idx = lax.broadcasted_iota(I32, selm.shape, 1)
    key = lax.broadcasted_iota(I32, (tq, LANES), 1)
    tpos = past_len + lax.broadcasted_iota(I32, (tq, LANES), 0)
    key16 = lax.broadcasted_iota(I32, (NSA_KV * tq, LANES), 1)

    def body(it, carry):
        m, l, acc = carry
        slot = it & 1

        @pl.when(it + 1 < n)
        def _():
            page_copy(1 - slot, it + 1).start()

        page_copy(slot, it).wait()
        pg = buf[slot]
        lp = lpage_ref[b, it]
        pos = lp * PAGE + key
        dist = (tpos - pos).astype(F32)
        sel_lo = jnp.sum(jnp.where(blk_idx == 2 * lp, selm, 0.0), axis=-1, keepdims=True)
        sel_hi = jnp.sum(jnp.where(blk_idx == 2 * lp + 1, selm, 0.0), axis=-1, keepdims=True)
        mk16 = jnp.where(key16 < BLK, sel_lo, sel_hi) > 0.5
        masks = [mk16[kv * tq:(kv + 1) * tq] for kv in range(NSA_KV)]
        s3 = _dot(q2, pg[:LANES, :].astype(BF16)).reshape(NSA_HEADS, tq, LANES)
        vt_bf = pg[LANES:, :].astype(BF16)
        return _online_step(s3, lambda h: masks[h // GQA], dist, lambda p: _dot_nt(p, vt_bf), m, l, acc, tq)

    init = (jnp.full((rows_all, LANES), NEG_INF, F32), jnp.zeros((rows_all, LANES), F32),
            jnp.zeros((rows_all, LANES), F32))
    m, l, acc = lax.fori_loop(0, n, body, init)

    kvn = kvn_ref[...]
    zpad = jnp.zeros((LANES - tq, LANES), F32)
    k_new = jnp.concatenate([kvn[:, 2 * LANES:3 * LANES], zpad], axis=0).astype(BF16)
    v_new = jnp.concatenate([kvn[:, 3 * LANES:4 * LANES], zpad], axis=0).astype(BF16)
    pos = past_len + key
    mk_new = (key < tq) & (pos <= tpos)
    s3 = _dot_nt(q2, k_new).reshape(NSA_HEADS, tq, LANES)
    m, l, acc = _online_step(s3, lambda h: mk_new, (tpos - pos).astype(F32), lambda p: _dot(p, v_new),
                             m, l, acc, tq)
    o_sel = acc / jnp.maximum(l, TINY)

    win = win_ref[0]
    wl = WINDOW + LANES
    kw_new = jnp.concatenate([kvn[:, 4 * LANES:5 * LANES], zpad], axis=0).astype(BF16)
    vw_new = jnp.concatenate([kvn[:, 5 * LANES:6 * LANES], zpad], axis=0).astype(BF16)
    widx = lax.broadcasted_iota(I32, (tq, wl), 1)
    pw = past_len - WINDOW + widx
    dw = (past_len + lax.broadcasted_iota(I32, (tq, wl), 0)) - pw
    mw = (dw >= 0) & (dw < WINDOW) & (pw >= 0) & (widx < WINDOW + tq)
    s = jnp.concatenate([_dot(q2, win[:LANES, :].astype(BF16)), _dot_nt(q2, kw_new)], axis=1)
    pws = _masked_softmax_heads(s.reshape(NSA_HEADS, tq, wl), lambda h: mw, dw.astype(F32))
    p_win = jnp.concatenate(pws, axis=0).astype(BF16)
    o_win = _dot_nt(p_win[:, :WINDOW], win[LANES:, :].astype(BF16)) + _dot(p_win[:, WINDOW:], vw_new)

    o_ref[...] = _combine_heads(g_ref[...], ocmp_ref[0], o_sel, o_win, tq)


def _nsa_sample_sel(cache, layer, pages, lpages, cnt, selm, u_q, u_gate, u_kv, o_cmp, win_cache,
                    row0, nb, tq, past_len):
    wq = NSA_HEADS * HEAD_DIM
    base = row0 // tq
    n_cmp = past_len // BLK
    grid_spec = pltpu.PrefetchScalarGridSpec(
        num_scalar_prefetch=3,
        grid=(nb,),
        in_specs=[
            pl.BlockSpec(memory_space=pl.ANY),
            pl.BlockSpec((1, NSA_KV * tq, n_cmp), lambda bb, *_: (bb, 0, 0)),
            pl.BlockSpec((tq, wq), lambda bb, *_: (base + bb, 0)),
            pl.BlockSpec((tq, LANES), lambda bb, *_: (base + bb, 0)),
            pl.BlockSpec((tq, 6 * LANES), lambda bb, *_: (base + bb, 0)),
            pl.BlockSpec((1, NSA_HEADS * tq, LANES), lambda bb, *_: (bb, 0, 0)),
            pl.BlockSpec((None, 1, 2 * LANES, WINDOW), lambda bb, *_: (layer, bb, 0, 0)),
        ],
        out_specs=pl.BlockSpec((tq, wq), lambda bb, *_: (bb, 0)),
        scratch_shapes=[pltpu.VMEM((2, 2 * LANES, PAGE), F32), pltpu.SemaphoreType.DMA((2,))],
    )
    return pl.pallas_call(
        functools.partial(_nsa_sample_sel_kernel, layer=layer, tq=tq, past_len=past_len),
        grid_spec=grid_spec,
        out_shape=jax.ShapeDtypeStruct((nb * tq, wq), F32),
        compiler_params=_cp(("arbitrary",)),
        name="nsa_sample_sel",
    )(pages, lpages, cnt, cache, selm, u_q, u_gate, u_kv, o_cmp, win_cache)


def _outproj_kernel(x_ref, cp_ref, cs_ref, ap_ref, as_ref, rp_ref, rs_ref, wo_ref, g_ref, b_ref, wr_ref, br_ref,
                    h_ref, te_ref, tg_ref, *, alpha, n_prompt_tiles):
    is_sample = pl.program_id(0) >= n_prompt_tiles
    pick = lambda p_ref, s_ref: jnp.where(is_sample, s_ref[...], p_ref[...]).astype(BF16)
    mix = (_dot(pick(cp_ref, cs_ref), wo_ref[0:CONV_CH, :])
           + _dot(pick(ap_ref, as_ref), wo_ref[CONV_CH:CONV_CH + NSA_HEADS * HEAD_DIM, :])
           + _dot(pick(rp_ref, rs_ref), wo_ref[CONV_CH + NSA_HEADS * HEAD_DIM:, :]))
    h = _layer_norm(alpha * x_ref[...] + mix, g_ref[...], b_ref[...])
    h_ref[...] = h
    h_hi = h.astype(BF16)
    h_lo = (h - h_hi.astype(F32)).astype(BF16)
    r = _dot(h_hi, wr_ref[...]) + _dot(h_lo, wr_ref[...])
    logits = r[:, 0:N_EXPERTS] + r[:, N_EXPERTS:2 * N_EXPERTS] + br_ref[...]
    idx = lax.broadcasted_iota(I32, logits.shape, 1).astype(F32)
    col = lax.broadcasted_iota(I32, (logits.shape[0], TOP_K), 1)
    vals = jnp.zeros((logits.shape[0], TOP_K), F32)
    ids = jnp.zeros((logits.shape[0], TOP_K), F32)
    cur = logits
    for k in range(TOP_K):
        mx = jnp.max(cur, axis=-1, keepdims=True)
        am = jnp.min(jnp.where(cur == mx, idx, float(N_EXPERTS)), axis=-1, keepdims=True)
        vals = jnp.where(col == k, mx, vals)
        ids = jnp.where(col == k, am, ids)
        cur = jnp.where(idx == am, -jnp.inf, cur)
    e = jnp.exp(vals - vals[:, 0:1])
    tg_ref[...] = e / jnp.sum(e, axis=-1, keepdims=True)
    te_ref[...] = ids.astype(I32)


def _outproj(x_all, mixers, w_out_bf, g, b, w_router, b_router, alpha):
    n, d = x_all.shape
    tm = TOKEN_TILE
    npt = mixers[0][0].shape[0] // tm
    row = lambda c: pl.BlockSpec((tm, c), lambda i: (i, 0))
    row_p = lambda c: pl.BlockSpec((tm, c), lambda i: (jnp.minimum(i, npt - 1), 0))
    row_s = lambda c: pl.BlockSpec((tm, c), lambda i: (jnp.maximum(i - npt, 0), 0))
    full = lambda r, c: pl.BlockSpec((r, c), lambda i: (0, 0))
    mix_specs, mix_args = [], []
    for a_p, a_s in mixers:
        mix_specs += [row_p(a_p.shape[1]), row_s(a_s.shape[1])]
        mix_args += [a_p, a_s]
    return pl.pallas_call(
        functools.partial(_outproj_kernel, alpha=alpha, n_prompt_tiles=npt),
        grid=(n // tm,),
        in_specs=[row(d)] + mix_specs + [
                  full(d, d), full(1, d), full(1, d), full(d, LANES), full(1, N_EXPERTS)],
        out_specs=[row(d), row(TOP_K), row(TOP_K)],
        out_shape=[jax.ShapeDtypeStruct((n, d), F32),
                   jax.ShapeDtypeStruct((n, TOP_K), I32), jax.ShapeDtypeStruct((n, TOP_K), F32)],
        compiler_params=_cp(("parallel",)),
        name="outproj",
    )(x_all, *mix_args, w_out_bf, g, b, w_router, b_router)


def _moe_kernel(blk_ref, exp_ref, lo_ref, hi_ref, first_ref, x_ref, wup_ref, bg_ref, bu_ref, wd_ref, bd_ref, o_ref,
                wg_scr, wu_scr, wd_scr):
    i = pl.program_id(0)
    dff = wg_scr.shape[1]

    @pl.when((i == 0) | (exp_ref[i] != exp_ref[jnp.maximum(i - 1, 0)]))
    def _():
        w2 = 2 * LANES
        r = lax.broadcasted_iota(I32, (w2, w2), 0)
        c = lax.broadcasted_iota(I32, (w2, w2), 1)
        perm = jnp.where(r == jnp.where(c < LANES, 2 * c, 2 * (c - LANES) + 1), 1.0, 0.0).astype(BF16)
        for k in range(2 * dff // w2):
            split = _dot(wup_ref[:, k * w2:(k + 1) * w2].astype(BF16), perm).astype(BF16)
            wg_scr[:, k * LANES:(k + 1) * LANES] = split[:, :LANES]
            wu_scr[:, k * LANES:(k + 1) * LANES] = split[:, LANES:]
        wd_scr[...] = wd_ref[...].astype(BF16)

    lo, hi = lo_ref[i], hi_ref[i]

    @pl.when(hi > lo)
    def _():
        x = x_ref[...].astype(BF16)
        g = jnp.minimum(_dot(x, wg_scr[...]) + bg_ref[...], SWIGLU_LIMIT)
        u = jnp.clip(_dot(x, wu_scr[...]) + bu_ref[...], -SWIGLU_LIMIT, SWIGLU_LIMIT)
        act = (u + 1.0) * g * _sigmoid(SWIGLU_ALPHA * g)
        y = _dot(act.astype(BF16), wd_scr[...]) + bd_ref[...]
        row = lax.broadcasted_iota(I32, y.shape, 0)
        mine = (row >= lo) & (row < hi)

        @pl.when(first_ref[i] == 1)
        def _():
            o_ref[...] = jnp.where(mine, y, 0.0)

        @pl.when(first_ref[i] == 0)
        def _():
            o_ref[...] = jnp.where(mine, y, o_ref[...])


def _moe_blocks(xs, items, w_up, bg, bu, w_down, bd):
    r, d = xs.shape
    bm = MOE_TILE
    dff = w_down.shape[1]
    n_items = items[0].shape[0]
    wspec = lambda a, c: pl.BlockSpec((None, a, c), lambda i, blk, exp, *_: (exp[i], 0, 0))
    grid_spec = pltpu.PrefetchScalarGridSpec(
        num_scalar_prefetch=5,
        grid=(n_items,),
        in_specs=[
            pl.BlockSpec((bm, d), lambda i, blk, *_: (blk[i], 0)),
            wspec(d, 2 * dff), wspec(1, dff), wspec(1, dff), wspec(dff, d), wspec(1, d),
        ],
        out_specs=pl.BlockSpec((bm, d), lambda i, blk, *_: (blk[i], 0)),
        scratch_shapes=[pltpu.VMEM((d, dff), BF16), pltpu.VMEM((d, dff), BF16), pltpu.VMEM((dff, d), BF16)],
    )
    return pl.pallas_call(
        _moe_kernel,
        grid_spec=grid_spec,
        out_shape=jax.ShapeDtypeStruct((r, d), F32),
        compiler_params=_cp(("arbitrary",)),
        name="moe_experts",
    )(*items, xs, w_up, bg, bu, w_down, bd)


def _moe(h, top_e, w_up, bg, bu, w_down, bd):
    n, d = h.shape
    bm = MOE_TILE
    nk = n * TOP_K
    assert nk % bm == 0
    n_blk = nk // bm
    experts = jnp.arange(N_EXPERTS, dtype=I32)
    flat_e = top_e.reshape(nk)
    order = jnp.argsort(flat_e).astype(I32)
    rank = jnp.argsort(order).astype(I32)
    starts = jnp.searchsorted(flat_e[order], experts, side='left', method='scan_unrolled').astype(I32)
    ends = jnp.concatenate([starts[1:], jnp.full((1,), nk, I32)])
    has = ends > starts
    first_blk = starts // bm
    per_e = jnp.where(has, (ends - 1) // bm - first_blk + 1, 0)
    it_end = jnp.cumsum(per_e)
    it_start = it_end - per_e
    n_items = n_blk + N_EXPERTS - 1
    t = jnp.arange(n_items, dtype=I32)
    live = t < it_end[-1]
    e_t = jnp.minimum(jnp.sum((it_end[None, :] <= t[:, None]).astype(I32), axis=1), N_EXPERTS - 1)
    blk_t = first_blk[e_t] + t - it_start[e_t]
    lo = jnp.clip(starts[e_t] - blk_t * bm, 0, bm)
    hi = jnp.clip(ends[e_t] - blk_t * bm, 0, bm)
    e_last = jnp.max(jnp.where(has, experts, 0))
    blk_t = jnp.where(live, blk_t, n_blk - 1)
    e_t = jnp.where(live, e_t, e_last)
    lo = jnp.where(live, lo, 0)
    hi = jnp.where(live, hi, 0)
    first = jnp.concatenate([jnp.ones((1,), bool), blk_t[1:] != blk_t[:-1]]) & live
    items = tuple(a.astype(I32) for a in (blk_t, e_t, lo, hi, first))
    xs = jnp.take(h, order // TOP_K, axis=0)
    yb = _moe_blocks(xs, items, w_up, bg, bu, w_down, bd)
    return jnp.take(yb, rank, axis=0).reshape(n, TOP_K * d)


def _final_kernel(h_ref, f_ref, tg_ref, p_ref, wple_ref, wplg_ref, g_ref, b_ref, y_ref, *, alpha):
    h = h_ref[...]
    d = h.shape[1]
    tg = tg_ref[...]
    ffn = tg[:, 0:1] * f_ref[:, 0:d]
    for k in range(1, TOP_K):
        ffn = ffn + tg[:, k:k + 1] * f_ref[:, k * d:(k + 1) * d]
    ple = _dot(p_ref[...].astype(BF16), wple_ref[...]) * _sigmoid(_dot(h.astype(BF16), wplg_ref[...]))
    y_ref[...] = _layer_norm(alpha * h + ffn + ple, g_ref[...], b_ref[...])


def _final(h, expert_out, gate, p_all, w_ple_bf, w_plg_bf, g, b, alpha):
    n, d = h.shape
    tm = TOKEN_TILE
    pd = p_all.shape[1]
    row = lambda c: pl.BlockSpec((tm, c), lambda i: (i, 0))
    full = lambda r, c: pl.BlockSpec((r, c), lambda i: (0, 0))
    return pl.pallas_call(
        functools.partial(_final_kernel, alpha=alpha),
        grid=(n // tm,),
        in_specs=[row(d), row(TOP_K * d), row(TOP_K), row(pd), full(pd, d), full(d, d), full(1, d), full(1, d)],
        out_specs=row(d),
        out_shape=jax.ShapeDtypeStruct((n, d), F32),
        compiler_params=_cp(("parallel",)),
        name="final",
    )(h, expert_out, gate, p_all, w_ple_bf, w_plg_bf, g, b)


def kernel(x_prompt, x_sample, p_prompt, p_sample, cache_nsa_kv, cache_win_kv, state_ret, state_conv, page_table,
           w_in, w_out, conv_w, conv_b, conv_ln_g, conv_ln_b, ret_norm_g, ln1_g, ln1_b, w_router, b_router,
           w_up, b_up, w_down, b_down, w_ple, w_plg, ln2_g, ln2_b):
    bp, seq, d = x_prompt.shape
    bs, tdec, _ = x_sample.shape
    depth = w_in.shape[0]
    n_pages = page_table.shape[1]
    past_len = n_pages * PAGE
    wbuf = cache_win_kv.shape[2]
    n_p, n_s = bp * seq, bs * tdec
    assert wbuf == WINDOW and tdec <= BLK and tdec % 8 == 0 and past_len // BLK >= N_SEL
    assert seq % NSA_TQ == 0 and seq >= WINDOW + NSA_TQ and n_p % TOKEN_TILE == 0 and n_s % TOKEN_TILE == 0
    alpha = (2 * depth) ** 0.25
    n_pool = cache_nsa_kv.shape[1]
    cache = jnp.transpose(cache_nsa_kv, (0, 1, 3, 4, 5, 2)).reshape(depth, n_pool, 4 * LANES, PAGE)
    win_t = jnp.transpose(cache_win_kv, (0, 1, 3, 4, 5, 2)).reshape(depth, bs, 2 * LANES, wbuf)
    page_table = page_table.astype(I32)

    c_conv = 2 * CONV_CH
    c_q = NSA_HEADS * HEAD_DIM
    c_kv = 6 * NSA_KV * HEAD_DIM
    c_gate = 3 * NSA_HEADS
    cuts = [0, c_conv, c_conv + c_q, c_conv + c_q + c_kv, c_conv + c_q + c_kv + c_gate, w_in.shape[2]]

    x_all = jnp.concatenate([x_prompt.reshape(n_p, d), x_sample.reshape(n_s, d)], axis=0)
    outs = {k: [] for k in ("kvp", "kvs", "wp", "ws", "rp", "rs", "cp", "cs")}
    zeros_ret = jnp.zeros((bp, RET_HEADS, HEAD_DIM, HEAD_DIM), F32)
    zeros_conv = jnp.zeros((bp, CONV_K - 1, CONV_CH), F32)

    for i in range(depth):
        wi = w_in[i].astype(BF16)
        ws = [wi[:, cuts[j]:cuts[j + 1]] for j in range(5)]
        ws[3] = jnp.pad(ws[3], ((0, 0), (0, LANES - c_gate)))
        u_conv, u_q, u_kv, u_gate, u_ret, kv_t = _inproj(x_all, ws, bp, seq)

        cw = jnp.pad(conv_w[i], ((0, 32 - CONV_K), (0, 0)))
        cargs = (cw, conv_b[i][None], conv_ln_g[i][None], conv_ln_b[i][None])
        conv_p, conv_sp = _conv(u_conv, zeros_conv, *cargs, 0, bp, seq)
        conv_s, conv_ss = _conv(u_conv, state_conv[i], *cargs, n_p, bs, tdec)

        rgain = ret_norm_g[i][None]
        ret_p, ret_sp = _retention(u_ret, zeros_ret, rgain, 0, bp, seq)
        ret_s, ret_ss = _retention(u_ret, state_ret[i], rgain, n_p, bs, tdec)

        nsa_p = _nsa_prompt(u_q, u_gate, u_kv, bp, seq)
        o_cmp, selm = _nsa_sample_cmp(cache, i, page_table, u_q, n_p, bs, tdec)
        need = selm.reshape(bs, NSA_KV * tdec, n_pages, PAGE // BLK).max(axis=(1, 3)) > 0.5
        npos = jnp.cumsum(need.astype(I32), axis=1) - 1
        cnt = npos[:, -1] + 1
        pidx = jnp.arange(n_pages, dtype=I32)
        hit = need[:, :, None] & (npos[:, :, None] == pidx[None, None, :])
        order = jnp.sum(jnp.where(hit, pidx[None, :, None], 0), axis=1).astype(I32)
        pages = jnp.take_along_axis(page_table, order, axis=1)
        nsa_s = _nsa_sample_sel(cache, i, pages, order, cnt, selm, u_q, u_gate, u_kv, o_cmp, win_t,
                                n_p, bs, tdec, past_len)

        wr_hi = w_router[i].astype(BF16)
        wr_lo = (w_router[i] - wr_hi.astype(F32)).astype(BF16)
        wr_cat = jnp.pad(jnp.concatenate([wr_hi, wr_lo], axis=1), ((0, 0), (0, LANES - 2 * N_EXPERTS)))
        h, top_e, gate = _outproj(x_all, ((conv_p, conv_s), (nsa_p, nsa_s), (ret_p, ret_s)),
                                  w_out[i].astype(BF16),
                                  ln1_g[i][None], ln1_b[i][None], wr_cat, b_router[i][None], alpha)

        expert_out = _moe(h, top_e, w_up[i], b_up[i][:, None, 0::2], b_up[i][:, None, 1::2],
                          w_down[i], b_down[i][:, None, :])

        p_all = jnp.concatenate([p_prompt[i].reshape(n_p, -1), p_sample[i].reshape(n_s, -1)], axis=0)
        x_all = _final(h, expert_out, gate, p_all, w_ple[i].astype(BF16), w_plg[i].astype(BF16),
                       ln2_g[i][None], ln2_b[i][None], alpha)

        kv_shape = (4, NSA_KV, HEAD_DIM)
        outs["kvp"].append(jnp.transpose(kv_t[:, :4 * LANES].reshape(bp, *kv_shape, seq), (0, 4, 1, 2, 3)))
        outs["kvs"].append(u_kv[n_p:, :4 * LANES].reshape(bs, tdec, *kv_shape))
        n_keep = min(WINDOW, seq)
        win_t_p = kv_t[:, 4 * LANES:, seq - n_keep:].reshape(bp, 2, NSA_KV, HEAD_DIM, n_keep)
        outs["wp"].append(jnp.transpose(win_t_p, (0, 4, 1, 2, 3)))
        win_new_s = u_kv[n_p:, 4 * LANES:].reshape(bs, tdec, 2, NSA_KV, HEAD_DIM)
        outs["ws"].append(jnp.concatenate([cache_win_kv[i][:, tdec:], win_new_s], axis=1))
        outs["rp"].append(ret_sp)
        outs["rs"].append(ret_ss)
        outs["cp"].append(conv_sp)
        outs["cs"].append(conv_ss)

    y_p = x_all[:n_p].reshape(bp, seq, d)
    y_s = x_all[n_p:].reshape(bs, tdec, d)
    st = lambda k: jnp.stack(outs[k])
    return (y_p, y_s, st("kvp"), st("kvs"), st("wp"), st("ws"), st("rp"), st("rs"), st("cp"), st("cs"))
```

```python
import functools

import jax
import jax.numpy as jnp
from jax import lax
from jax.experimental import pallas as pl
from jax.experimental.pallas import tpu as pltpu

F32 = jnp.float32
BF16 = jnp.bfloat16
I32 = jnp.int32

HEAD_DIM = 64
CONV_CH = 256
CONV_K = 31
NSA_HEADS = 8
NSA_KV = 2
GQA = 4
BLK = 64
N_SEL = 16
WINDOW = 512
FORCE_SCORE = 1e4
RET_HEADS = 4
RET_CHUNK = 128
N_EXPERTS = 32
TOP_K = 4
SWIGLU_LIMIT = 7.0
SWIGLU_ALPHA = 1.702
PAGE = 128
LN_EPS = 1e-5
NEG_INF = -1e30
TINY = 1e-30
LANES = 128

TOKEN_TILE = 256
MOE_TILE = 256
NSA_TQ = 128
SAMPLE_PAGES_PER_STEP = 16
VMEM_LIMIT = 56 * 1024 * 1024


def _cp(sem):
    return pltpu.CompilerParams(dimension_semantics=sem, vmem_limit_bytes=VMEM_LIMIT)


def _dot(a, b):
    return jnp.dot(a, b, preferred_element_type=F32)


def _dot_nt(a, b):
    return lax.dot_general(a, b, (((1,), (1,)), ((), ())), preferred_element_type=F32)


def _dot_tn(a, b):
    return lax.dot_general(a, b, (((0,), (0,)), ((), ())), preferred_element_type=F32)


def _layer_norm(x, g, b):
    mu = jnp.mean(x, axis=-1, keepdims=True)
    xc = x - mu
    var = jnp.mean(xc * xc, axis=-1, keepdims=True)
    return xc * lax.rsqrt(var + LN_EPS) * g + b


def _sigmoid(x):
    return 1.0 / (1.0 + jnp.exp(-x))


def _inproj_kernel(x_ref, wc_ref, wq_ref, wkv_ref, wg_ref, wr_ref, oc_ref, oq_ref, okv_ref, og_ref, or_ref, okvt_ref,
                   *, n_prompt_tiles):
    x = x_ref[...].astype(BF16)
    oc_ref[...] = _dot(x, wc_ref[...])
    oq_ref[...] = _dot(x, wq_ref[...])
    ukv = _dot(x, wkv_ref[...])
    okv_ref[...] = ukv
    og_ref[...] = _dot(x, wg_ref[...])
    or_ref[...] = _dot(x, wr_ref[...])

    @pl.when(pl.program_id(0) < n_prompt_tiles)
    def _():
        okvt_ref[...] = ukv.T


def _inproj(x_all, ws, bp, seq):
    n, d = x_all.shape
    tm = TOKEN_TILE
    widths = [w.shape[1] for w in ws]
    c_kv = widths[2]
    tiles_per_seq = seq // tm
    npt = bp * tiles_per_seq
    kvt_spec = pl.BlockSpec((None, c_kv, tm), lambda i: (jnp.minimum(i, npt - 1) // tiles_per_seq, 0,
                                                          jnp.minimum(i, npt - 1) % tiles_per_seq))
    return pl.pallas_call(
        functools.partial(_inproj_kernel, n_prompt_tiles=npt),
        grid=(n // tm,),
        in_specs=[pl.BlockSpec((tm, d), lambda i: (i, 0))]
        + [pl.BlockSpec((d, c), lambda i: (0, 0)) for c in widths],
        out_specs=[pl.BlockSpec((tm, c), lambda i: (i, 0)) for c in widths] + [kvt_spec],
        out_shape=[jax.ShapeDtypeStruct((n, c), F32) for c in widths]
        + [jax.ShapeDtypeStruct((bp, c_kv, seq), F32)],
        compiler_params=_cp(("arbitrary",)),
        name="inproj",
    )(x_all, *ws)


def _conv_kernel(u_ref, s0_ref, w_ref, cb_ref, g_ref, b_ref, o_ref, st_ref, ext, *, tt):
    i = pl.program_id(1)
    pad = 32 - (CONV_K - 1)

    @pl.when(i == 0)
    def _():
        ext[0:pad, :] = jnp.zeros((pad, CONV_CH), F32)
        ext[pad:32, :] = s0_ref[0]

    u = u_ref[...]
    h = u[:, :CONV_CH] * _sigmoid(u[:, CONV_CH:])
    ext[32:32 + tt, :] = h
    acc = jnp.zeros((tt, CONV_CH), F32)
    for j in range(CONV_K):
        acc = acc + ext[pad + j:pad + j + tt, :] * w_ref[j:j + 1, :]
    y = _layer_norm(acc + cb_ref[...], g_ref[...], b_ref[...])
    o_ref[...] = y * _sigmoid(y)
    tail = ext[tt + pad:tt + 32, :]

    @pl.when(i == pl.num_programs(1) - 1)
    def _():
        st_ref[0] = tail

    ext[pad:32, :] = tail


def _conv(u_conv, s0, w, cb, g, b, row0, nb, t):
    tt = min(t, 256)
    nt = t // tt
    base = row0 // tt
    in_specs = [
        pl.BlockSpec((tt, 2 * CONV_CH), lambda bb, i: (base + bb * nt + i, 0)),
        pl.BlockSpec((1, CONV_K - 1, CONV_CH), lambda bb, i: (bb, 0, 0)),
        pl.BlockSpec((32, CONV_CH), lambda bb, i: (0, 0)),
        pl.BlockSpec((1, CONV_CH), lambda bb, i: (0, 0)),
        pl.BlockSpec((1, CONV_CH), lambda bb, i: (0, 0)),
        pl.BlockSpec((1, CONV_CH), lambda bb, i: (0, 0)),
    ]
    args = [u_conv, s0, w, cb, g, b]
    return pl.pallas_call(
        functools.partial(_conv_kernel, tt=tt),
        grid=(nb, nt),
        in_specs=in_specs,
        out_specs=[
            pl.BlockSpec((tt, CONV_CH), lambda bb, i: (bb * nt + i, 0)),
            pl.BlockSpec((1, CONV_K - 1, CONV_CH), lambda bb, i: (bb, 0, 0)),
        ],
        out_shape=[
            jax.ShapeDtypeStruct((nb * t, CONV_CH), F32),
            jax.ShapeDtypeStruct((nb, CONV_K - 1, CONV_CH), F32),
        ],
        scratch_shapes=[pltpu.VMEM((32 + tt, CONV_CH), F32)],
        compiler_params=_cp(("parallel", "arbitrary")),
        name="conv",
    )(*args)


def _ret_kernel(u_ref, s0_ref, dm_ref, qd_ref, kd_ref, cd_ref, g_ref, o_ref, st_ref, s_scr):
    i = pl.program_id(1)

    @pl.when(i == 0)
    def _():
        s_scr[...] = s0_ref[0]

    u = u_ref[...]
    w = RET_HEADS * HEAD_DIM
    outs = []
    for h in range(RET_HEADS):
        lo = h * HEAD_DIM
        q = u[:, lo:lo + HEAD_DIM]
        k = u[:, w + lo:w + lo + HEAD_DIM] * (HEAD_DIM ** -0.5)
        v = u[:, 2 * w + lo:2 * w + lo + HEAD_DIM]
        rg = u[:, 3 * w + lo:3 * w + lo + HEAD_DIM]
        qb, kb, vb = q.astype(BF16), k.astype(BF16), v.astype(BF16)
        att = _dot_nt(qb, kb) * dm_ref[h]
        s_h = s_scr[h]
        o = _dot(att.astype(BF16), vb) + _dot(qb, s_h.astype(BF16)) * qd_ref[h]
        kdec = (k * kd_ref[h]).astype(BF16)
        s_scr[h] = s_h * cd_ref[h] + _dot_tn(kdec, vb)
        mu = jnp.mean(o, axis=-1, keepdims=True)
        oc = o - mu
        var = jnp.mean(oc * oc, axis=-1, keepdims=True)
        on = oc * lax.rsqrt(var + LN_EPS) * g_ref[:, lo:lo + HEAD_DIM]
        outs.append(rg * _sigmoid(rg) * on)
    o_ref[...] = jnp.concatenate(outs, axis=1)

    @pl.when(i == pl.num_programs(1) - 1)
    def _():
        st_ref[0] = s_scr[...]


def _ret_tables(c):
    log_g = jnp.log1p(-jnp.exp2(-5.0 - jnp.arange(RET_HEADS, dtype=F32)))
    i = jnp.arange(c, dtype=F32)
    diff = i[:, None] - i[None, :]
    dmask = jnp.exp(jnp.where(diff >= 0, log_g[:, None, None] * diff, -jnp.inf))
    q_dec = jnp.exp(log_g[:, None] * (i[None, :] + 1.0))
    k_dec = jnp.exp(log_g[:, None] * (c - 1.0 - i[None, :]))
    c_dec = jnp.exp(log_g * c)
    bc = lambda a: jnp.broadcast_to(a[:, :, None], a.shape + (HEAD_DIM,))
    cd = jnp.broadcast_to(c_dec[:, None, None], (RET_HEADS, HEAD_DIM, HEAD_DIM))
    return dmask, bc(q_dec), bc(k_dec), cd


def _retention(u_ret, s0, gain, row0, nb, t):
    c = min(t, RET_CHUNK)
    while t % c:
        c -= 1
    nt = t // c
    base = row0 // c
    dm, qd, kd, cd = _ret_tables(c)
    w = RET_HEADS * HEAD_DIM
    full = lambda shape: pl.BlockSpec(shape, lambda bb, i: (0,) * len(shape))
    in_specs = [
        pl.BlockSpec((c, 4 * w), lambda bb, i: (base + bb * nt + i, 0)),
        pl.BlockSpec((1, RET_HEADS, HEAD_DIM, HEAD_DIM), lambda bb, i: (bb, 0, 0, 0)),
        full((RET_HEADS, c, c)),
        full((RET_HEADS, c, HEAD_DIM)),
        full((RET_HEADS, c, HEAD_DIM)),
        full((RET_HEADS, HEAD_DIM, HEAD_DIM)),
        full((1, w)),
    ]
    args = [u_ret, s0, dm, qd, kd, cd, gain]
    return pl.pallas_call(
        _ret_kernel,
        grid=(nb, nt),
        in_specs=in_specs,
        out_specs=[
            pl.BlockSpec((c, w), lambda bb, i: (bb * nt + i, 0)),
            pl.BlockSpec((1, RET_HEADS, HEAD_DIM, HEAD_DIM), lambda bb, i: (bb, 0, 0, 0)),
        ],
        out_shape=[
            jax.ShapeDtypeStruct((nb * t, w), F32),
            jax.ShapeDtypeStruct((nb, RET_HEADS, HEAD_DIM, HEAD_DIM), F32),
        ],
        scratch_shapes=[pltpu.VMEM((RET_HEADS, HEAD_DIM, HEAD_DIM), F32)],
        compiler_params=_cp(("parallel", "arbitrary")),
        name="retention",
    )(*args)


def _slope(h):
    return 2.0 ** -(h + 1)


def _build_q2(uq, tq):
    lane = lax.broadcasted_iota(I32, (tq, LANES), 1)
    parts = []
    for h in range(NSA_HEADS):
        kvh = h // GQA
        p = uq[:, (h // 2) * LANES:(h // 2 + 1) * LANES]
        if (h % 2) != kvh:
            p = pltpu.roll(p, HEAD_DIM, 1)
        keep = (lane >= HEAD_DIM) if kvh == 1 else (lane < HEAD_DIM)
        parts.append(jnp.where(keep, p * (HEAD_DIM ** -0.5), 0.0))
    return jnp.concatenate(parts, axis=0).astype(BF16)


def _masked_softmax_heads(s3, mask_of_head, dist):
    ps = []
    for h in range(NSA_HEADS):
        mk = mask_of_head(h)
        sh = jnp.where(mk, s3[h] - _slope(h) * dist, NEG_INF)
        mx = jnp.max(sh, axis=-1, keepdims=True)
        e = jnp.where(mk, jnp.exp(sh - mx), 0.0)
        den = jnp.maximum(jnp.sum(e, axis=-1, keepdims=True), TINY)
        ps.append(e / den)
    return ps


def _online_step(s3, mask_of_head, dist, pv, m, l, acc, tq):
    ps, alphas, ms, ls = [], [], [], []
    for h in range(NSA_HEADS):
        mk = mask_of_head(h)
        rows = slice(h * tq, (h + 1) * tq)
        sh = jnp.where(mk, s3[h] - _slope(h) * dist, NEG_INF)
        m_old = m[rows]
        m_new = jnp.maximum(m_old, jnp.max(sh, axis=-1, keepdims=True))
        p = jnp.where(mk, jnp.exp(sh - m_new), 0.0)
        alpha = jnp.exp(m_old - m_new)
        ls.append(alpha * l[rows] + jnp.sum(p, axis=-1, keepdims=True))
        ms.append(m_new)
        alphas.append(alpha)
        ps.append(p)
    p_all = jnp.concatenate(ps, axis=0).astype(BF16)
    alpha_all = jnp.concatenate(alphas, axis=0)
    acc = acc * alpha_all + pv(p_all)
    return jnp.concatenate(ms, axis=0), jnp.concatenate(ls, axis=0), acc


def _alibi_query_lanes(tq):
    lane = lax.broadcasted_iota(I32, (tq, LANES), 1)
    parts = [jnp.where(lane == 0, _slope(h), jnp.where(lane == 1, _slope(h) * LANES, 0.0))
             for h in range(NSA_HEADS)]
    return jnp.concatenate(parts, axis=0).astype(BF16)


def _alibi_key_lanes(pos):
    lane = lax.broadcasted_iota(I32, pos.shape, 1)
    return jnp.where(lane == 0, pos & (LANES - 1), jnp.where(lane == 1, pos >> 7, 0)).astype(F32).astype(BF16)


def _combine_heads(gl, o_cmp, o_sel, o_win, tq):
    gs = _sigmoid(gl)
    lane = lax.broadcasted_iota(I32, (tq, LANES), 1)
    pairs = []
    for mpair in range(NSA_HEADS // 2):
        halves = []
        for h in (2 * mpair, 2 * mpair + 1):
            rows = slice(h * tq, (h + 1) * tq)
            o = (gs[:, 3 * h:3 * h + 1] * o_cmp[rows] + gs[:, 3 * h + 1:3 * h + 2] * o_sel[rows]
                 + gs[:, 3 * h + 2:3 * h + 3] * o_win[rows])
            if (h // GQA) != (h % 2):
                o = pltpu.roll(o, HEAD_DIM, 1)
            halves.append(o)
        pairs.append(jnp.where(lane < HEAD_DIM, halves[0], halves[1]))
    return jnp.concatenate(pairs, axis=1)


def _cmp_probs(s, t0, tq, n_lanes):
    s3 = s.reshape(NSA_HEADS, tq, n_lanes)
    tpos = t0 + lax.broadcasted_iota(I32, (tq, n_lanes), 0)
    cend = lax.broadcasted_iota(I32, (tq, n_lanes), 1) * BLK + (BLK - 1)
    valid = cend <= tpos
    dist = (tpos - cend).astype(F32)
    return _masked_softmax_heads(s3, lambda h: valid, dist)


def _block_scores(ps, kv, t0, tq, n_lanes):
    tpos = t0 + lax.broadcasted_iota(I32, (tq, n_lanes), 0)
    nb = lax.broadcasted_iota(I32, (tq, n_lanes), 1)
    cur = tpos >> 6
    imp = ps[kv * GQA] + ps[kv * GQA + 1] + ps[kv * GQA + 2] + ps[kv * GQA + 3]
    forced = (nb == 0) | (nb == cur) | (nb == cur - 1)
    allowed = nb <= cur
    return jnp.where(allowed, jnp.where(forced, FORCE_SCORE, imp), -1.0), allowed, nb


def _nsa_prompt_kernel(q_ref, g_ref, kv_ref, o_ref, kb_ref, kc_ref, e_ref, sel_ref, q2e_ref, m_ref, l_ref, acc_ref,
                       *, t, tq):
    i = pl.program_id(1)
    n_blk = t // BLK
    rows_all = NSA_HEADS * tq

    @pl.when(i == 0)
    def _():
        kc_ref[...] = jnp.zeros(kc_ref.shape, F32)
        step = 256
        for c in range(t // step):
            sl = slice(c * step, (c + 1) * step)
            pos_lanes = _alibi_key_lanes(c * step + lax.broadcasted_iota(I32, (step, LANES), 0))
            kb_ref[sl, 0:LANES] = kv_ref[sl, 2 * LANES:3 * LANES].astype(BF16)
            kb_ref[sl, LANES:2 * LANES] = pos_lanes
            kb_ref[sl, 2 * LANES:3 * LANES] = kv_ref[sl, 4 * LANES:5 * LANES].astype(BF16)
            kb_ref[sl, 3 * LANES:4 * LANES] = pos_lanes
            kb_ref[sl, 4 * LANES:5 * LANES] = kv_ref[sl, 3 * LANES:4 * LANES].astype(BF16)
            kb_ref[sl, 5 * LANES:6 * LANES] = kv_ref[sl, 5 * LANES:6 * LANES].astype(BF16)
            kc_ref[c * (step // BLK):(c + 1) * (step // BLK), :] = (
                kv_ref[sl, 0:2 * LANES].reshape(step // BLK, BLK, 2 * LANES).sum(axis=1) * (1.0 / BLK))
        blk_of_key = lax.broadcasted_iota(I32, (LANES, t), 1) >> 6
        e_ref[...] = jnp.where(blk_of_key == lax.broadcasted_iota(I32, (LANES, t), 0), 1.0, 0.0).astype(BF16)

    t0 = i * tq
    q2 = _build_q2(q_ref[...], tq)
    kc = kc_ref[...]
    ps = _cmp_probs(_dot_nt(q2, kc[:, :LANES].astype(BF16)), t0, tq, LANES)
    o_cmp = _dot(jnp.concatenate(ps, axis=0).astype(BF16), kc[:, LANES:].astype(BF16))

    every_allowed_block_fits = (t0 + tq - 1) // BLK < N_SEL
    for kv in range(NSA_KV):
        score, allowed, nb = _block_scores(ps, kv, t0, tq, LANES)

        @pl.when(every_allowed_block_fits)
        def _():
            sel_ref[kv] = jnp.where(allowed, 1.0, 0.0).astype(BF16)

        @pl.when(jnp.logical_not(every_allowed_block_fits))
        def _():
            rank = jnp.zeros((tq, LANES), F32)
            for mblk in range(n_blk):
                c = score[:, mblk:mblk + 1]
                tie = jnp.where(nb > mblk, 1.0, 0.0)
                rank = rank + jnp.where(c > score, 1.0, jnp.where(c == score, tie, 0.0))
            sel_ref[kv] = jnp.where(allowed, jnp.where(rank < N_SEL, 1.0, 0.0), 0.0).astype(BF16)

    q2e_ref[...] = jnp.concatenate([q2, _alibi_query_lanes(tq)], axis=1)

    m_ref[...] = jnp.full((rows_all, LANES), NEG_INF, F32)
    l_ref[...] = jnp.zeros((rows_all, LANES), F32)
    acc_ref[...] = jnp.zeros((rows_all, LANES), F32)
    tk = 2 * tq
    tpos = t0 + lax.broadcasted_iota(I32, (tq, tk), 0)

    def body(kt, carry):
        k0 = pl.multiple_of(kt * tk, tk)
        k_bf = kb_ref[pl.ds(k0, tk), 0:2 * LANES]
        v_bf = kb_ref[pl.ds(k0, tk), 4 * LANES:5 * LANES]
        causal = (k0 + lax.broadcasted_iota(I32, (tq, tk), 1)) <= tpos
        e_tile = e_ref[:, pl.ds(k0, tk)]
        biases = [jnp.where(causal, jnp.where(_dot(sel_ref[kv], e_tile) > 0.5, 0.0, NEG_INF), NEG_INF)
                  for kv in range(NSA_KV)]
        s3 = _dot_nt(q2e_ref[...], k_bf).reshape(NSA_HEADS, tq, tk)
        m_all, l_all = m_ref[...], l_ref[...]
        ps, alphas, ms, ls = [], [], [], []
        for h in range(NSA_HEADS):
            rows = slice(h * tq, (h + 1) * tq)
            sh = s3[h] + biases[h // GQA]
            m_old = m_all[rows]
            m_new = jnp.maximum(m_old, jnp.max(sh, axis=-1, keepdims=True))
            p = jnp.exp(sh - jnp.concatenate([m_new, m_new], axis=1))
            alpha = jnp.exp(m_old - m_new)
            ls.append(alpha * l_all[rows] + p[:, :LANES] + p[:, LANES:])
            ms.append(m_new)
            alphas.append(alpha)
            ps.append(p.astype(BF16))
        acc_ref[...] = acc_ref[...] * jnp.concatenate(alphas, axis=0) + _dot(jnp.concatenate(ps, axis=0), v_bf)
        m_ref[...] = jnp.concatenate(ms, axis=0)
        l_ref[...] = jnp.concatenate(ls, axis=0)
        return carry

    lax.fori_loop(0, (i + 2) // 2, body, 0)
    o_sel = acc_ref[...] / jnp.sum(l_ref[...], axis=-1, keepdims=True)

    wl = WINDOW + tq
    ws = pl.multiple_of(jnp.maximum(t0 - WINDOW, 0), tq)
    kw = kb_ref[pl.ds(ws, wl), 2 * LANES:4 * LANES]
    vw = kb_ref[pl.ds(ws, wl), 5 * LANES:6 * LANES]
    dw = (t0 + lax.broadcasted_iota(I32, (tq, wl), 0)) - (ws + lax.broadcasted_iota(I32, (tq, wl), 1))
    bias_w = jnp.where(dw >= 0, jnp.where(dw < WINDOW, 0.0, NEG_INF), NEG_INF)
    s3 = _dot_nt(q2e_ref[...], kw).reshape(NSA_HEADS, tq, wl)
    es, sums = [], []
    for h in range(NSA_HEADS):
        sh = s3[h] + bias_w
        e = jnp.exp(sh - jnp.max(sh, axis=-1, keepdims=True))
        es.append(e.astype(BF16))
        sums.append(jnp.broadcast_to(jnp.sum(e, axis=-1, keepdims=True), (tq, LANES)))
    o_win = _dot(jnp.concatenate(es, axis=0), vw) / jnp.concatenate(sums, axis=0)

    o_ref[...] = _combine_heads(g_ref[...], o_cmp, o_sel, o_win, tq)


def _nsa_prompt(u_q, u_gate, u_kv, nb, t):
    tq = NSA_TQ
    nt = t // tq
    wq = NSA_HEADS * HEAD_DIM
    in_specs = [
        pl.BlockSpec((tq, wq), lambda bb, i: (bb * nt + i, 0)),
        pl.BlockSpec((tq, LANES), lambda bb, i: (bb * nt + i, 0)),
        pl.BlockSpec((t, 6 * LANES), lambda bb, i: (bb, 0)),
    ]
    args = [u_q, u_gate, u_kv]
    rows_all = NSA_HEADS * tq
    return pl.pallas_call(
        functools.partial(_nsa_prompt_kernel, t=t, tq=tq),
        grid=(nb, nt),
        in_specs=in_specs,
        out_specs=pl.BlockSpec((tq, wq), lambda bb, i: (bb * nt + i, 0)),
        out_shape=jax.ShapeDtypeStruct((nb * t, wq), F32),
        scratch_shapes=[
            pltpu.VMEM((t, 6 * LANES), BF16),
            pltpu.VMEM((LANES, 2 * LANES), F32),
            pltpu.VMEM((LANES, t), BF16),
            pltpu.VMEM((NSA_KV, tq, LANES), BF16),
            pltpu.VMEM((rows_all, 2 * LANES), BF16),
            pltpu.VMEM((rows_all, LANES), F32),
            pltpu.VMEM((rows_all, LANES), F32),
            pltpu.VMEM((rows_all, LANES), F32),
        ],
        compiler_params=_cp(("parallel", "arbitrary")),
        name="nsa_prompt",
    )(*args)


def _nsa_sample_cmp_kernel(pt_ref, *refs, pp, tq, past_len):
    page_refs = refs[:pp]
    q_ref, ocmp_ref, selm_ref, cmp_scr = refs[pp:]
    j = pl.program_id(1)
    n_cmp = past_len // BLK
    per_step = pp * (PAGE // BLK)
    assert LANES % per_step == 0
    steps_per_tile = LANES // per_step

    @pl.when(j == 0)
    def _():
        cmp_scr[...] = jnp.zeros(cmp_scr.shape, F32)

    lane = lax.broadcasted_iota(I32, (2 * LANES, LANES), 1)
    off = (j % steps_per_tile) * per_step
    cols = jnp.zeros((2 * LANES, LANES), F32)
    first_half = lax.broadcasted_iota(I32, (2 * LANES, PAGE), 1) < BLK
    for k in range(pp):
        pg = page_refs[k][...]
        lo = jnp.sum(jnp.where(first_half, pg, 0.0), axis=-1, keepdims=True)
        hi = jnp.sum(jnp.where(first_half, 0.0, pg), axis=-1, keepdims=True)
        cols = jnp.where(lane == off + 2 * k, lo, cols)
        cols = jnp.where(lane == off + 2 * k + 1, hi, cols)
    tile = pl.ds(pl.multiple_of((j // steps_per_tile) * LANES, LANES), LANES)
    cmp_scr[:, tile] = cmp_scr[:, tile] + cols * (1.0 / BLK)

    @pl.when(j == pl.num_programs(1) - 1)
    def _():
        q2 = _build_q2(q_ref[...], tq)
        kct = cmp_scr[0:LANES, 0:n_cmp].astype(BF16)
        vct = cmp_scr[LANES:2 * LANES, 0:n_cmp].astype(BF16)
        ps = _cmp_probs(_dot(q2, kct), past_len, tq, n_cmp)
        ocmp_ref[0] = _dot_nt(jnp.concatenate(ps, axis=0).astype(BF16), vct)
        scores = [_block_scores(ps, kv, past_len, tq, n_cmp)[0] for kv in range(NSA_KV)]
        sc = jnp.concatenate(scores, axis=0)
        idx = lax.broadcasted_iota(I32, sc.shape, 1).astype(F32)
        sel = jnp.zeros(sc.shape, F32)
        for _ in range(N_SEL - 1):
            mx = jnp.max(sc, axis=-1, keepdims=True)
            am = jnp.min(jnp.where(sc == mx, idx, float(n_cmp)), axis=-1, keepdims=True)
            hit = idx == am
            sel = jnp.where(hit, 1.0, sel)
            sc = jnp.where(hit, -2.0, sc)
        selm_ref[0] = sel


def _nsa_sample_cmp(cache, layer, page_table, u_q, row0, nb, tq):
    n_pages = page_table.shape[1]
    past_len = n_pages * PAGE
    n_cmp = past_len // BLK
    pp = min(SAMPLE_PAGES_PER_STEP, n_pages)
    steps = n_pages // pp
    wq = NSA_HEADS * HEAD_DIM
    base = row0 // tq

    def page_spec(k):
        return pl.BlockSpec((None, None, 2 * LANES, PAGE),
                            lambda bb, j, pt: (layer, pt[bb, j * pp + k], 0, 0))

    grid_spec = pltpu.PrefetchScalarGridSpec(
        num_scalar_prefetch=1,
        grid=(nb, steps),
        in_specs=[page_spec(k) for k in range(pp)]
        + [pl.BlockSpec((tq, wq), lambda bb, j, pt: (base + bb, 0))],
        out_specs=[
            pl.BlockSpec((1, NSA_HEADS * tq, LANES), lambda bb, j, pt: (bb, 0, 0)),
            pl.BlockSpec((1, NSA_KV * tq, n_cmp), lambda bb, j, pt: (bb, 0, 0)),
        ],
        scratch_shapes=[pltpu.VMEM((2 * LANES, -(-n_cmp // LANES) * LANES), F32)],
    )
    return pl.pallas_call(
        functools.partial(_nsa_sample_cmp_kernel, pp=pp, tq=tq, past_len=past_len),
        grid_spec=grid_spec,
        out_shape=[
            jax.ShapeDtypeStruct((nb, NSA_HEADS * tq, LANES), F32),
            jax.ShapeDtypeStruct((nb, NSA_KV * tq, n_cmp), F32),
        ],
        compiler_params=_cp(("parallel", "arbitrary")),
        name="nsa_sample_cmp",
    )(page_table, *([cache] * pp), u_q)


def _nsa_sample_sel_kernel(pages_ref, lpage_ref, cnt_ref, cache_ref, selm_ref, q_ref, g_ref, kvn_ref, ocmp_ref,
                           win_ref, o_ref, buf, sem, *, layer, tq, past_len):
    b = pl.program_id(0)
    n = cnt_ref[b]
    n_cmp = past_len // BLK
    rows_all = NSA_HEADS * tq

    def page_copy(slot, idx):
        return pltpu.make_async_copy(
            cache_ref.at[layer, pages_ref[b, idx], pl.ds(2 * LANES, 2 * LANES), :], buf.at[slot], sem.at[slot])

    @pl.when(n > 0)
    def _():
        page_copy(0, 0).start()

    q2 = _build_q2(q_ref[...], tq)
    selm = selm_ref[0]
    blk_idx = lax.broadcasted_iota(I32, selm.shape, 1)
    key = lax.broadcasted_iota(I32, (tq, LANES), 1)
    tpos = past_len + lax.broadcasted_iota(I32, (tq, LANES), 0)
    key16 = lax.broadcasted_iota(I32, (NSA_KV * tq, LANES), 1)

    def body(it, carry):
        m, l, acc = carry
        slot = it & 1

        @pl.when(it + 1 < n)
        def _():
            page_copy(1 - slot, it + 1).start()

        page_copy(slot, it).wait()
        pg = buf[slot]
        lp = lpage_ref[b, it]
        pos = lp * PAGE + key
        dist = (tpos - pos).astype(F32)
        sel_lo = jnp.sum(jnp.where(blk_idx == 2 * lp, selm, 0.0), axis=-1, keepdims=True)
        sel_hi = jnp.sum(jnp.where(blk_idx == 2 * lp + 1, selm, 0.0), axis=-1, keepdims=True)
        mk16 = jnp.where(key16 < BLK, sel_lo, sel_hi) > 0.5
        masks = [mk16[kv * tq:(kv + 1) * tq] for kv in range(NSA_KV)]
        s3 = _dot(q2, pg[:LANES, :].astype(BF16)).reshape(NSA_HEADS, tq, LANES)
        vt_bf = pg[LANES:, :].astype(BF16)
        return _online_step(s3, lambda h: masks[h // GQA], dist, lambda p: _dot_nt(p, vt_bf), m, l, acc, tq)

    init = (jnp.full((rows_all, LANES), NEG_INF, F32), jnp.zeros((rows_all, LANES), F32),
            jnp.zeros((rows_all, LANES), F32))
    m, l, acc = lax.fori_loop(0, n, body, init)

    kvn = kvn_ref[...]
    zpad = jnp.zeros((LANES - tq, LANES), F32)
    k_new = jnp.concatenate([kvn[:, 2 * LANES:3 * LANES], zpad], axis=0).astype(BF16)
    v_new = jnp.concatenate([kvn[:, 3 * LANES:4 * LANES], zpad], axis=0).astype(BF16)
    pos = past_len + key
    mk_new = (key < tq) & (pos <= tpos)
    s3 = _dot_nt(q2, k_new).reshape(NSA_HEADS, tq, LANES)
    m, l, acc = _online_step(s3, lambda h: mk_new, (tpos - pos).astype(F32), lambda p: _dot(p, v_new),
                             m, l, acc, tq)
    o_sel = acc / jnp.maximum(l, TINY)

    win = win_ref[0]
    wl = WINDOW + LANES
    kw_new = jnp.concatenate([kvn[:, 4 * LANES:5 * LANES], zpad], axis=0).astype(BF16)
    vw_new = jnp.concatenate([kvn[:, 5 * LANES:6 * LANES], zpad], axis=0).astype(BF16)
    widx = lax.broadcasted_iota(I32, (tq, wl), 1)
    pw = past_len - WINDOW + widx
    dw = (past_len + lax.broadcasted_iota(I32, (tq, wl), 0)) - pw
    mw = (dw >= 0) & (dw < WINDOW) & (pw >= 0) & (widx < WINDOW + tq)
    s = jnp.concatenate([_dot(q2, win[:LANES, :].astype(BF16)), _dot_nt(q2, kw_new)], axis=1)
    pws = _masked_softmax_heads(s.reshape(NSA_HEADS, tq, wl), lambda h: mw, dw.astype(F32))
    p_win = jnp.concatenate(pws, axis=0).astype(BF16)
    o_win = _dot_nt(p_win[:, :WINDOW], win[LANES:, :].astype(BF16)) + _dot(p_win[:, WINDOW:], vw_new)

    o_ref[...] = _combine_heads(g_ref[...], ocmp_ref[0], o_sel, o_win, tq)


def _nsa_sample_sel(cache, layer, pages, lpages, cnt, selm, u_q, u_gate, u_kv, o_cmp, win_cache,
                    row0, nb, tq, past_len):
    wq = NSA_HEADS * HEAD_DIM
    base = row0 // tq
    n_cmp = past_len // BLK
    grid_spec = pltpu.PrefetchScalarGridSpec(
        num_scalar_prefetch=3,
        grid=(nb,),
        in_specs=[
            pl.BlockSpec(memory_space=pl.ANY),
            pl.BlockSpec((1, NSA_KV * tq, n_cmp), lambda bb, *_: (bb, 0, 0)),
            pl.BlockSpec((tq, wq), lambda bb, *_: (base + bb, 0)),
            pl.BlockSpec((tq, LANES), lambda bb, *_: (base + bb, 0)),
            pl.BlockSpec((tq, 6 * LANES), lambda bb, *_: (base + bb, 0)),
            pl.BlockSpec((1, NSA_HEADS * tq, LANES), lambda bb, *_: (bb, 0, 0)),
            pl.BlockSpec((None, 1, 2 * LANES, WINDOW), lambda bb, *_: (layer, bb, 0, 0)),
        ],
        out_specs=pl.BlockSpec((tq, wq), lambda bb, *_: (bb, 0)),
        scratch_shapes=[pltpu.VMEM((2, 2 * LANES, PAGE), F32), pltpu.SemaphoreType.DMA((2,))],
    )
    return pl.pallas_call(
        functools.partial(_nsa_sample_sel_kernel, layer=layer, tq=tq, past_len=past_len),
        grid_spec=grid_spec,
        out_shape=jax.ShapeDtypeStruct((nb * tq, wq), F32),
        compiler_params=_cp(("arbitrary",)),
        name="nsa_sample_sel",
    )(pages, lpages, cnt, cache, selm, u_q, u_gate, u_kv, o_cmp, win_cache)


def _outproj_kernel(x_ref, cp_ref, cs_ref, ap_ref, as_ref, rp_ref, rs_ref, wo_ref, g_ref, b_ref, wr_ref, br_ref,
                    h_ref, te_ref, tg_ref, cnt_ref, *, alpha, n_prompt_tiles):
    is_sample = pl.program_id(0) >= n_prompt_tiles
    pick = lambda p_ref, s_ref: jnp.where(is_sample, s_ref[...], p_ref[...]).astype(BF16)
    mix = (_dot(pick(cp_ref, cs_ref), wo_ref[0:CONV_CH, :])
           + _dot(pick(ap_ref, as_ref), wo_ref[CONV_CH:CONV_CH + NSA_HEADS * HEAD_DIM, :])
           + _dot(pick(rp_ref, rs_ref), wo_ref[CONV_CH + NSA_HEADS * HEAD_DIM:, :]))
    h = _layer_norm(alpha * x_ref[...] + mix, g_ref[...], b_ref[...])
    h_ref[...] = h
    h_hi = h.astype(BF16)
    h_lo = (h - h_hi.astype(F32)).astype(BF16)
    r = _dot(h_hi, wr_ref[...]) + _dot(h_lo, wr_ref[...])
    logits = r[:, 0:N_EXPERTS] + r[:, N_EXPERTS:2 * N_EXPERTS] + br_ref[...]
    idx = lax.broadcasted_iota(I32, logits.shape, 1).astype(F32)
    col = lax.broadcasted_iota(I32, (logits.shape[0], TOP_K), 1)
    vals = jnp.zeros((logits.shape[0], TOP_K), F32)
    ids = jnp.zeros((logits.shape[0], TOP_K), F32)
    cur = logits
    hist = jnp.zeros((1, N_EXPERTS), F32)
    for k in range(TOP_K):
        mx = jnp.max(cur, axis=-1, keepdims=True)
        am = jnp.min(jnp.where(cur == mx, idx, float(N_EXPERTS)), axis=-1, keepdims=True)
        vals = jnp.where(col == k, mx, vals)
        ids = jnp.where(col == k, am, ids)
        hit = idx == am
        hist = hist + jnp.sum(jnp.where(hit, 1.0, 0.0), axis=0, keepdims=True)
        cur = jnp.where(hit, -jnp.inf, cur)
    e = jnp.exp(vals - vals[:, 0:1])
    tg_ref[...] = e / jnp.sum(e, axis=-1, keepdims=True)
    te_ref[...] = ids.astype(I32)

    @pl.when(pl.program_id(0) == 0)
    def _():
        cnt_ref[...] = jnp.zeros(cnt_ref.shape, F32)

    cnt_ref[...] = cnt_ref[...] + hist


def _outproj(x_all, mixers, w_out_bf, g, b, w_router, b_router, alpha):
    n, d = x_all.shape
    tm = TOKEN_TILE
    npt = mixers[0][0].shape[0] // tm
    row = lambda c: pl.BlockSpec((tm, c), lambda i: (i, 0))
    row_p = lambda c: pl.BlockSpec((tm, c), lambda i: (jnp.minimum(i, npt - 1), 0))
    row_s = lambda c: pl.BlockSpec((tm, c), lambda i: (jnp.maximum(i - npt, 0), 0))
    full = lambda r, c: pl.BlockSpec((r, c), lambda i: (0, 0))
    mix_specs, mix_args = [], []
    for a_p, a_s in mixers:
        mix_specs += [row_p(a_p.shape[1]), row_s(a_s.shape[1])]
        mix_args += [a_p, a_s]
    return pl.pallas_call(
        functools.partial(_outproj_kernel, alpha=alpha, n_prompt_tiles=npt),
        grid=(n // tm,),
        in_specs=[row(d)] + mix_specs + [
                  full(d, d), full(1, d), full(1, d), full(d, LANES), full(1, N_EXPERTS)],
        out_specs=[row(d), row(TOP_K), row(TOP_K), full(1, N_EXPERTS)],
        out_shape=[jax.ShapeDtypeStruct((n, d), F32),
                   jax.ShapeDtypeStruct((n, TOP_K), I32), jax.ShapeDtypeStruct((n, TOP_K), F32),
                   jax.ShapeDtypeStruct((1, N_EXPERTS), F32)],
        compiler_params=_cp(("arbitrary",)),
        name="outproj",
    )(x_all, *mix_args, w_out_bf, g, b, w_router, b_router)


def _moe_kernel(blk_ref, exp_ref, lo_ref, hi_ref, first_ref, x_ref, wup_ref, bg_ref, bu_ref, wd_ref, bd_ref, o_ref,
                wg_scr, wu_scr, wd_scr):
    i = pl.program_id(0)
    dff = wg_scr.shape[1]

    @pl.when((i == 0) | (exp_ref[i] != exp_ref[jnp.maximum(i - 1, 0)]))
    def _():
        w2 = 2 * LANES
        r = lax.broadcasted_iota(I32, (w2, w2), 0)
        c = lax.broadcasted_iota(I32, (w2, w2), 1)
        perm = jnp.where(r == jnp.where(c < LANES, 2 * c, 2 * (c - LANES) + 1), 1.0, 0.0).astype(BF16)
        for k in range(2 * dff // w2):
            split = _dot(wup_ref[:, k * w2:(k + 1) * w2].astype(BF16), perm).astype(BF16)
            wg_scr[:, k * LANES:(k + 1) * LANES] = split[:, :LANES]
            wu_scr[:, k * LANES:(k + 1) * LANES] = split[:, LANES:]
        wd_scr[...] = wd_ref[...].astype(BF16)

    lo, hi = lo_ref[i], hi_ref[i]

    @pl.when(hi > lo)
    def _():
        x = x_ref[...].astype(BF16)
        g = jnp.minimum(_dot(x, wg_scr[...]) + bg_ref[...], SWIGLU_LIMIT)
        u = jnp.clip(_dot(x, wu_scr[...]) + bu_ref[...], -SWIGLU_LIMIT, SWIGLU_LIMIT)
        act = (u + 1.0) * g * _sigmoid(SWIGLU_ALPHA * g)
        y = _dot(act.astype(BF16), wd_scr[...]) + bd_ref[...]
        row = lax.broadcasted_iota(I32, y.shape, 0)
        mine = (row >= lo) & (row < hi)

        @pl.when(first_ref[i] == 1)
        def _():
            o_ref[...] = jnp.where(mine, y, 0.0)

        @pl.when(first_ref[i] == 0)
        def _():
            o_ref[...] = jnp.where(mine, y, o_ref[...])


def _moe_blocks(xs, items, w_up, bg, bu, w_down, bd, layer):
    r, d = xs.shape
    bm = MOE_TILE
    dff = w_down.shape[2]
    n_items = items[0].shape[0]
    wspec = lambda a, c: pl.BlockSpec((None, None, a, c), lambda i, blk, exp, *_: (layer, exp[i], 0, 0))
    bspec = lambda c: pl.BlockSpec((None, 1, c), lambda i, blk, exp, *_: (exp[i], 0, 0))
    grid_spec = pltpu.PrefetchScalarGridSpec(
        num_scalar_prefetch=5,
        grid=(n_items,),
        in_specs=[
            pl.BlockSpec((bm, d), lambda i, blk, *_: (blk[i], 0)),
            wspec(d, 2 * dff), bspec(dff), bspec(dff), wspec(dff, d), bspec(d),
        ],
        out_specs=pl.BlockSpec((bm, d), lambda i, blk, *_: (blk[i], 0)),
        scratch_shapes=[pltpu.VMEM((d, dff), BF16), pltpu.VMEM((d, dff), BF16), pltpu.VMEM((dff, d), BF16)],
    )
    return pl.pallas_call(
        _moe_kernel,
        grid_spec=grid_spec,
        out_shape=jax.ShapeDtypeStruct((r, d), F32),
        compiler_params=_cp(("arbitrary",)),
        name="moe_experts",
    )(*items, xs, w_up, bg, bu, w_down, bd)


def _moe(h, top_e, counts, w_up, bg, bu, w_down, bd, layer):
    n, d = h.shape
    bm = MOE_TILE
    nk = n * TOP_K
    assert nk % bm == 0
    n_blk = nk // bm
    experts = jnp.arange(N_EXPERTS, dtype=I32)
    flat_e = top_e.T.reshape(nk)
    order = jnp.argsort(flat_e).astype(I32)
    rank = jnp.argsort(order).astype(I32)
    ends = jnp.cumsum(counts)
    starts = ends - counts
    has = ends > starts
    first_blk = starts // bm
    per_e = jnp.where(has, (ends - 1) // bm - first_blk + 1, 0)
    it_end = jnp.cumsum(per_e)
    it_start = it_end - per_e
    n_items = n_blk + N_EXPERTS - 1
    t = jnp.arange(n_items, dtype=I32)
    live = t < it_end[-1]
    e_t = jnp.minimum(jnp.sum((it_end[None, :] <= t[:, None]).astype(I32), axis=1), N_EXPERTS - 1)
    blk_t = first_blk[e_t] + t - it_start[e_t]
    lo = jnp.clip(starts[e_t] - blk_t * bm, 0, bm)
    hi = jnp.clip(ends[e_t] - blk_t * bm, 0, bm)
    e_last = jnp.max(jnp.where(has, experts, 0))
    blk_t = jnp.where(live, blk_t, n_blk - 1)
    e_t = jnp.where(live, e_t, e_last)
    lo = jnp.where(live, lo, 0)
    hi = jnp.where(live, hi, 0)
    first = jnp.concatenate([jnp.ones((1,), bool), blk_t[1:] != blk_t[:-1]]) & live
    items = tuple(a.astype(I32) for a in (blk_t, e_t, lo, hi, first))
    xs = jnp.take(h, order % n, axis=0, mode='clip')
    yb = _moe_blocks(xs, items, w_up, bg, bu, w_down, bd, layer)
    return jnp.take(yb, rank, axis=0, mode='clip')


def _final_kernel(h_ref, *refs, alpha):
    f_refs = refs[:TOP_K]
    tg_ref, p_ref, wple_ref, wplg_ref, g_ref, b_ref, y_ref = refs[TOP_K:]
    h = h_ref[...]
    tg = tg_ref[...]
    ffn = tg[:, 0:1] * f_refs[0][...]
    for k in range(1, TOP_K):
        ffn = ffn + tg[:, k:k + 1] * f_refs[k][...]
    ple = _dot(p_ref[...].astype(BF16), wple_ref[...]) * _sigmoid(_dot(h.astype(BF16), wplg_ref[...]))
    y_ref[...] = _layer_norm(alpha * h + ffn + ple, g_ref[...], b_ref[...])


def _final(h, expert_out, gate, p_all, w_ple_bf, w_plg_bf, g, b, alpha):
    n, d = h.shape
    tm = TOKEN_TILE
    pd = p_all.shape[1]
    row = lambda c: pl.BlockSpec((tm, c), lambda i: (i, 0))
    full = lambda r, c: pl.BlockSpec((r, c), lambda i: (0, 0))
    kth = lambda k: pl.BlockSpec((tm, d), lambda i: (k * (n // tm) + i, 0))
    return pl.pallas_call(
        functools.partial(_final_kernel, alpha=alpha),
        grid=(n // tm,),
        in_specs=[row(d)] + [kth(k) for k in range(TOP_K)]
        + [row(TOP_K), row(pd), full(pd, d), full(d, d), full(1, d), full(1, d)],
        out_specs=row(d),
        out_shape=jax.ShapeDtypeStruct((n, d), F32),
        compiler_params=_cp(("parallel",)),
        name="final",
    )(h, *([expert_out] * TOP_K), gate, p_all, w_ple_bf, w_plg_bf, g, b)


def kernel(x_prompt, x_sample, p_prompt, p_sample, cache_nsa_kv, cache_win_kv, state_ret, state_conv, page_table,
           w_in, w_out, conv_w, conv_b, conv_ln_g, conv_ln_b, ret_norm_g, ln1_g, ln1_b, w_router, b_router,
           w_up, b_up, w_down, b_down, w_ple, w_plg, ln2_g, ln2_b):
    bp, seq, d = x_prompt.shape
    bs, tdec, _ = x_sample.shape
    depth = w_in.shape[0]
    n_pages = page_table.shape[1]
    past_len = n_pages * PAGE
    wbuf = cache_win_kv.shape[2]
    n_p, n_s = bp * seq, bs * tdec
    assert wbuf == WINDOW and tdec <= BLK and tdec % 8 == 0 and past_len // BLK >= N_SEL
    assert seq % NSA_TQ == 0 and seq >= WINDOW + NSA_TQ and n_p % TOKEN_TILE == 0 and n_s % TOKEN_TILE == 0
    alpha = (2 * depth) ** 0.25
    n_pool = cache_nsa_kv.shape[1]
    cache = jnp.transpose(cache_nsa_kv, (0, 1, 3, 4, 5, 2)).reshape(depth, n_pool, 4 * LANES, PAGE)
    win_t = jnp.transpose(cache_win_kv, (0, 1, 3, 4, 5, 2)).reshape(depth, bs, 2 * LANES, wbuf)
    page_table = page_table.astype(I32)

    c_conv = 2 * CONV_CH
    c_q = NSA_HEADS * HEAD_DIM
    c_kv = 6 * NSA_KV * HEAD_DIM
    c_gate = 3 * NSA_HEADS
    cuts = [0, c_conv, c_conv + c_q, c_conv + c_q + c_kv, c_conv + c_q + c_kv + c_gate, w_in.shape[2]]

    x_all = jnp.concatenate([x_prompt.reshape(n_p, d), x_sample.reshape(n_s, d)], axis=0)
    outs = {k: [] for k in ("kvp", "kvs", "wp", "ws", "rp", "rs", "cp", "cs")}
    zeros_ret = jnp.zeros((bp, RET_HEADS, HEAD_DIM, HEAD_DIM), F32)
    zeros_conv = jnp.zeros((bp, CONV_K - 1, CONV_CH), F32)

    for i in range(depth):
        wi = w_in[i].astype(BF16)
        ws = [wi[:, cuts[j]:cuts[j + 1]] for j in range(5)]
        ws[3] = jnp.pad(ws[3], ((0, 0), (0, LANES - c_gate)))
        u_conv, u_q, u_kv, u_gate, u_ret, kv_t = _inproj(x_all, ws, bp, seq)

        cw = jnp.pad(conv_w[i], ((0, 32 - CONV_K), (0, 0)))
        cargs = (cw, conv_b[i][None], conv_ln_g[i][None], conv_ln_b[i][None])
        conv_p, conv_sp = _conv(u_conv, zeros_conv, *cargs, 0, bp, seq)
        conv_s, conv_ss = _conv(u_conv, state_conv[i], *cargs, n_p, bs, tdec)

        rgain = ret_norm_g[i][None]
        ret_p, ret_sp = _retention(u_ret, zeros_ret, rgain, 0, bp, seq)
        ret_s, ret_ss = _retention(u_ret, state_ret[i], rgain, n_p, bs, tdec)

        nsa_p = _nsa_prompt(u_q, u_gate, u_kv, bp, seq)
        o_cmp, selm = _nsa_sample_cmp(cache, i, page_table, u_q, n_p, bs, tdec)
        need = selm.reshape(bs, NSA_KV * tdec, n_pages, PAGE // BLK).max(axis=(1, 3)) > 0.5
        npos = jnp.cumsum(need.astype(I32), axis=1) - 1
        cnt = npos[:, -1] + 1
        pidx = jnp.arange(n_pages, dtype=I32)
        hit = need[:, :, None] & (npos[:, :, None] == pidx[None, None, :])
        order = jnp.sum(jnp.where(hit, pidx[None, :, None], 0), axis=1).astype(I32)
        pages = jnp.take_along_axis(page_table, order, axis=1)
        nsa_s = _nsa_sample_sel(cache, i, pages, order, cnt, selm, u_q, u_gate, u_kv, o_cmp, win_t,
                                n_p, bs, tdec, past_len)

        wr_hi = w_router[i].astype(BF16)
        wr_lo = (w_router[i] - wr_hi.astype(F32)).astype(BF16)
        wr_cat = jnp.pad(jnp.concatenate([wr_hi, wr_lo], axis=1), ((0, 0), (0, LANES - 2 * N_EXPERTS)))
        h, top_e, gate, counts = _outproj(x_all, ((conv_p, conv_s), (nsa_p, nsa_s), (ret_p, ret_s)),
                                          w_out[i].astype(BF16),
                                          ln1_g[i][None], ln1_b[i][None], wr_cat, b_router[i][None], alpha)

        expert_out = _moe(h, top_e, counts[0].astype(I32), w_up, b_up[i][:, None, 0::2], b_up[i][:, None, 1::2],
                          w_down, b_down[i][:, None, :], i)

        p_all = jnp.concatenate([p_prompt[i].reshape(n_p, -1), p_sample[i].reshape(n_s, -1)], axis=0)
        x_all = _final(h, expert_out, gate, p_all, w_ple[i].astype(BF16), w_plg[i].astype(BF16),
                       ln2_g[i][None], ln2_b[i][None], alpha)

        kv_shape = (4, NSA_KV, HEAD_DIM)
        outs["kvp"].append(jnp.transpose(kv_t[:, :4 * LANES].reshape(bp, *kv_shape, seq), (0, 4, 1, 2, 3)))
        outs["kvs"].append(u_kv[n_p:, :4 * LANES].reshape(bs, tdec, *kv_shape))
        n_keep = min(WINDOW, seq)
        win_t_p = kv_t[:, 4 * LANES:, seq - n_keep:].reshape(bp, 2, NSA_KV, HEAD_DIM, n_keep)
        outs["wp"].append(jnp.transpose(win_t_p, (0, 4, 1, 2, 3)))
        win_new_s = u_kv[n_p:, 4 * LANES:].reshape(bs, tdec, 2, NSA_KV, HEAD_DIM)
        outs["ws"].append(jnp.concatenate([cache_win_kv[i][:, tdec:], win_new_s], axis=1))
        outs["rp"].append(ret_sp)
        outs["rs"].append(ret_ss)
        outs["cp"].append(conv_sp)
        outs["cs"].append(conv_ss)

    y_p = x_all[:n_p].reshape(bp, seq, d)
    y_s = x_all[n_p:].reshape(bs, tdec, d)
    st = lambda k: jnp.stack(outs[k])
    return (y_p, y_s, st("kvp"), st("kvs"), st("wp"), st("ws"), st("rp"), st("rs"), st("cp"), st("cs"))
```

```python
import functools

import jax
import jax.numpy as jnp
from jax import lax
from jax.experimental import pallas as pl
from jax.experimental.pallas import tpu as pltpu

F32 = jnp.float32
BF16 = jnp.bfloat16
I32 = jnp.int32

HEAD_DIM = 64
CONV_CH = 256
CONV_K = 31
NSA_HEADS = 8
NSA_KV = 2
GQA = 4
BLK = 64
N_SEL = 16
WINDOW = 512
FORCE_SCORE = 1e4
RET_HEADS = 4
RET_CHUNK = 128
N_EXPERTS = 32
TOP_K = 4
SWIGLU_LIMIT = 7.0
SWIGLU_ALPHA = 1.702
PAGE = 128
LN_EPS = 1e-5
NEG_INF = -1e30
TINY = 1e-30
LANES = 128

TOKEN_TILE = 256
MOE_TILE = 256
NSA_TQ = 128
SAMPLE_PAGES_PER_STEP = 16
VMEM_LIMIT = 56 * 1024 * 1024


def _cp(sem):
    return pltpu.CompilerParams(dimension_semantics=sem, vmem_limit_bytes=VMEM_LIMIT)


def _dot(a, b):
    return jnp.dot(a, b, preferred_element_type=F32)


def _dot_nt(a, b):
    return lax.dot_general(a, b, (((1,), (1,)), ((), ())), preferred_element_type=F32)


def _dot_tn(a, b):
    return lax.dot_general(a, b, (((0,), (0,)), ((), ())), preferred_element_type=F32)


def _layer_norm(x, g, b):
    mu = jnp.mean(x, axis=-1, keepdims=True)
    xc = x - mu
    var = jnp.mean(xc * xc, axis=-1, keepdims=True)
    return xc * lax.rsqrt(var + LN_EPS) * g + b


def _sigmoid(x):
    return 1.0 / (1.0 + jnp.exp(-x))


def _inproj_kernel(x_ref, wc_ref, wq_ref, wkv_ref, wg_ref, wr_ref, oc_ref, oq_ref, okv_ref, og_ref, or_ref, okvt_ref,
                   *, n_prompt_tiles):
    x = x_ref[...].astype(BF16)
    oc_ref[...] = _dot(x, wc_ref[...])
    oq_ref[...] = _dot(x, wq_ref[...])
    ukv = _dot(x, wkv_ref[...])
    okv_ref[...] = ukv
    og_ref[...] = _dot(x, wg_ref[...])
    or_ref[...] = _dot(x, wr_ref[...])

    @pl.when(pl.program_id(0) < n_prompt_tiles)
    def _():
        okvt_ref[...] = ukv.T


def _inproj(x_all, ws, bp, seq):
    n, d = x_all.shape
    tm = TOKEN_TILE
    widths = [w.shape[1] for w in ws]
    c_kv = widths[2]
    tiles_per_seq = seq // tm
    npt = bp * tiles_per_seq
    kvt_spec = pl.BlockSpec((None, c_kv, tm), lambda i: (jnp.minimum(i, npt - 1) // tiles_per_seq, 0,
                                                          jnp.minimum(i, npt - 1) % tiles_per_seq))
    return pl.pallas_call(
        functools.partial(_inproj_kernel, n_prompt_tiles=npt),
        grid=(n // tm,),
        in_specs=[pl.BlockSpec((tm, d), lambda i: (i, 0))]
        + [pl.BlockSpec((d, c), lambda i: (0, 0)) for c in widths],
        out_specs=[pl.BlockSpec((tm, c), lambda i: (i, 0)) for c in widths] + [kvt_spec],
        out_shape=[jax.ShapeDtypeStruct((n, c), F32) for c in widths]
        + [jax.ShapeDtypeStruct((bp, c_kv, seq), F32)],
        compiler_params=_cp(("arbitrary",)),
        name="inproj",
    )(x_all, *ws)


def _conv_kernel(u_ref, s0_ref, w_ref, cb_ref, g_ref, b_ref, o_ref, st_ref, ext, *, tt):
    i = pl.program_id(1)
    pad = 32 - (CONV_K - 1)

    @pl.when(i == 0)
    def _():
        ext[0:pad, :] = jnp.zeros((pad, CONV_CH), F32)
        ext[pad:32, :] = s0_ref[0]

    u = u_ref[...]
    h = u[:, :CONV_CH] * _sigmoid(u[:, CONV_CH:])
    ext[32:32 + tt, :] = h
    acc = jnp.zeros((tt, CONV_CH), F32)
    for j in range(CONV_K):
        acc = acc + ext[pad + j:pad + j + tt, :] * w_ref[j:j + 1, :]
    y = _layer_norm(acc + cb_ref[...], g_ref[...], b_ref[...])
    o_ref[...] = y * _sigmoid(y)
    tail = ext[tt + pad:tt + 32, :]

    @pl.when(i == pl.num_programs(1) - 1)
    def _():
        st_ref[0] = tail

    ext[pad:32, :] = tail


def _conv(u_conv, s0, w, cb, g, b, row0, nb, t):
    tt = min(t, 256)
    nt = t // tt
    base = row0 // tt
    in_specs = [
        pl.BlockSpec((tt, 2 * CONV_CH), lambda bb, i: (base + bb * nt + i, 0)),
        pl.BlockSpec((1, CONV_K - 1, CONV_CH), lambda bb, i: (bb, 0, 0)),
        pl.BlockSpec((32, CONV_CH), lambda bb, i: (0, 0)),
        pl.BlockSpec((1, CONV_CH), lambda bb, i: (0, 0)),
        pl.BlockSpec((1, CONV_CH), lambda bb, i: (0, 0)),
        pl.BlockSpec((1, CONV_CH), lambda bb, i: (0, 0)),
    ]
    args = [u_conv, s0, w, cb, g, b]
    return pl.pallas_call(
        functools.partial(_conv_kernel, tt=tt),
        grid=(nb, nt),
        in_specs=in_specs,
        out_specs=[
            pl.BlockSpec((tt, CONV_CH), lambda bb, i: (bb * nt + i, 0)),
            pl.BlockSpec((1, CONV_K - 1, CONV_CH), lambda bb, i: (bb, 0, 0)),
        ],
        out_shape=[
            jax.ShapeDtypeStruct((nb * t, CONV_CH), F32),
            jax.ShapeDtypeStruct((nb, CONV_K - 1, CONV_CH), F32),
        ],
        scratch_shapes=[pltpu.VMEM((32 + tt, CONV_CH), F32)],
        compiler_params=_cp(("parallel", "arbitrary")),
        name="conv",
    )(*args)


def _ret_kernel(u_ref, s0_ref, dm_ref, qd_ref, kd_ref, cd_ref, g_ref, o_ref, st_ref, s_scr):
    i = pl.program_id(1)

    @pl.when(i == 0)
    def _():
        s_scr[...] = s0_ref[0]

    u = u_ref[...]
    w = RET_HEADS * HEAD_DIM
    outs = []
    for h in range(RET_HEADS):
        lo = h * HEAD_DIM
        q = u[:, lo:lo + HEAD_DIM]
        k = u[:, w + lo:w + lo + HEAD_DIM] * (HEAD_DIM ** -0.5)
        v = u[:, 2 * w + lo:2 * w + lo + HEAD_DIM]
        rg = u[:, 3 * w + lo:3 * w + lo + HEAD_DIM]
        qb, kb, vb = q.astype(BF16), k.astype(BF16), v.astype(BF16)
        att = _dot_nt(qb, kb) * dm_ref[h]
        s_h = s_scr[h]
        o = _dot(att.astype(BF16), vb) + _dot(qb, s_h.astype(BF16)) * qd_ref[h]
        kdec = (k * kd_ref[h]).astype(BF16)
        s_scr[h] = s_h * cd_ref[h] + _dot_tn(kdec, vb)
        mu = jnp.mean(o, axis=-1, keepdims=True)
        oc = o - mu
        var = jnp.mean(oc * oc, axis=-1, keepdims=True)
        on = oc * lax.rsqrt(var + LN_EPS) * g_ref[:, lo:lo + HEAD_DIM]
        outs.append(rg * _sigmoid(rg) * on)
    o_ref[...] = jnp.concatenate(outs, axis=1)

    @pl.when(i == pl.num_programs(1) - 1)
    def _():
        st_ref[0] = s_scr[...]


def _ret_tables(c):
    log_g = jnp.log1p(-jnp.exp2(-5.0 - jnp.arange(RET_HEADS, dtype=F32)))
    i = jnp.arange(c, dtype=F32)
    diff = i[:, None] - i[None, :]
    dmask = jnp.exp(jnp.where(diff >= 0, log_g[:, None, None] * diff, -jnp.inf))
    q_dec = jnp.exp(log_g[:, None] * (i[None, :] + 1.0))
    k_dec = jnp.exp(log_g[:, None] * (c - 1.0 - i[None, :]))
    c_dec = jnp.exp(log_g * c)
    bc = lambda a: jnp.broadcast_to(a[:, :, None], a.shape + (HEAD_DIM,))
    cd = jnp.broadcast_to(c_dec[:, None, None], (RET_HEADS, HEAD_DIM, HEAD_DIM))
    return dmask, bc(q_dec), bc(k_dec), cd


def _retention(u_ret, s0, gain, row0, nb, t):
    c = min(t, RET_CHUNK)
    while t % c:
        c -= 1
    nt = t // c
    base = row0 // c
    dm, qd, kd, cd = _ret_tables(c)
    w = RET_HEADS * HEAD_DIM
    full = lambda shape: pl.BlockSpec(shape, lambda bb, i: (0,) * len(shape))
    in_specs = [
        pl.BlockSpec((c, 4 * w), lambda bb, i: (base + bb * nt + i, 0)),
        pl.BlockSpec((1, RET_HEADS, HEAD_DIM, HEAD_DIM), lambda bb, i: (bb, 0, 0, 0)),
        full((RET_HEADS, c, c)),
        full((RET_HEADS, c, HEAD_DIM)),
        full((RET_HEADS, c, HEAD_DIM)),
        full((RET_HEADS, HEAD_DIM, HEAD_DIM)),
        full((1, w)),
    ]
    args = [u_ret, s0, dm, qd, kd, cd, gain]
    return pl.pallas_call(
        _ret_kernel,
        grid=(nb, nt),
        in_specs=in_specs,
        out_specs=[
            pl.BlockSpec((c, w), lambda bb, i: (bb * nt + i, 0)),
            pl.BlockSpec((1, RET_HEADS, HEAD_DIM, HEAD_DIM), lambda bb, i: (bb, 0, 0, 0)),
        ],
        out_shape=[
            jax.ShapeDtypeStruct((nb * t, w), F32),
            jax.ShapeDtypeStruct((nb, RET_HEADS, HEAD_DIM, HEAD_DIM), F32),
        ],
        scratch_shapes=[pltpu.VMEM((RET_HEADS, HEAD_DIM, HEAD_DIM), F32)],
        compiler_params=_cp(("parallel", "arbitrary")),
        name="retention",
    )(*args)


def _slope(h):
    return 2.0 ** -(h + 1)


def _build_q2(uq, tq):
    lane = lax.broadcasted_iota(I32, (tq, LANES), 1)
    parts = []
    for h in range(NSA_HEADS):
        kvh = h // GQA
        p = uq[:, (h // 2) * LANES:(h // 2 + 1) * LANES]
        if (h % 2) != kvh:
            p = pltpu.roll(p, HEAD_DIM, 1)
        keep = (lane >= HEAD_DIM) if kvh == 1 else (lane < HEAD_DIM)
        parts.append(jnp.where(keep, p * (HEAD_DIM ** -0.5), 0.0))
    return jnp.concatenate(parts, axis=0).astype(BF16)


def _masked_softmax_heads(s3, mask_of_head, dist):
    ps = []
    for h in range(NSA_HEADS):
        mk = mask_of_head(h)
        sh = jnp.where(mk, s3[h] - _slope(h) * dist, NEG_INF)
        mx = jnp.max(sh, axis=-1, keepdims=True)
        e = jnp.where(mk, jnp.exp(sh - mx), 0.0)
        den = jnp.maximum(jnp.sum(e, axis=-1, keepdims=True), TINY)
        ps.append(e / den)
    return ps


def _online_step(s3, mask_of_head, dist, pv, m, l, acc, tq):
    ps, alphas, ms, ls = [], [], [], []
    for h in range(NSA_HEADS):
        mk = mask_of_head(h)
        rows = slice(h * tq, (h + 1) * tq)
        sh = jnp.where(mk, s3[h] - _slope(h) * dist, NEG_INF)
        m_old = m[rows]
        m_new = jnp.maximum(m_old, jnp.max(sh, axis=-1, keepdims=True))
        p = jnp.where(mk, jnp.exp(sh - m_new), 0.0)
        alpha = jnp.exp(m_old - m_new)
        ls.append(alpha * l[rows] + jnp.sum(p, axis=-1, keepdims=True))
        ms.append(m_new)
        alphas.append(alpha)
        ps.append(p)
    p_all = jnp.concatenate(ps, axis=0).astype(BF16)
    alpha_all = jnp.concatenate(alphas, axis=0)
    acc = acc * alpha_all + pv(p_all)
    return jnp.concatenate(ms, axis=0), jnp.concatenate(ls, axis=0), acc


def _alibi_query_lanes(tq):
    lane = lax.broadcasted_iota(I32, (tq, LANES), 1)
    parts = [jnp.where(lane == 0, _slope(h), jnp.where(lane == 1, _slope(h) * LANES, 0.0))
             for h in range(NSA_HEADS)]
    return jnp.concatenate(parts, axis=0).astype(BF16)


def _alibi_key_lanes(pos):
    lane = lax.broadcasted_iota(I32, pos.shape, 1)
    return jnp.where(lane == 0, pos & (LANES - 1), jnp.where(lane == 1, pos >> 7, 0)).astype(F32).astype(BF16)


def _combine_heads(gl, o_cmp, o_sel, o_win, tq):
    gs = _sigmoid(gl)
    lane = lax.broadcasted_iota(I32, (tq, LANES), 1)
    pairs = []
    for mpair in range(NSA_HEADS // 2):
        halves = []
        for h in (2 * mpair, 2 * mpair + 1):
            rows = slice(h * tq, (h + 1) * tq)
            o = (gs[:, 3 * h:3 * h + 1] * o_cmp[rows] + gs[:, 3 * h + 1:3 * h + 2] * o_sel[rows]
                 + gs[:, 3 * h + 2:3 * h + 3] * o_win[rows])
            if (h // GQA) != (h % 2):
                o = pltpu.roll(o, HEAD_DIM, 1)
            halves.append(o)
        pairs.append(jnp.where(lane < HEAD_DIM, halves[0], halves[1]))
    return jnp.concatenate(pairs, axis=1)


def _cmp_probs(s, t0, tq, n_lanes):
    s3 = s.reshape(NSA_HEADS, tq, n_lanes)
    tpos = t0 + lax.broadcasted_iota(I32, (tq, n_lanes), 0)
    cend = lax.broadcasted_iota(I32, (tq, n_lanes), 1) * BLK + (BLK - 1)
    valid = cend <= tpos
    dist = (tpos - cend).astype(F32)
    return _masked_softmax_heads(s3, lambda h: valid, dist)


def _block_scores(ps, kv, t0, tq, n_lanes):
    tpos = t0 + lax.broadcasted_iota(I32, (tq, n_lanes), 0)
    nb = lax.broadcasted_iota(I32, (tq, n_lanes), 1)
    cur = tpos >> 6
    imp = ps[kv * GQA] + ps[kv * GQA + 1] + ps[kv * GQA + 2] + ps[kv * GQA + 3]
    forced = (nb == 0) | (nb == cur) | (nb == cur - 1)
    allowed = nb <= cur
    return jnp.where(allowed, jnp.where(forced, FORCE_SCORE, imp), -1.0), allowed, nb


def _nsa_prompt_kernel(q_ref, g_ref, kv_ref, o_ref, kb_ref, kc_ref, e_ref, sel_ref, q2e_ref, m_ref, l_ref, acc_ref,
                       *, t, tq):
    i = pl.program_id(1)
    n_blk = t // BLK
    rows_all = NSA_HEADS * tq

    @pl.when(i == 0)
    def _():
        kc_ref[...] = jnp.zeros(kc_ref.shape, F32)
        step = 256
        for c in range(t // step):
            sl = slice(c * step, (c + 1) * step)
            pos_lanes = _alibi_key_lanes(c * step + lax.broadcasted_iota(I32, (step, LANES), 0))
            kb_ref[sl, 0:LANES] = kv_ref[sl, 2 * LANES:3 * LANES].astype(BF16)
            kb_ref[sl, LANES:2 * LANES] = pos_lanes
            kb_ref[sl, 2 * LANES:3 * LANES] = kv_ref[sl, 4 * LANES:5 * LANES].astype(BF16)
            kb_ref[sl, 3 * LANES:4 * LANES] = pos_lanes
            kb_ref[sl, 4 * LANES:5 * LANES] = kv_ref[sl, 3 * LANES:4 * LANES].astype(BF16)
            kb_ref[sl, 5 * LANES:6 * LANES] = kv_ref[sl, 5 * LANES:6 * LANES].astype(BF16)
            kc_ref[c * (step // BLK):(c + 1) * (step // BLK), :] = (
                kv_ref[sl, 0:2 * LANES].reshape(step // BLK, BLK, 2 * LANES).sum(axis=1) * (1.0 / BLK))
        blk_of_key = lax.broadcasted_iota(I32, (LANES, t), 1) >> 6
        e_ref[...] = jnp.where(blk_of_key == lax.broadcasted_iota(I32, (LANES, t), 0), 1.0, 0.0).astype(BF16)

    t0 = i * tq
    q2 = _build_q2(q_ref[...], tq)

    kc = kc_ref[0:n_blk, :]
    s_t = _dot_nt(kc[:, :LANES].astype(BF16), q2)
    blk = lax.broadcasted_iota(I32, (n_blk, tq), 0)
    tpos_c = t0 + lax.broadcasted_iota(I32, (n_blk, tq), 1)
    cend = blk * BLK + (BLK - 1)
    valid = cend <= tpos_c
    dist_c = (tpos_c - cend).astype(F32)
    p_t = []
    for h in range(NSA_HEADS):
        sh = jnp.where(valid, s_t[:, h * tq:(h + 1) * tq] - _slope(h) * dist_c, NEG_INF)
        e = jnp.where(valid, jnp.exp(sh - jnp.max(sh, axis=0, keepdims=True)), 0.0)
        p_t.append(e / jnp.maximum(jnp.sum(e, axis=0, keepdims=True), TINY))
    o_cmp = _dot_tn(jnp.concatenate(p_t, axis=1).astype(BF16), kc[:, LANES:].astype(BF16))

    cur = tpos_c >> 6
    allowed = blk <= cur
    forced = (blk == 0) | (blk == cur) | (blk == cur - 1)
    eye = jnp.where(lax.broadcasted_iota(I32, (n_blk, LANES), 0) == lax.broadcasted_iota(I32, (n_blk, LANES), 1),
                    1.0, 0.0).astype(BF16)
    for kv in range(NSA_KV):
        imp = p_t[kv * GQA] + p_t[kv * GQA + 1] + p_t[kv * GQA + 2] + p_t[kv * GQA + 3]
        score = jnp.where(allowed, jnp.where(forced, FORCE_SCORE, imp), -1.0)
        rank = jnp.zeros((n_blk, tq), F32)
        for mblk in range(n_blk):
            c = score[mblk:mblk + 1, :]
            tie = jnp.where(blk > mblk, 1.0, 0.0)
            rank = rank + jnp.where(c > score, 1.0, jnp.where(c == score, tie, 0.0))
        sel_t = jnp.where(allowed, jnp.where(rank < N_SEL, 1.0, 0.0), 0.0).astype(BF16)
        sel_ref[kv] = _dot_tn(sel_t, eye).astype(BF16)

    q2e_ref[...] = jnp.concatenate([q2, _alibi_query_lanes(tq)], axis=1)

    m_ref[...] = jnp.full((rows_all, LANES), NEG_INF, F32)
    l_ref[...] = jnp.zeros((rows_all, LANES), F32)
    acc_ref[...] = jnp.zeros((rows_all, LANES), F32)
    tk = 2 * tq
    tpos = t0 + lax.broadcasted_iota(I32, (tq, tk), 0)

    def body(kt, carry):
        k0 = pl.multiple_of(kt * tk, tk)
        k_bf = kb_ref[pl.ds(k0, tk), 0:2 * LANES]
        v_bf = kb_ref[pl.ds(k0, tk), 4 * LANES:5 * LANES]
        causal = (k0 + lax.broadcasted_iota(I32, (tq, tk), 1)) <= tpos
        e_tile = e_ref[:, pl.ds(k0, tk)]
        biases = [jnp.where(causal, jnp.where(_dot(sel_ref[kv], e_tile) > 0.5, 0.0, NEG_INF), NEG_INF)
                  for kv in range(NSA_KV)]
        s3 = _dot_nt(q2e_ref[...], k_bf).reshape(NSA_HEADS, tq, tk)
        m_all, l_all = m_ref[...], l_ref[...]
        ps, alphas, ms, ls = [], [], [], []
        for h in range(NSA_HEADS):
            rows = slice(h * tq, (h + 1) * tq)
            sh = s3[h] + biases[h // GQA]
            m_old = m_all[rows]
            m_new = jnp.maximum(m_old, jnp.max(sh, axis=-1, keepdims=True))
            p = jnp.exp(sh - jnp.concatenate([m_new, m_new], axis=1))
            alpha = jnp.exp(m_old - m_new)
            ls.append(alpha * l_all[rows] + p[:, :LANES] + p[:, LANES:])
            ms.append(m_new)
            alphas.append(alpha)
            ps.append(p.astype(BF16))
        acc_ref[...] = acc_ref[...] * jnp.concatenate(alphas, axis=0) + _dot(jnp.concatenate(ps, axis=0), v_bf)
        m_ref[...] = jnp.concatenate(ms, axis=0)
        l_ref[...] = jnp.concatenate(ls, axis=0)
        return carry

    lax.fori_loop(0, (i + 2) // 2, body, 0)
    o_sel = acc_ref[...] / jnp.sum(l_ref[...], axis=-1, keepdims=True)

    wl = WINDOW + tq
    ws = pl.multiple_of(jnp.maximum(t0 - WINDOW, 0), tq)
    kw = kb_ref[pl.ds(ws, wl), 2 * LANES:4 * LANES]
    vw = kb_ref[pl.ds(ws, wl), 5 * LANES:6 * LANES]
    dw = (t0 + lax.broadcasted_iota(I32, (tq, wl), 0)) - (ws + lax.broadcasted_iota(I32, (tq, wl), 1))
    bias_w = jnp.where(dw >= 0, jnp.where(dw < WINDOW, 0.0, NEG_INF), NEG_INF)
    s3 = _dot_nt(q2e_ref[...], kw).reshape(NSA_HEADS, tq, wl)
    es, sums = [], []
    for h in range(NSA_HEADS):
        sh = s3[h] + bias_w
        e = jnp.exp(sh - jnp.max(sh, axis=-1, keepdims=True))
        es.append(e.astype(BF16))
        sums.append(jnp.broadcast_to(jnp.sum(e, axis=-1, keepdims=True), (tq, LANES)))
    o_win = _dot(jnp.concatenate(es, axis=0), vw) / jnp.concatenate(sums, axis=0)

    o_ref[...] = _combine_heads(g_ref[...], o_cmp, o_sel, o_win, tq)


def _nsa_prompt(u_q, u_gate, u_kv, nb, t):
    tq = NSA_TQ
    nt = t // tq
    wq = NSA_HEADS * HEAD_DIM
    in_specs = [
        pl.BlockSpec((tq, wq), lambda bb, i: (bb * nt + i, 0)),
        pl.BlockSpec((tq, LANES), lambda bb, i: (bb * nt + i, 0)),
        pl.BlockSpec((t, 6 * LANES), lambda bb, i: (bb, 0)),
    ]
    args = [u_q, u_gate, u_kv]
    rows_all = NSA_HEADS * tq
    return pl.pallas_call(
        functools.partial(_nsa_prompt_kernel, t=t, tq=tq),
        grid=(nb, nt),
        in_specs=in_specs,
        out_specs=pl.BlockSpec((tq, wq), lambda bb, i: (bb * nt + i, 0)),
        out_shape=jax.ShapeDtypeStruct((nb * t, wq), F32),
        scratch_shapes=[
            pltpu.VMEM((t, 6 * LANES), BF16),
            pltpu.VMEM((LANES, 2 * LANES), F32),
            pltpu.VMEM((LANES, t), BF16),
            pltpu.VMEM((NSA_KV, tq, LANES), BF16),
            pltpu.VMEM((rows_all, 2 * LANES), BF16),
            pltpu.VMEM((rows_all, LANES), F32),
            pltpu.VMEM((rows_all, LANES), F32),
            pltpu.VMEM((rows_all, LANES), F32),
        ],
        compiler_params=_cp(("parallel", "arbitrary")),
        name="nsa_prompt",
    )(*args)


def _nsa_sample_cmp_kernel(pt_ref, *refs, pp, tq, past_len):
    page_refs = refs[:pp]
    q_ref, ocmp_ref, score_ref, cmp_scr = refs[pp:]
    j = pl.program_id(1)
    n_cmp = past_len // BLK
    per_step = pp * (PAGE // BLK)
    rows = []
    for k in range(pp):
        pg_t = page_refs[k][...].T
        rows.append(pg_t.reshape(PAGE // BLK, BLK, 2 * LANES).sum(axis=1) * (1.0 / BLK))
    cmp_scr[pl.ds(pl.multiple_of(j * per_step, per_step), per_step), :] = jnp.concatenate(rows, axis=0)

    @pl.when(j == pl.num_programs(1) - 1)
    def _():
        q2 = _build_q2(q_ref[...], tq)
        kc = cmp_scr[...]
        ps = _cmp_probs(_dot_nt(q2, kc[:, :LANES].astype(BF16)), past_len, tq, n_cmp)
        ocmp_ref[0] = _dot(jnp.concatenate(ps, axis=0).astype(BF16), kc[:, LANES:].astype(BF16))
        scores = [_block_scores(ps, kv, past_len, tq, n_cmp)[0] for kv in range(NSA_KV)]
        score_ref[0] = jnp.concatenate(scores, axis=0)


def _topk_blocks_kernel(sc_ref, sel_ref, *, k):
    sc = sc_ref[...]
    n = sc.shape[1]
    idx = lax.broadcasted_iota(I32, sc.shape, 1).astype(F32)
    sel = jnp.zeros(sc.shape, F32)
    for _ in range(k):
        mx = jnp.max(sc, axis=-1, keepdims=True)
        am = jnp.min(jnp.where(sc == mx, idx, float(n)), axis=-1, keepdims=True)
        hit = idx == am
        sel = jnp.where(hit, 1.0, sel)
        sc = jnp.where(hit, -2.0, sc)
    sel_ref[...] = sel


def _topk_blocks(scores, k):
    return pl.pallas_call(
        functools.partial(_topk_blocks_kernel, k=k),
        out_shape=jax.ShapeDtypeStruct(scores.shape, F32),
        compiler_params=pltpu.CompilerParams(vmem_limit_bytes=VMEM_LIMIT),
        name="topk_blocks",
    )(scores)


def _nsa_sample_cmp(cache, layer, page_table, u_q, row0, nb, tq):
    n_pages = page_table.shape[1]
    past_len = n_pages * PAGE
    n_cmp = past_len // BLK
    pp = min(SAMPLE_PAGES_PER_STEP, n_pages)
    steps = n_pages // pp
    wq = NSA_HEADS * HEAD_DIM
    base = row0 // tq

    def page_spec(k):
        return pl.BlockSpec((None, None, 2 * LANES, PAGE),
                            lambda bb, j, pt: (layer, pt[bb, j * pp + k], 0, 0))

    grid_spec = pltpu.PrefetchScalarGridSpec(
        num_scalar_prefetch=1,
        grid=(nb, steps),
        in_specs=[page_spec(k) for k in range(pp)]
        + [pl.BlockSpec((tq, wq), lambda bb, j, pt: (base + bb, 0))],
        out_specs=[
            pl.BlockSpec((1, NSA_HEADS * tq, LANES), lambda bb, j, pt: (bb, 0, 0)),
            pl.BlockSpec((1, NSA_KV * tq, n_cmp), lambda bb, j, pt: (bb, 0, 0)),
        ],
        scratch_shapes=[pltpu.VMEM((n_cmp, 2 * LANES), F32)],
    )
    return pl.pallas_call(
        functools.partial(_nsa_sample_cmp_kernel, pp=pp, tq=tq, past_len=past_len),
        grid_spec=grid_spec,
        out_shape=[
            jax.ShapeDtypeStruct((nb, NSA_HEADS * tq, LANES), F32),
            jax.ShapeDtypeStruct((nb, NSA_KV * tq, n_cmp), F32),
        ],
        compiler_params=_cp(("parallel", "arbitrary")),
        name="nsa_sample_cmp",
    )(page_table, *([cache] * pp), u_q)


def _nsa_sample_sel_kernel(pages_ref, lpage_ref, cnt_ref, cache_ref, selm_ref, q_ref, g_ref, kvn_ref, ocmp_ref,
                           win_ref, o_ref, buf, sem, *, layer, tq, past_len):
    b = pl.program_id(0)
    n = cnt_ref[b]
    n_cmp = past_len // BLK
    rows_all = NSA_HEADS * tq

    def page_copy(slot, idx):
        return pltpu.make_async_copy(
            cache_ref.at[layer, pages_ref[b, idx], pl.ds(2 * LANES, 2 * LANES), :], buf.at[slot], sem.at[slot])

    @pl.when(n > 0)
    def _():
        page_copy(0, 0).start()

    q2 = _build_q2(q_ref[...], tq)
    selm = selm_ref[0]
    blk_idx = lax.broadcasted_iota(I32, selm.shape, 1)
    key = lax.broadcasted_iota(I32, (tq, LANES), 1)
    tpos = past_len + lax.broadcasted_iota(I32, (tq, LANES), 0)
    key16 = lax.broadcasted_iota(I32, (NSA_KV * tq, LANES), 1)

    def body(it, carry):
        m, l, acc = carry
        slot = it & 1

        @pl.when(it + 1 < n)
        def _():
            page_copy(1 - slot, it + 1).start()

        page_copy(slot, it).wait()
        pg = buf[slot]
        lp = lpage_ref[b, it]
        pos = lp * PAGE + key
        dist = (tpos - pos).astype(F32)
        sel_lo = jnp.sum(jnp.where(blk_idx == 2 * lp, selm, 0.0), axis=-1, keepdims=True)
        sel_hi = jnp.sum(jnp.where(blk_idx == 2 * lp + 1, selm, 0.0), axis=-1, keepdims=True)
        mk16 = jnp.where(key16 < BLK, sel_lo, sel_hi) > 0.5
        masks = [mk16[kv * tq:(kv + 1) * tq] for kv in range(NSA_KV)]
        s3 = _dot(q2, pg[:LANES, :].astype(BF16)).reshape(NSA_HEADS, tq, LANES)
        vt_bf = pg[LANES:, :].astype(BF16)
        return _online_step(s3, lambda h: masks[h // GQA], dist, lambda p: _dot_nt(p, vt_bf), m, l, acc, tq)

    init = (jnp.full((rows_all, LANES), NEG_INF, F32), jnp.zeros((rows_all, LANES), F32),
            jnp.zeros((rows_all, LANES), F32))
    m, l, acc = lax.fori_loop(0, n, body, init)

    kvn = kvn_ref[...]
    zpad = jnp.zeros((LANES - tq, LANES), F32)
    k_new = jnp.concatenate([kvn[:, 2 * LANES:3 * LANES], zpad], axis=0).astype(BF16)
    v_new = jnp.concatenate([kvn[:, 3 * LANES:4 * LANES], zpad], axis=0).astype(BF16)
    pos = past_len + key
    mk_new = (key < tq) & (pos <= tpos)
    s3 = _dot_nt(q2, k_new).reshape(NSA_HEADS, tq, LANES)
    m, l, acc = _online_step(s3, lambda h: mk_new, (tpos - pos).astype(F32), lambda p: _dot(p, v_new),
                             m, l, acc, tq)
    o_sel = acc / jnp.maximum(l, TINY)

    win = win_ref[0]
    wl = WINDOW + LANES
    kw_new = jnp.concatenate([kvn[:, 4 * LANES:5 * LANES], zpad], axis=0).astype(BF16)
    vw_new = jnp.concatenate([kvn[:, 5 * LANES:6 * LANES], zpad], axis=0).astype(BF16)
    widx = lax.broadcasted_iota(I32, (tq, wl), 1)
    pw = past_len - WINDOW + widx
    dw = (past_len + lax.broadcasted_iota(I32, (tq, wl), 0)) - pw
    mw = (dw >= 0) & (dw < WINDOW) & (pw >= 0) & (widx < WINDOW + tq)
    s = jnp.concatenate([_dot(q2, win[:LANES, :].astype(BF16)), _dot_nt(q2, kw_new)], axis=1)
    pws = _masked_softmax_heads(s.reshape(NSA_HEADS, tq, wl), lambda h: mw, dw.astype(F32))
    p_win = jnp.concatenate(pws, axis=0).astype(BF16)
    o_win = _dot_nt(p_win[:, :WINDOW], win[LANES:, :].astype(BF16)) + _dot(p_win[:, WINDOW:], vw_new)

    o_ref[...] = _combine_heads(g_ref[...], ocmp_ref[0], o_sel, o_win, tq)


def _nsa_sample_sel(cache, layer, pages, lpages, cnt, selm, u_q, u_gate, u_kv, o_cmp, win_cache,
                    row0, nb, tq, past_len):
    wq = NSA_HEADS * HEAD_DIM
    base = row0 // tq
    n_cmp = past_len // BLK
    grid_spec = pltpu.PrefetchScalarGridSpec(
        num_scalar_prefetch=3,
        grid=(nb,),
        in_specs=[
            pl.BlockSpec(memory_space=pl.ANY),
            pl.BlockSpec((1, NSA_KV * tq, n_cmp), lambda bb, *_: (bb, 0, 0)),
            pl.BlockSpec((tq, wq), lambda bb, *_: (base + bb, 0)),
            pl.BlockSpec((tq, LANES), lambda bb, *_: (base + bb, 0)),
            pl.BlockSpec((tq, 6 * LANES), lambda bb, *_: (base + bb, 0)),
            pl.BlockSpec((1, NSA_HEADS * tq, LANES), lambda bb, *_: (bb, 0, 0)),
            pl.BlockSpec((None, 1, 2 * LANES, WINDOW), lambda bb, *_: (layer, bb, 0, 0)),
        ],
        out_specs=pl.BlockSpec((tq, wq), lambda bb, *_: (bb, 0)),
        scratch_shapes=[pltpu.VMEM((2, 2 * LANES, PAGE), F32), pltpu.SemaphoreType.DMA((2,))],
    )
    return pl.pallas_call(
        functools.partial(_nsa_sample_sel_kernel, layer=layer, tq=tq, past_len=past_len),
        grid_spec=grid_spec,
        out_shape=jax.ShapeDtypeStruct((nb * tq, wq), F32),
        compiler_params=_cp(("arbitrary",)),
        name="nsa_sample_sel",
    )(pages, lpages, cnt, cache, selm, u_q, u_gate, u_kv, o_cmp, win_cache)


def _outproj_kernel(x_ref, cp_ref, cs_ref, ap_ref, as_ref, rp_ref, rs_ref, wo_ref, g_ref, b_ref, wr_ref, br_ref,
                    h_ref, te_ref, tg_ref, cnt_ref, *, alpha, n_prompt_tiles):
    is_sample = pl.program_id(0) >= n_prompt_tiles
    pick = lambda p_ref, s_ref: jnp.where(is_sample, s_ref[...], p_ref[...]).astype(BF16)
    mix = (_dot(pick(cp_ref, cs_ref), wo_ref[0:CONV_CH, :])
           + _dot(pick(ap_ref, as_ref), wo_ref[CONV_CH:CONV_CH + NSA_HEADS * HEAD_DIM, :])
           + _dot(pick(rp_ref, rs_ref), wo_ref[CONV_CH + NSA_HEADS * HEAD_DIM:, :]))
    h = _layer_norm(alpha * x_ref[...] + mix, g_ref[...], b_ref[...])
    h_ref[...] = h
    h_hi = h.astype(BF16)
    h_lo = (h - h_hi.astype(F32)).astype(BF16)
    r = _dot(h_hi, wr_ref[...]) + _dot(h_lo, wr_ref[...])
    logits = r[:, 0:N_EXPERTS] + r[:, N_EXPERTS:2 * N_EXPERTS] + br_ref[...]
    idx = lax.broadcasted_iota(I32, logits.shape, 1).astype(F32)
    col = lax.broadcasted_iota(I32, (logits.shape[0], TOP_K), 1)
    vals = jnp.zeros((logits.shape[0], TOP_K), F32)
    ids = jnp.zeros((logits.shape[0], TOP_K), F32)
    cur = logits
    hist = jnp.zeros((1, N_EXPERTS), F32)
    for k in range(TOP_K):
        mx = jnp.max(cur, axis=-1, keepdims=True)
        am = jnp.min(jnp.where(cur == mx, idx, float(N_EXPERTS)), axis=-1, keepdims=True)
        vals = jnp.where(col == k, mx, vals)
        ids = jnp.where(col == k, am, ids)
        hit = idx == am
        hist = hist + jnp.sum(jnp.where(hit, 1.0, 0.0), axis=0, keepdims=True)
        cur = jnp.where(hit, -jnp.inf, cur)
    e = jnp.exp(vals - vals[:, 0:1])
    tg_ref[...] = e / jnp.sum(e, axis=-1, keepdims=True)
    te_ref[...] = ids.astype(I32)

    @pl.when(pl.program_id(0) == 0)
    def _():
        cnt_ref[...] = jnp.zeros(cnt_ref.shape, F32)

    cnt_ref[...] = cnt_ref[...] + hist


def _outproj(x_all, mixers, w_out_bf, g, b, w_router, b_router, alpha):
    n, d = x_all.shape
    tm = TOKEN_TILE
    npt = mixers[0][0].shape[0] // tm
    row = lambda c: pl.BlockSpec((tm, c), lambda i: (i, 0))
    row_p = lambda c: pl.BlockSpec((tm, c), lambda i: (jnp.minimum(i, npt - 1), 0))
    row_s = lambda c: pl.BlockSpec((tm, c), lambda i: (jnp.maximum(i - npt, 0), 0))
    full = lambda r, c: pl.BlockSpec((r, c), lambda i: (0, 0))
    mix_specs, mix_args = [], []
    for a_p, a_s in mixers:
        mix_specs += [row_p(a_p.shape[1]), row_s(a_s.shape[1])]
        mix_args += [a_p, a_s]
    return pl.pallas_call(
        functools.partial(_outproj_kernel, alpha=alpha, n_prompt_tiles=npt),
        grid=(n // tm,),
        in_specs=[row(d)] + mix_specs + [
                  full(d, d), full(1, d), full(1, d), full(d, LANES), full(1, N_EXPERTS)],
        out_specs=[row(d), row(TOP_K), row(TOP_K), full(1, N_EXPERTS)],
        out_shape=[jax.ShapeDtypeStruct((n, d), F32),
                   jax.ShapeDtypeStruct((n, TOP_K), I32), jax.ShapeDtypeStruct((n, TOP_K), F32),
                   jax.ShapeDtypeStruct((1, N_EXPERTS), F32)],
        compiler_params=_cp(("arbitrary",)),
        name="outproj",
    )(x_all, *mix_args, w_out_bf, g, b, w_router, b_router)


def _moe_kernel(blk_ref, exp_ref, lo_ref, hi_ref, first_ref, x_ref, wup_ref, bg_ref, bu_ref, wd_ref, bd_ref, o_ref,
                wg_scr, wu_scr, wd_scr):
    i = pl.program_id(0)
    dff = wg_scr.shape[1]

    @pl.when((i == 0) | (exp_ref[i] != exp_ref[jnp.maximum(i - 1, 0)]))
    def _():
        w2 = 2 * LANES
        r = lax.broadcasted_iota(I32, (w2, w2), 0)
        c = lax.broadcasted_iota(I32, (w2, w2), 1)
        perm = jnp.where(r == jnp.where(c < LANES, 2 * c, 2 * (c - LANES) + 1), 1.0, 0.0).astype(BF16)
        for k in range(2 * dff // w2):
            split = _dot(wup_ref[:, k * w2:(k + 1) * w2].astype(BF16), perm).astype(BF16)
            wg_scr[:, k * LANES:(k + 1) * LANES] = split[:, :LANES]
            wu_scr[:, k * LANES:(k + 1) * LANES] = split[:, LANES:]
        wd_scr[...] = wd_ref[...].astype(BF16)

    lo, hi = lo_ref[i], hi_ref[i]

    @pl.when(hi > lo)
    def _():
        x = x_ref[...].astype(BF16)
        g = jnp.minimum(_dot(x, wg_scr[...]) + bg_ref[...], SWIGLU_LIMIT)
        u = jnp.clip(_dot(x, wu_scr[...]) + bu_ref[...], -SWIGLU_LIMIT, SWIGLU_LIMIT)
        act = (u + 1.0) * g * _sigmoid(SWIGLU_ALPHA * g)
        y = _dot(act.astype(BF16), wd_scr[...]) + bd_ref[...]
        row = lax.broadcasted_iota(I32, y.shape, 0)
        mine = (row >= lo) & (row < hi)

        @pl.when(first_ref[i] == 1)
        def _():
            o_ref[...] = jnp.where(mine, y, 0.0)

        @pl.when(first_ref[i] == 0)
        def _():
            o_ref[...] = jnp.where(mine, y, o_ref[...])


def _moe_blocks(xs, items, w_up, bg, bu, w_down, bd, layer):
    r, d = xs.shape
    bm = MOE_TILE
    dff = w_down.shape[2]
    n_items = items[0].shape[0]
    wspec = lambda a, c: pl.BlockSpec((None, None, a, c), lambda i, blk, exp, *_: (layer, exp[i], 0, 0))
    bspec = lambda c: pl.BlockSpec((None, 1, c), lambda i, blk, exp, *_: (exp[i], 0, 0))
    grid_spec = pltpu.PrefetchScalarGridSpec(
        num_scalar_prefetch=5,
        grid=(n_items,),
        in_specs=[
            pl.BlockSpec((bm, d), lambda i, blk, *_: (blk[i], 0)),
            wspec(d, 2 * dff), bspec(dff), bspec(dff), wspec(dff, d), bspec(d),
        ],
        out_specs=pl.BlockSpec((bm, d), lambda i, blk, *_: (blk[i], 0)),
        scratch_shapes=[pltpu.VMEM((d, dff), BF16), pltpu.VMEM((d, dff), BF16), pltpu.VMEM((dff, d), BF16)],
    )
    return pl.pallas_call(
        _moe_kernel,
        grid_spec=grid_spec,
        out_shape=jax.ShapeDtypeStruct((r, d), F32),
        compiler_params=_cp(("arbitrary",)),
        name="moe_experts",
    )(*items, xs, w_up, bg, bu, w_down, bd)


def _moe(h, top_e, counts, w_up, bg, bu, w_down, bd, layer):
    n, d = h.shape
    bm = MOE_TILE
    nk = n * TOP_K
    assert nk % bm == 0
    n_blk = nk // bm
    experts = jnp.arange(N_EXPERTS, dtype=I32)
    flat_e = top_e.T.reshape(nk)
    order = jnp.argsort(flat_e).astype(I32)
    rank = jnp.argsort(order).astype(I32)
    ends = jnp.cumsum(counts)
    starts = ends - counts
    has = ends > starts
    first_blk = starts // bm
    per_e = jnp.where(has, (ends - 1) // bm - first_blk + 1, 0)
    it_end = jnp.cumsum(per_e)
    it_start = it_end - per_e
    n_items = n_blk + N_EXPERTS - 1
    t = jnp.arange(n_items, dtype=I32)
    live = t < it_end[-1]
    e_t = jnp.minimum(jnp.sum((it_end[None, :] <= t[:, None]).astype(I32), axis=1), N_EXPERTS - 1)
    blk_t = first_blk[e_t] + t - it_start[e_t]
    lo = jnp.clip(starts[e_t] - blk_t * bm, 0, bm)
    hi = jnp.clip(ends[e_t] - blk_t * bm, 0, bm)
    e_last = jnp.max(jnp.where(has, experts, 0))
    blk_t = jnp.where(live, blk_t, n_blk - 1)
    e_t = jnp.where(live, e_t, e_last)
    lo = jnp.where(live, lo, 0)
    hi = jnp.where(live, hi, 0)
    first = jnp.concatenate([jnp.ones((1,), bool), blk_t[1:] != blk_t[:-1]]) & live
    items = tuple(a.astype(I32) for a in (blk_t, e_t, lo, hi, first))
    xs = jnp.take(h, order % n, axis=0, mode='clip')
    yb = _moe_blocks(xs, items, w_up, bg, bu, w_down, bd, layer)
    return jnp.take(yb, rank, axis=0, mode='clip')


def _final_kernel(h_ref, *refs, alpha, n_prompt_tiles):
    f_refs = refs[:TOP_K]
    tg_ref, pp_ref, ps_ref, wple_ref, wplg_ref, g_ref, b_ref, y_ref = refs[TOP_K:]
    h = h_ref[...]
    tg = tg_ref[...]
    ffn = tg[:, 0:1] * f_refs[0][...]
    for k in range(1, TOP_K):
        ffn = ffn + tg[:, k:k + 1] * f_refs[k][...]
    p = jnp.where(pl.program_id(0) >= n_prompt_tiles, ps_ref[...], pp_ref[...]).astype(BF16)
    ple = _dot(p, wple_ref[...]) * _sigmoid(_dot(h.astype(BF16), wplg_ref[...]))
    y_ref[...] = _layer_norm(alpha * h + ffn + ple, g_ref[...], b_ref[...])


def _final(h, expert_out, gate, p_prompt, p_sample, layer, w_ple_bf, w_plg_bf, g, b, alpha):
    n, d = h.shape
    tm = TOKEN_TILE
    pd = p_prompt.shape[2]
    npt = p_prompt.shape[1] // tm
    row = lambda c: pl.BlockSpec((tm, c), lambda i: (i, 0))
    full = lambda r, c: pl.BlockSpec((r, c), lambda i: (0, 0))
    kth = lambda k: pl.BlockSpec((tm, d), lambda i: (k * (n // tm) + i, 0))
    p_specs = [pl.BlockSpec((None, tm, pd), lambda i: (layer, jnp.minimum(i, npt - 1), 0)),
               pl.BlockSpec((None, tm, pd), lambda i: (layer, jnp.maximum(i - npt, 0), 0))]
    return pl.pallas_call(
        functools.partial(_final_kernel, alpha=alpha, n_prompt_tiles=npt),
        grid=(n // tm,),
        in_specs=[row(d)] + [kth(k) for k in range(TOP_K)]
        + [row(TOP_K)] + p_specs + [full(pd, d), full(d, d), full(1, d), full(1, d)],
        out_specs=row(d),
        out_shape=jax.ShapeDtypeStruct((n, d), F32),
        compiler_params=_cp(("parallel",)),
        name="final",
    )(h, *([expert_out] * TOP_K), gate, p_prompt, p_sample, w_ple_bf, w_plg_bf, g, b)


def kernel(x_prompt, x_sample, p_prompt, p_sample, cache_nsa_kv, cache_win_kv, state_ret, state_conv, page_table,
           w_in, w_out, conv_w, conv_b, conv_ln_g, conv_ln_b, ret_norm_g, ln1_g, ln1_b, w_router, b_router,
           w_up, b_up, w_down, b_down, w_ple, w_plg, ln2_g, ln2_b):
    bp, seq, d = x_prompt.shape
    bs, tdec, _ = x_sample.shape
    depth = w_in.shape[0]
    n_pages = page_table.shape[1]
    past_len = n_pages * PAGE
    wbuf = cache_win_kv.shape[2]
    n_p, n_s = bp * seq, bs * tdec
    assert wbuf == WINDOW and tdec <= BLK and tdec % 8 == 0 and past_len // BLK >= N_SEL
    assert seq % NSA_TQ == 0 and seq >= WINDOW + NSA_TQ and n_p % TOKEN_TILE == 0 and n_s % TOKEN_TILE == 0
    alpha = (2 * depth) ** 0.25
    n_pool = cache_nsa_kv.shape[1]
    cache = jnp.transpose(cache_nsa_kv, (0, 1, 3, 4, 5, 2)).reshape(depth, n_pool, 4 * LANES, PAGE)
    win_t = jnp.transpose(cache_win_kv, (0, 1, 3, 4, 5, 2)).reshape(depth, bs, 2 * LANES, wbuf)
    page_table = page_table.astype(I32)

    c_conv = 2 * CONV_CH
    c_q = NSA_HEADS * HEAD_DIM
    c_kv = 6 * NSA_KV * HEAD_DIM
    c_gate = 3 * NSA_HEADS
    cuts = [0, c_conv, c_conv + c_q, c_conv + c_q + c_kv, c_conv + c_q + c_kv + c_gate, w_in.shape[2]]

    x_all = jnp.concatenate([x_prompt.reshape(n_p, d), x_sample.reshape(n_s, d)], axis=0)
    outs = {k: [] for k in ("kvp", "kvs", "wp", "ws", "rp", "rs", "cp", "cs")}
    zeros_ret = jnp.zeros((bp, RET_HEADS, HEAD_DIM, HEAD_DIM), F32)
    zeros_conv = jnp.zeros((bp, CONV_K - 1, CONV_CH), F32)

    for i in range(depth):
        wi = w_in[i].astype(BF16)
        ws = [wi[:, cuts[j]:cuts[j + 1]] for j in range(5)]
        ws[3] = jnp.pad(ws[3], ((0, 0), (0, LANES - c_gate)))
        u_conv, u_q, u_kv, u_gate, u_ret, kv_t = _inproj(x_all, ws, bp, seq)

        cw = jnp.pad(conv_w[i], ((0, 32 - CONV_K), (0, 0)))
        cargs = (cw, conv_b[i][None], conv_ln_g[i][None], conv_ln_b[i][None])
        conv_p, conv_sp = _conv(u_conv, zeros_conv, *cargs, 0, bp, seq)
        conv_s, conv_ss = _conv(u_conv, state_conv[i], *cargs, n_p, bs, tdec)

        rgain = ret_norm_g[i][None]
        ret_p, ret_sp = _retention(u_ret, zeros_ret, rgain, 0, bp, seq)
        ret_s, ret_ss = _retention(u_ret, state_ret[i], rgain, n_p, bs, tdec)

        nsa_p = _nsa_prompt(u_q, u_gate, u_kv, bp, seq)
        o_cmp, blk_score = _nsa_sample_cmp(cache, i, page_table, u_q, n_p, bs, tdec)
        selm = _topk_blocks(blk_score.reshape(bs * NSA_KV * tdec, -1), N_SEL - 1).reshape(blk_score.shape)
        need = selm.reshape(bs, NSA_KV * tdec, n_pages, PAGE // BLK).max(axis=(1, 3)) > 0.5
        npos = jnp.cumsum(need.astype(I32), axis=1) - 1
        cnt = npos[:, -1] + 1
        pidx = jnp.arange(n_pages, dtype=I32)
        hit = need[:, :, None] & (npos[:, :, None] == pidx[None, None, :])
        order = jnp.sum(jnp.where(hit, pidx[None, :, None], 0), axis=1).astype(I32)
        pages = jnp.take_along_axis(page_table, order, axis=1)
        nsa_s = _nsa_sample_sel(cache, i, pages, order, cnt, selm, u_q, u_gate, u_kv, o_cmp, win_t,
                                n_p, bs, tdec, past_len)

        wr_hi = w_router[i].astype(BF16)
        wr_lo = (w_router[i] - wr_hi.astype(F32)).astype(BF16)
        wr_cat = jnp.pad(jnp.concatenate([wr_hi, wr_lo], axis=1), ((0, 0), (0, LANES - 2 * N_EXPERTS)))
        h, top_e, gate, counts = _outproj(x_all, ((conv_p, conv_s), (nsa_p, nsa_s), (ret_p, ret_s)),
                                          w_out[i].astype(BF16),
                                          ln1_g[i][None], ln1_b[i][None], wr_cat, b_router[i][None], alpha)

        expert_out = _moe(h, top_e, counts[0].astype(I32), w_up, b_up[i][:, None, 0::2], b_up[i][:, None, 1::2],
                          w_down, b_down[i][:, None, :], i)

        x_all = _final(h, expert_out, gate, p_prompt.reshape(depth, n_p, -1), p_sample.reshape(depth, n_s, -1), i,
                       w_ple[i].astype(BF16), w_plg[i].astype(BF16),
                       ln2_g[i][None], ln2_b[i][None], alpha)

        kv_shape = (4, NSA_KV, HEAD_DIM)
        outs["kvp"].append(jnp.transpose(kv_t[:, :4 * LANES].reshape(bp, *kv_shape, seq), (0, 4, 1, 2, 3)))
        outs["kvs"].append(u_kv[n_p:, :4 * LANES].reshape(bs, tdec, *kv_shape))
        n_keep = min(WINDOW, seq)
        win_t_p = kv_t[:, 4 * LANES:, seq - n_keep:].reshape(bp, 2, NSA_KV, HEAD_DIM, n_keep)
        outs["wp"].append(jnp.transpose(win_t_p, (0, 4, 1, 2, 3)))
        win_new_s = u_kv[n_p:, 4 * LANES:].reshape(bs, tdec, 2, NSA_KV, HEAD_DIM)
        outs["ws"].append(jnp.concatenate([cache_win_kv[i][:, tdec:], win_new_s], axis=1))
        outs["rp"].append(ret_sp)
        outs["rs"].append(ret_ss)
        outs["cp"].append(conv_sp)
        outs["cs"].append(conv_ss)

    y_p = x_all[:n_p].reshape(bp, seq, d)
    y_s = x_all[n_p:].reshape(bs, tdec, d)
    st = lambda k: jnp.stack(outs[k])
    return (y_p, y_s, st("kvp"), st("kvs"), st("wp"), st("ws"), st("rp"), st("rs"), st("cp"), st("cs"))
```

```python
import functools

import jax
import jax.numpy as jnp
from jax import lax
from jax.experimental import pallas as pl
from jax.experimental.pallas import tpu as pltpu

F32 = jnp.float32
BF16 = jnp.bfloat16
I32 = jnp.int32

HEAD_DIM = 64
CONV_CH = 256
CONV_K = 31
NSA_HEADS = 8
NSA_KV = 2
GQA = 4
BLK = 64
N_SEL = 16
WINDOW = 512
FORCE_SCORE = 1e4
RET_HEADS = 4
RET_CHUNK = 128
N_EXPERTS = 32
TOP_K = 4
SWIGLU_LIMIT = 7.0
SWIGLU_ALPHA = 1.702
PAGE = 128
LN_EPS = 1e-5
NEG_INF = -1e30
TINY = 1e-30
LANES = 128

TOKEN_TILE = 256
MOE_TILE = 256
NSA_TQ = 128
NSA_KEY_TILES = 4
SAMPLE_PAGES_PER_STEP = 16
VMEM_LIMIT = 56 * 1024 * 1024


def _cp(sem):
    return pltpu.CompilerParams(dimension_semantics=sem, vmem_limit_bytes=VMEM_LIMIT)


def _dot(a, b):
    return jnp.dot(a, b, preferred_element_type=F32)


def _dot_nt(a, b):
    return lax.dot_general(a, b, (((1,), (1,)), ((), ())), preferred_element_type=F32)


def _dot_tn(a, b):
    return lax.dot_general(a, b, (((0,), (0,)), ((), ())), preferred_element_type=F32)


def _layer_norm(x, g, b):
    mu = jnp.mean(x, axis=-1, keepdims=True)
    xc = x - mu
    var = jnp.mean(xc * xc, axis=-1, keepdims=True)
    return xc * lax.rsqrt(var + LN_EPS) * g + b


def _sigmoid(x):
    return 1.0 / (1.0 + jnp.exp(-x))


def _inproj_kernel(xp_ref, xs_ref, wc_ref, wq_ref, wkv_ref, wg_ref, wr_ref, oc_ref, oq_ref, okv_ref, og_ref, or_ref,
                   okvt_ref, *, n_prompt_tiles):
    x = jnp.where(pl.program_id(0) >= n_prompt_tiles, xs_ref[...], xp_ref[...]).astype(BF16)
    oc_ref[...] = _dot(x, wc_ref[...])
    oq_ref[...] = _dot(x, wq_ref[...])
    ukv = _dot(x, wkv_ref[...])
    okv_ref[...] = ukv
    og_ref[...] = _dot(x, wg_ref[...])
    or_ref[...] = _dot(x, wr_ref[...])

    @pl.when(pl.program_id(0) < n_prompt_tiles)
    def _():
        okvt_ref[...] = ukv.T


def _inproj(x_p, x_s, ws, bp, seq):
    d = x_p.shape[1]
    n = x_p.shape[0] + x_s.shape[0]
    tm = TOKEN_TILE
    widths = [w.shape[1] for w in ws]
    c_kv = widths[2]
    tiles_per_seq = seq // tm
    npt = bp * tiles_per_seq
    kvt_spec = pl.BlockSpec((None, c_kv, tm), lambda i: (jnp.minimum(i, npt - 1) // tiles_per_seq, 0,
                                                          jnp.minimum(i, npt - 1) % tiles_per_seq))
    return pl.pallas_call(
        functools.partial(_inproj_kernel, n_prompt_tiles=npt),
        grid=(n // tm,),
        in_specs=[pl.BlockSpec((tm, d), lambda i: (jnp.minimum(i, npt - 1), 0)),
                  pl.BlockSpec((tm, d), lambda i: (jnp.maximum(i - npt, 0), 0))]
        + [pl.BlockSpec((d, c), lambda i: (0, 0)) for c in widths],
        out_specs=[pl.BlockSpec((tm, c), lambda i: (i, 0)) for c in widths] + [kvt_spec],
        out_shape=[jax.ShapeDtypeStruct((n, c), F32) for c in widths]
        + [jax.ShapeDtypeStruct((bp, c_kv, seq), F32)],
        compiler_params=_cp(("arbitrary",)),
        name="inproj",
    )(x_p, x_s, *ws)


def _conv_kernel(u_ref, s0_ref, w_ref, cb_ref, g_ref, b_ref, o_ref, st_ref, ext, *, tt):
    i = pl.program_id(1)
    pad = 32 - (CONV_K - 1)

    @pl.when(i == 0)
    def _():
        ext[0:pad, :] = jnp.zeros((pad, CONV_CH), F32)
        ext[pad:32, :] = s0_ref[0]

    u = u_ref[...]
    h = u[:, :CONV_CH] * _sigmoid(u[:, CONV_CH:])
    ext[32:32 + tt, :] = h
    acc = jnp.zeros((tt, CONV_CH), F32)
    for j in range(CONV_K):
        acc = acc + ext[pad + j:pad + j + tt, :] * w_ref[j:j + 1, :]
    y = _layer_norm(acc + cb_ref[...], g_ref[...], b_ref[...])
    o_ref[...] = y * _sigmoid(y)
    tail = ext[tt + pad:tt + 32, :]

    @pl.when(i == pl.num_programs(1) - 1)
    def _():
        st_ref[0] = tail

    ext[pad:32, :] = tail


def _conv(u_conv, s0, w, cb, g, b, row0, nb, t):
    tt = min(t, 256)
    nt = t // tt
    base = row0 // tt
    in_specs = [
        pl.BlockSpec((tt, 2 * CONV_CH), lambda bb, i: (base + bb * nt + i, 0)),
        pl.BlockSpec((1, CONV_K - 1, CONV_CH), lambda bb, i: (bb, 0, 0)),
        pl.BlockSpec((32, CONV_CH), lambda bb, i: (0, 0)),
        pl.BlockSpec((1, CONV_CH), lambda bb, i: (0, 0)),
        pl.BlockSpec((1, CONV_CH), lambda bb, i: (0, 0)),
        pl.BlockSpec((1, CONV_CH), lambda bb, i: (0, 0)),
    ]
    args = [u_conv, s0, w, cb, g, b]
    return pl.pallas_call(
        functools.partial(_conv_kernel, tt=tt),
        grid=(nb, nt),
        in_specs=in_specs,
        out_specs=[
            pl.BlockSpec((tt, CONV_CH), lambda bb, i: (bb * nt + i, 0)),
            pl.BlockSpec((1, CONV_K - 1, CONV_CH), lambda bb, i: (bb, 0, 0)),
        ],
        out_shape=[
            jax.ShapeDtypeStruct((nb * t, CONV_CH), F32),
            jax.ShapeDtypeStruct((nb, CONV_K - 1, CONV_CH), F32),
        ],
        scratch_shapes=[pltpu.VMEM((32 + tt, CONV_CH), F32)],
        compiler_params=_cp(("parallel", "arbitrary")),
        name="conv",
    )(*args)


def _ret_kernel(u_ref, s0_ref, dm_ref, qd_ref, kd_ref, cd_ref, g_ref, o_ref, st_ref, s_scr):
    i = pl.program_id(1)

    @pl.when(i == 0)
    def _():
        s_scr[...] = s0_ref[0]

    u = u_ref[...]
    w = RET_HEADS * HEAD_DIM
    outs = []
    for h in range(RET_HEADS):
        lo = h * HEAD_DIM
        q = u[:, lo:lo + HEAD_DIM]
        k = u[:, w + lo:w + lo + HEAD_DIM] * (HEAD_DIM ** -0.5)
        v = u[:, 2 * w + lo:2 * w + lo + HEAD_DIM]
        rg = u[:, 3 * w + lo:3 * w + lo + HEAD_DIM]
        qb, kb, vb = q.astype(BF16), k.astype(BF16), v.astype(BF16)
        att = _dot_nt(qb, kb) * dm_ref[h]
        s_h = s_scr[h]
        o = _dot(att.astype(BF16), vb) + _dot(qb, s_h.astype(BF16)) * qd_ref[h]
        kdec = (k * kd_ref[h]).astype(BF16)
        s_scr[h] = s_h * cd_ref[h] + _dot_tn(kdec, vb)
        mu = jnp.mean(o, axis=-1, keepdims=True)
        oc = o - mu
        var = jnp.mean(oc * oc, axis=-1, keepdims=True)
        on = oc * lax.rsqrt(var + LN_EPS) * g_ref[:, lo:lo + HEAD_DIM]
        outs.append(rg * _sigmoid(rg) * on)
    o_ref[...] = jnp.concatenate(outs, axis=1)

    @pl.when(i == pl.num_programs(1) - 1)
    def _():
        st_ref[0] = s_scr[...]


def _ret_tables(c):
    log_g = jnp.log1p(-jnp.exp2(-5.0 - jnp.arange(RET_HEADS, dtype=F32)))
    i = jnp.arange(c, dtype=F32)
    diff = i[:, None] - i[None, :]
    dmask = jnp.exp(jnp.where(diff >= 0, log_g[:, None, None] * diff, -jnp.inf))
    q_dec = jnp.exp(log_g[:, None] * (i[None, :] + 1.0))
    k_dec = jnp.exp(log_g[:, None] * (c - 1.0 - i[None, :]))
    c_dec = jnp.exp(log_g * c)
    bc = lambda a: jnp.broadcast_to(a[:, :, None], a.shape + (HEAD_DIM,))
    cd = jnp.broadcast_to(c_dec[:, None, None], (RET_HEADS, HEAD_DIM, HEAD_DIM))
    return dmask, bc(q_dec), bc(k_dec), cd


def _retention(u_ret, s0, gain, row0, nb, t):
    c = min(t, RET_CHUNK)
    while t % c:
        c -= 1
    nt = t // c
    base = row0 // c
    dm, qd, kd, cd = _ret_tables(c)
    w = RET_HEADS * HEAD_DIM
    full = lambda shape: pl.BlockSpec(shape, lambda bb, i: (0,) * len(shape))
    in_specs = [
        pl.BlockSpec((c, 4 * w), lambda bb, i: (base + bb * nt + i, 0)),
        pl.BlockSpec((1, RET_HEADS, HEAD_DIM, HEAD_DIM), lambda bb, i: (bb, 0, 0, 0)),
        full((RET_HEADS, c, c)),
        full((RET_HEADS, c, HEAD_DIM)),
        full((RET_HEADS, c, HEAD_DIM)),
        full((RET_HEADS, HEAD_DIM, HEAD_DIM)),
        full((1, w)),
    ]
    args = [u_ret, s0, dm, qd, kd, cd, gain]
    return pl.pallas_call(
        _ret_kernel,
        grid=(nb, nt),
        in_specs=in_specs,
        out_specs=[
            pl.BlockSpec((c, w), lambda bb, i: (bb * nt + i, 0)),
            pl.BlockSpec((1, RET_HEADS, HEAD_DIM, HEAD_DIM), lambda bb, i: (bb, 0, 0, 0)),
        ],
        out_shape=[
            jax.ShapeDtypeStruct((nb * t, w), F32),
            jax.ShapeDtypeStruct((nb, RET_HEADS, HEAD_DIM, HEAD_DIM), F32),
        ],
        scratch_shapes=[pltpu.VMEM((RET_HEADS, HEAD_DIM, HEAD_DIM), F32)],
        compiler_params=_cp(("parallel", "arbitrary")),
        name="retention",
    )(*args)


def _slope(h):
    return 2.0 ** -(h + 1)


def _build_q2(uq, tq):
    lane = lax.broadcasted_iota(I32, (tq, LANES), 1)
    parts = []
    for h in range(NSA_HEADS):
        kvh = h // GQA
        p = uq[:, (h // 2) * LANES:(h // 2 + 1) * LANES]
        if (h % 2) != kvh:
            p = pltpu.roll(p, HEAD_DIM, 1)
        keep = (lane >= HEAD_DIM) if kvh == 1 else (lane < HEAD_DIM)
        parts.append(jnp.where(keep, p * (HEAD_DIM ** -0.5), 0.0))
    return jnp.concatenate(parts, axis=0).astype(BF16)


def _masked_softmax_heads(s3, mask_of_head, dist):
    ps = []
    for h in range(NSA_HEADS):
        mk = mask_of_head(h)
        sh = jnp.where(mk, s3[h] - _slope(h) * dist, NEG_INF)
        mx = jnp.max(sh, axis=-1, keepdims=True)
        e = jnp.where(mk, jnp.exp(sh - mx), 0.0)
        den = jnp.maximum(jnp.sum(e, axis=-1, keepdims=True), TINY)
        ps.append(e / den)
    return ps


def _online_step(s3, mask_of_head, dist, pv, m, l, acc, tq):
    ps, alphas, ms, ls = [], [], [], []
    for h in range(NSA_HEADS):
        mk = mask_of_head(h)
        rows = slice(h * tq, (h + 1) * tq)
        sh = jnp.where(mk, s3[h] - _slope(h) * dist, NEG_INF)
        m_old = m[rows]
        m_new = jnp.maximum(m_old, jnp.max(sh, axis=-1, keepdims=True))
        p = jnp.where(mk, jnp.exp(sh - m_new), 0.0)
        alpha = jnp.exp(m_old - m_new)
        ls.append(alpha * l[rows] + jnp.sum(p, axis=-1, keepdims=True))
        ms.append(m_new)
        alphas.append(alpha)
        ps.append(p)
    p_all = jnp.concatenate(ps, axis=0).astype(BF16)
    alpha_all = jnp.concatenate(alphas, axis=0)
    acc = acc * alpha_all + pv(p_all)
    return jnp.concatenate(ms, axis=0), jnp.concatenate(ls, axis=0), acc


def _alibi_query_lanes(tq):
    lane = lax.broadcasted_iota(I32, (tq, LANES), 1)
    parts = [jnp.where(lane == 0, _slope(h), jnp.where(lane == 1, _slope(h) * LANES, 0.0))
             for h in range(NSA_HEADS)]
    return jnp.concatenate(parts, axis=0).astype(BF16)


def _alibi_key_lanes(pos):
    lane = lax.broadcasted_iota(I32, pos.shape, 1)
    return jnp.where(lane == 0, pos & (LANES - 1), jnp.where(lane == 1, pos >> 7, 0)).astype(F32).astype(BF16)


def _combine_heads(gl, o_cmp, o_sel, o_win, tq):
    gs = _sigmoid(gl)
    lane = lax.broadcasted_iota(I32, (tq, LANES), 1)
    pairs = []
    for mpair in range(NSA_HEADS // 2):
        halves = []
        for h in (2 * mpair, 2 * mpair + 1):
            rows = slice(h * tq, (h + 1) * tq)
            o = (gs[:, 3 * h:3 * h + 1] * o_cmp[rows] + gs[:, 3 * h + 1:3 * h + 2] * o_sel[rows]
                 + gs[:, 3 * h + 2:3 * h + 3] * o_win[rows])
            if (h // GQA) != (h % 2):
                o = pltpu.roll(o, HEAD_DIM, 1)
            halves.append(o)
        pairs.append(jnp.where(lane < HEAD_DIM, halves[0], halves[1]))
    return jnp.concatenate(pairs, axis=1)


def _cmp_probs(s, t0, tq, n_lanes):
    s3 = s.reshape(NSA_HEADS, tq, n_lanes)
    tpos = t0 + lax.broadcasted_iota(I32, (tq, n_lanes), 0)
    cend = lax.broadcasted_iota(I32, (tq, n_lanes), 1) * BLK + (BLK - 1)
    valid = cend <= tpos
    dist = (tpos - cend).astype(F32)
    return _masked_softmax_heads(s3, lambda h: valid, dist)


def _block_scores(ps, kv, t0, tq, n_lanes):
    tpos = t0 + lax.broadcasted_iota(I32, (tq, n_lanes), 0)
    nb = lax.broadcasted_iota(I32, (tq, n_lanes), 1)
    cur = tpos >> 6
    imp = ps[kv * GQA] + ps[kv * GQA + 1] + ps[kv * GQA + 2] + ps[kv * GQA + 3]
    forced = (nb == 0) | (nb == cur) | (nb == cur - 1)
    allowed = nb <= cur
    return jnp.where(allowed, jnp.where(forced, FORCE_SCORE, imp), -1.0), allowed, nb


def _nsa_prompt_kernel(q_ref, g_ref, kv_ref, o_ref, kb_ref, kc_ref, e_ref, sel_ref, q2e_ref, m_ref, l_ref, acc_ref,
                       *, t, tq):
    i = pl.program_id(1)
    n_blk = t // BLK
    rows_all = NSA_HEADS * tq

    @pl.when(i == 0)
    def _():
        kc_ref[...] = jnp.zeros(kc_ref.shape, F32)
        step = 256
        for c in range(t // step):
            sl = slice(c * step, (c + 1) * step)
            pos_lanes = _alibi_key_lanes(c * step + lax.broadcasted_iota(I32, (step, LANES), 0))
            kb_ref[sl, 0:LANES] = kv_ref[sl, 2 * LANES:3 * LANES].astype(BF16)
            kb_ref[sl, LANES:2 * LANES] = pos_lanes
            kb_ref[sl, 2 * LANES:3 * LANES] = kv_ref[sl, 4 * LANES:5 * LANES].astype(BF16)
            kb_ref[sl, 3 * LANES:4 * LANES] = pos_lanes
            kb_ref[sl, 4 * LANES:5 * LANES] = kv_ref[sl, 3 * LANES:4 * LANES].astype(BF16)
            kb_ref[sl, 5 * LANES:6 * LANES] = kv_ref[sl, 5 * LANES:6 * LANES].astype(BF16)
            kc_ref[c * (step // BLK):(c + 1) * (step // BLK), :] = (
                kv_ref[sl, 0:2 * LANES].reshape(step // BLK, BLK, 2 * LANES).sum(axis=1) * (1.0 / BLK))
        blk_of_key = lax.broadcasted_iota(I32, (LANES, t), 1) >> 6
        e_ref[...] = jnp.where(blk_of_key == lax.broadcasted_iota(I32, (LANES, t), 0), 1.0, 0.0).astype(BF16)

    t0 = i * tq
    q2 = _build_q2(q_ref[...], tq)

    kc = kc_ref[0:n_blk, :]
    s_t = _dot_nt(kc[:, :LANES].astype(BF16), q2)
    blk = lax.broadcasted_iota(I32, (n_blk, tq), 0)
    tpos_c = t0 + lax.broadcasted_iota(I32, (n_blk, tq), 1)
    cend = blk * BLK + (BLK - 1)
    valid = cend <= tpos_c
    dist_c = (tpos_c - cend).astype(F32)
    p_t = []
    for h in range(NSA_HEADS):
        sh = jnp.where(valid, s_t[:, h * tq:(h + 1) * tq] - _slope(h) * dist_c, NEG_INF)
        e = jnp.where(valid, jnp.exp(sh - jnp.max(sh, axis=0, keepdims=True)), 0.0)
        p_t.append(e / jnp.maximum(jnp.sum(e, axis=0, keepdims=True), TINY))
    o_cmp = _dot_tn(jnp.concatenate(p_t, axis=1).astype(BF16), kc[:, LANES:].astype(BF16))

    cur = tpos_c >> 6
    allowed = blk <= cur
    forced = (blk == 0) | (blk == cur) | (blk == cur - 1)
    eye = jnp.where(lax.broadcasted_iota(I32, (n_blk, LANES), 0) == lax.broadcasted_iota(I32, (n_blk, LANES), 1),
                    1.0, 0.0).astype(BF16)
    for kv in range(NSA_KV):
        imp = p_t[kv * GQA] + p_t[kv * GQA + 1] + p_t[kv * GQA + 2] + p_t[kv * GQA + 3]
        score = jnp.where(allowed, jnp.where(forced, FORCE_SCORE, imp), -1.0)
        rank = jnp.zeros((n_blk, tq), F32)
        for mblk in range(n_blk):
            c = score[mblk:mblk + 1, :]
            tie = jnp.where(blk > mblk, 1.0, 0.0)
            rank = rank + jnp.where(c > score, 1.0, jnp.where(c == score, tie, 0.0))
        sel_t = jnp.where(allowed, jnp.where(rank < N_SEL, 1.0, 0.0), 0.0).astype(BF16)
        sel_ref[kv] = _dot_tn(sel_t, eye).astype(BF16)

    q2e_ref[...] = jnp.concatenate([q2, _alibi_query_lanes(tq)], axis=1)

    m_ref[...] = jnp.full((rows_all, LANES), NEG_INF, F32)
    l_ref[...] = jnp.zeros((rows_all, LANES), F32)
    acc_ref[...] = jnp.zeros((rows_all, LANES), F32)
    tk = NSA_KEY_TILES * tq
    tpos = t0 + lax.broadcasted_iota(I32, (tq, tk), 0)

    def body(kt, carry):
        k0 = pl.multiple_of(kt * tk, tk)
        k_bf = kb_ref[pl.ds(k0, tk), 0:2 * LANES]
        v_bf = kb_ref[pl.ds(k0, tk), 4 * LANES:5 * LANES]
        causal = (k0 + lax.broadcasted_iota(I32, (tq, tk), 1)) <= tpos
        e_tile = e_ref[:, pl.ds(k0, tk)]
        biases = [jnp.where(causal, jnp.where(_dot(sel_ref[kv], e_tile) > 0.5, 0.0, NEG_INF), NEG_INF)
                  for kv in range(NSA_KV)]
        s3 = _dot_nt(q2e_ref[...], k_bf).reshape(NSA_HEADS, tq, tk)
        m_all, l_all = m_ref[...], l_ref[...]
        ps, alphas, ms, ls = [], [], [], []
        for h in range(NSA_HEADS):
            rows = slice(h * tq, (h + 1) * tq)
            sh = s3[h] + biases[h // GQA]
            m_old = m_all[rows]
            m_new = jnp.maximum(m_old, jnp.max(sh, axis=-1, keepdims=True))
            p = jnp.exp(sh - jnp.concatenate([m_new] * NSA_KEY_TILES, axis=1))
            alpha = jnp.exp(m_old - m_new)
            p_lanes = p[:, :LANES]
            for c in range(1, NSA_KEY_TILES):
                p_lanes = p_lanes + p[:, c * LANES:(c + 1) * LANES]
            ls.append(alpha * l_all[rows] + p_lanes)
            ms.append(m_new)
            alphas.append(alpha)
            ps.append(p.astype(BF16))
        acc_ref[...] = acc_ref[...] * jnp.concatenate(alphas, axis=0) + _dot(jnp.concatenate(ps, axis=0), v_bf)
        m_ref[...] = jnp.concatenate(ms, axis=0)
        l_ref[...] = jnp.concatenate(ls, axis=0)
        return carry

    lax.fori_loop(0, (i + NSA_KEY_TILES) // NSA_KEY_TILES, body, 0)
    o_sel = acc_ref[...] / jnp.sum(l_ref[...], axis=-1, keepdims=True)

    wl = WINDOW + tq
    ws = pl.multiple_of(jnp.maximum(t0 - WINDOW, 0), tq)
    kw = kb_ref[pl.ds(ws, wl), 2 * LANES:4 * LANES]
    vw = kb_ref[pl.ds(ws, wl), 5 * LANES:6 * LANES]
    dw = (t0 + lax.broadcasted_iota(I32, (tq, wl), 0)) - (ws + lax.broadcasted_iota(I32, (tq, wl), 1))
    bias_w = jnp.where(dw >= 0, jnp.where(dw < WINDOW, 0.0, NEG_INF), NEG_INF)
    s3 = _dot_nt(q2e_ref[...], kw).reshape(NSA_HEADS, tq, wl)
    es, sums = [], []
    for h in range(NSA_HEADS):
        sh = s3[h] + bias_w
        e = jnp.exp(sh - jnp.max(sh, axis=-1, keepdims=True))
        es.append(e.astype(BF16))
        sums.append(jnp.broadcast_to(jnp.sum(e, axis=-1, keepdims=True), (tq, LANES)))
    o_win = _dot(jnp.concatenate(es, axis=0), vw) / jnp.concatenate(sums, axis=0)

    o_ref[...] = _combine_heads(g_ref[...], o_cmp, o_sel, o_win, tq)


def _nsa_prompt(u_q, u_gate, u_kv, nb, t):
    tq = NSA_TQ
    nt = t // tq
    wq = NSA_HEADS * HEAD_DIM
    in_specs = [
        pl.BlockSpec((tq, wq), lambda bb, i: (bb * nt + i, 0)),
        pl.BlockSpec((tq, LANES), lambda bb, i: (bb * nt + i, 0)),
        pl.BlockSpec((t, 6 * LANES), lambda bb, i: (bb, 0)),
    ]
    args = [u_q, u_gate, u_kv]
    rows_all = NSA_HEADS * tq
    return pl.pallas_call(
        functools.partial(_nsa_prompt_kernel, t=t, tq=tq),
        grid=(nb, nt),
        in_specs=in_specs,
        out_specs=pl.BlockSpec((tq, wq), lambda bb, i: (bb * nt + i, 0)),
        out_shape=jax.ShapeDtypeStruct((nb * t, wq), F32),
        scratch_shapes=[
            pltpu.VMEM((t, 6 * LANES), BF16),
            pltpu.VMEM((LANES, 2 * LANES), F32),
            pltpu.VMEM((LANES, t), BF16),
            pltpu.VMEM((NSA_KV, tq, LANES), BF16),
            pltpu.VMEM((rows_all, 2 * LANES), BF16),
            pltpu.VMEM((rows_all, LANES), F32),
            pltpu.VMEM((rows_all, LANES), F32),
            pltpu.VMEM((rows_all, LANES), F32),
        ],
        compiler_params=_cp(("parallel", "arbitrary")),
        name="nsa_prompt",
    )(*args)


def _nsa_sample_cmp_kernel(pt_ref, *refs, pp, tq, past_len):
    page_refs = refs[:pp]
    q_ref, ocmp_ref, score_ref, cmp_scr = refs[pp:]
    j = pl.program_id(1)
    n_cmp = past_len // BLK
    per_step = pp * (PAGE // BLK)
    rows = []
    for k in range(pp):
        pg_t = page_refs[k][...].T
        rows.append(pg_t.reshape(PAGE // BLK, BLK, 2 * LANES).sum(axis=1) * (1.0 / BLK))
    cmp_scr[pl.ds(pl.multiple_of(j * per_step, per_step), per_step), :] = jnp.concatenate(rows, axis=0)

    @pl.when(j == pl.num_programs(1) - 1)
    def _():
        q2 = _build_q2(q_ref[...], tq)
        kc = cmp_scr[...]
        ps = _cmp_probs(_dot_nt(q2, kc[:, :LANES].astype(BF16)), past_len, tq, n_cmp)
        ocmp_ref[0] = _dot(jnp.concatenate(ps, axis=0).astype(BF16), kc[:, LANES:].astype(BF16))
        scores = [_block_scores(ps, kv, past_len, tq, n_cmp)[0] for kv in range(NSA_KV)]
        score_ref[0] = jnp.concatenate(scores, axis=0)


def _topk_blocks_kernel(sc_ref, sel_ref, *, k):
    sc = sc_ref[...]
    n = sc.shape[1]
    idx = lax.broadcasted_iota(I32, sc.shape, 1).astype(F32)
    sel = jnp.zeros(sc.shape, F32)
    for _ in range(k):
        mx = jnp.max(sc, axis=-1, keepdims=True)
        am = jnp.min(jnp.where(sc == mx, idx, float(n)), axis=-1, keepdims=True)
        hit = idx == am
        sel = jnp.where(hit, 1.0, sel)
        sc = jnp.where(hit, -2.0, sc)
    sel_ref[...] = sel


def _topk_blocks(scores, k):
    return pl.pallas_call(
        functools.partial(_topk_blocks_kernel, k=k),
        out_shape=jax.ShapeDtypeStruct(scores.shape, F32),
        compiler_params=pltpu.CompilerParams(vmem_limit_bytes=VMEM_LIMIT),
        name="topk_blocks",
    )(scores)


def _nsa_sample_cmp(cache, layer, page_table, u_q, row0, nb, tq):
    n_pages = page_table.shape[1]
    past_len = n_pages * PAGE
    n_cmp = past_len // BLK
    pp = min(SAMPLE_PAGES_PER_STEP, n_pages)
    steps = n_pages // pp
    wq = NSA_HEADS * HEAD_DIM
    base = row0 // tq

    def page_spec(k):
        return pl.BlockSpec((None, None, 2 * LANES, PAGE),
                            lambda bb, j, pt: (layer, pt[bb, j * pp + k], 0, 0))

    grid_spec = pltpu.PrefetchScalarGridSpec(
        num_scalar_prefetch=1,
        grid=(nb, steps),
        in_specs=[page_spec(k) for k in range(pp)]
        + [pl.BlockSpec((tq, wq), lambda bb, j, pt: (base + bb, 0))],
        out_specs=[
            pl.BlockSpec((1, NSA_HEADS * tq, LANES), lambda bb, j, pt: (bb, 0, 0)),
            pl.BlockSpec((1, NSA_KV * tq, n_cmp), lambda bb, j, pt: (bb, 0, 0)),
        ],
        scratch_shapes=[pltpu.VMEM((n_cmp, 2 * LANES), F32)],
    )
    return pl.pallas_call(
        functools.partial(_nsa_sample_cmp_kernel, pp=pp, tq=tq, past_len=past_len),
        grid_spec=grid_spec,
        out_shape=[
            jax.ShapeDtypeStruct((nb, NSA_HEADS * tq, LANES), F32),
            jax.ShapeDtypeStruct((nb, NSA_KV * tq, n_cmp), F32),
        ],
        compiler_params=_cp(("parallel", "arbitrary")),
        name="nsa_sample_cmp",
    )(page_table, *([cache] * pp), u_q)


def _nsa_sample_sel_kernel(pages_ref, lpage_ref, cnt_ref, cache_ref, selm_ref, q_ref, g_ref, kvn_ref, ocmp_ref,
                           win_ref, o_ref, buf, sem, *, layer, tq, past_len):
    b = pl.program_id(0)
    n = cnt_ref[b]
    n_cmp = past_len // BLK
    rows_all = NSA_HEADS * tq

    def page_copy(slot, idx):
        return pltpu.make_async_copy(
            cache_ref.at[layer, pages_ref[b, idx], pl.ds(2 * LANES, 2 * LANES), :], buf.at[slot], sem.at[slot])

    @pl.when(n > 0)
    def _():
        page_copy(0, 0).start()

    q2 = _build_q2(q_ref[...], tq)
    selm = selm_ref[0]
    blk_idx = lax.broadcasted_iota(I32, selm.shape, 1)
    key = lax.broadcasted_iota(I32, (tq, LANES), 1)
    tpos = past_len + lax.broadcasted_iota(I32, (tq, LANES), 0)
    key16 = lax.broadcasted_iota(I32, (NSA_KV * tq, LANES), 1)

    def body(it, carry):
        m, l, acc = carry
        slot = it & 1

        @pl.when(it + 1 < n)
        def _():
            page_copy(1 - slot, it + 1).start()

        page_copy(slot, it).wait()
        pg = buf[slot]
        lp = lpage_ref[b, it]
        pos = lp * PAGE + key
        dist = (tpos - pos).astype(F32)
        sel_lo = jnp.sum(jnp.where(blk_idx == 2 * lp, selm, 0.0), axis=-1, keepdims=True)
        sel_hi = jnp.sum(jnp.where(blk_idx == 2 * lp + 1, selm, 0.0), axis=-1, keepdims=True)
        mk16 = jnp.where(key16 < BLK, sel_lo, sel_hi) > 0.5
        masks = [mk16[kv * tq:(kv + 1) * tq] for kv in range(NSA_KV)]
        s3 = _dot(q2, pg[:LANES, :].astype(BF16)).reshape(NSA_HEADS, tq, LANES)
        vt_bf = pg[LANES:, :].astype(BF16)
        return _online_step(s3, lambda h: masks[h // GQA], dist, lambda p: _dot_nt(p, vt_bf), m, l, acc, tq)

    init = (jnp.full((rows_all, LANES), NEG_INF, F32), jnp.zeros((rows_all, LANES), F32),
            jnp.zeros((rows_all, LANES), F32))
    m, l, acc = lax.fori_loop(0, n, body, init)

    kvn = kvn_ref[...]
    zpad = jnp.zeros((LANES - tq, LANES), F32)
    k_new = jnp.concatenate([kvn[:, 2 * LANES:3 * LANES], zpad], axis=0).astype(BF16)
    v_new = jnp.concatenate([kvn[:, 3 * LANES:4 * LANES], zpad], axis=0).astype(BF16)
    pos = past_len + key
    mk_new = (key < tq) & (pos <= tpos)
    s3 = _dot_nt(q2, k_new).reshape(NSA_HEADS, tq, LANES)
    m, l, acc = _online_step(s3, lambda h: mk_new, (tpos - pos).astype(F32), lambda p: _dot(p, v_new),
                             m, l, acc, tq)
    o_sel = acc / jnp.maximum(l, TINY)

    win = win_ref[0]
    wl = WINDOW + LANES
    kw_new = jnp.concatenate([kvn[:, 4 * LANES:5 * LANES], zpad], axis=0).astype(BF16)
    vw_new = jnp.concatenate([kvn[:, 5 * LANES:6 * LANES], zpad], axis=0).astype(BF16)
    widx = lax.broadcasted_iota(I32, (tq, wl), 1)
    pw = past_len - WINDOW + widx
    dw = (past_len + lax.broadcasted_iota(I32, (tq, wl), 0)) - pw
    mw = (dw >= 0) & (dw < WINDOW) & (pw >= 0) & (widx < WINDOW + tq)
    s = jnp.concatenate([_dot(q2, win[:LANES, :].astype(BF16)), _dot_nt(q2, kw_new)], axis=1)
    pws = _masked_softmax_heads(s.reshape(NSA_HEADS, tq, wl), lambda h: mw, dw.astype(F32))
    p_win = jnp.concatenate(pws, axis=0).astype(BF16)
    o_win = _dot_nt(p_win[:, :WINDOW], win[LANES:, :].astype(BF16)) + _dot(p_win[:, WINDOW:], vw_new)

    o_ref[...] = _combine_heads(g_ref[...], ocmp_ref[0], o_sel, o_win, tq)


def _nsa_sample_sel(cache, layer, pages, lpages, cnt, selm, u_q, u_gate, u_kv, o_cmp, win_cache,
                    row0, nb, tq, past_len):
    wq = NSA_HEADS * HEAD_DIM
    base = row0 // tq
    n_cmp = past_len // BLK
    grid_spec = pltpu.PrefetchScalarGridSpec(
        num_scalar_prefetch=3,
        grid=(nb,),
        in_specs=[
            pl.BlockSpec(memory_space=pl.ANY),
            pl.BlockSpec((1, NSA_KV * tq, n_cmp), lambda bb, *_: (bb, 0, 0)),
            pl.BlockSpec((tq, wq), lambda bb, *_: (base + bb, 0)),
            pl.BlockSpec((tq, LANES), lambda bb, *_: (base + bb, 0)),
            pl.BlockSpec((tq, 6 * LANES), lambda bb, *_: (base + bb, 0)),
            pl.BlockSpec((1, NSA_HEADS * tq, LANES), lambda bb, *_: (bb, 0, 0)),
            pl.BlockSpec((None, 1, 2 * LANES, WINDOW), lambda bb, *_: (layer, bb, 0, 0)),
        ],
        out_specs=pl.BlockSpec((tq, wq), lambda bb, *_: (bb, 0)),
        scratch_shapes=[pltpu.VMEM((2, 2 * LANES, PAGE), F32), pltpu.SemaphoreType.DMA((2,))],
    )
    return pl.pallas_call(
        functools.partial(_nsa_sample_sel_kernel, layer=layer, tq=tq, past_len=past_len),
        grid_spec=grid_spec,
        out_shape=jax.ShapeDtypeStruct((nb * tq, wq), F32),
        compiler_params=_cp(("arbitrary",)),
        name="nsa_sample_sel",
    )(pages, lpages, cnt, cache, selm, u_q, u_gate, u_kv, o_cmp, win_cache)


def _outproj_kernel(xp_ref, xs_ref, cp_ref, cs_ref, ap_ref, as_ref, rp_ref, rs_ref, wo_ref, g_ref, b_ref, wr_ref, br_ref,
                    h_ref, te_ref, tg_ref, cnt_ref, *, alpha, n_prompt_tiles):
    is_sample = pl.program_id(0) >= n_prompt_tiles
    pick = lambda p_ref, s_ref: jnp.where(is_sample, s_ref[...], p_ref[...]).astype(BF16)
    mix = (_dot(pick(cp_ref, cs_ref), wo_ref[0:CONV_CH, :])
           + _dot(pick(ap_ref, as_ref), wo_ref[CONV_CH:CONV_CH + NSA_HEADS * HEAD_DIM, :])
           + _dot(pick(rp_ref, rs_ref), wo_ref[CONV_CH + NSA_HEADS * HEAD_DIM:, :]))
    x = jnp.where(is_sample, xs_ref[...], xp_ref[...])
    h = _layer_norm(alpha * x + mix, g_ref[...], b_ref[...])
    h_ref[...] = h
    h_hi = h.astype(BF16)
    h_lo = (h - h_hi.astype(F32)).astype(BF16)
    r = _dot(h_hi, wr_ref[...]) + _dot(h_lo, wr_ref[...])
    logits = r[:, 0:N_EXPERTS] + r[:, N_EXPERTS:2 * N_EXPERTS] + br_ref[...]
    idx = lax.broadcasted_iota(I32, logits.shape, 1).astype(F32)
    col = lax.broadcasted_iota(I32, (logits.shape[0], TOP_K), 1)
    vals = jnp.zeros((logits.shape[0], TOP_K), F32)
    ids = jnp.zeros((logits.shape[0], TOP_K), F32)
    cur = logits
    hist = jnp.zeros((1, N_EXPERTS), F32)
    for k in range(TOP_K):
        mx = jnp.max(cur, axis=-1, keepdims=True)
        am = jnp.min(jnp.where(cur == mx, idx, float(N_EXPERTS)), axis=-1, keepdims=True)
        vals = jnp.where(col == k, mx, vals)
        ids = jnp.where(col == k, am, ids)
        hit = idx == am
        hist = hist + jnp.sum(jnp.where(hit, 1.0, 0.0), axis=0, keepdims=True)
        cur = jnp.where(hit, -jnp.inf, cur)
    e = jnp.exp(vals - vals[:, 0:1])
    tg_ref[...] = e / jnp.sum(e, axis=-1, keepdims=True)
    te_ref[...] = ids.astype(I32)

    @pl.when(pl.program_id(0) == 0)
    def _():
        cnt_ref[...] = jnp.zeros(cnt_ref.shape, F32)

    cnt_ref[...] = cnt_ref[...] + hist


def _outproj(x_p, x_s, mixers, w_out_bf, g, b, w_router, b_router, alpha):
    d = x_p.shape[1]
    n = x_p.shape[0] + x_s.shape[0]
    tm = TOKEN_TILE
    npt = mixers[0][0].shape[0] // tm
    row = lambda c: pl.BlockSpec((tm, c), lambda i: (i, 0))
    row_p = lambda c: pl.BlockSpec((tm, c), lambda i: (jnp.minimum(i, npt - 1), 0))
    row_s = lambda c: pl.BlockSpec((tm, c), lambda i: (jnp.maximum(i - npt, 0), 0))
    full = lambda r, c: pl.BlockSpec((r, c), lambda i: (0, 0))
    mix_specs, mix_args = [], []
    for a_p, a_s in mixers:
        mix_specs += [row_p(a_p.shape[1]), row_s(a_s.shape[1])]
        mix_args += [a_p, a_s]
    return pl.pallas_call(
        functools.partial(_outproj_kernel, alpha=alpha, n_prompt_tiles=npt),
        grid=(n // tm,),
        in_specs=[row_p(d), row_s(d)] + mix_specs + [
                  full(d, d), full(1, d), full(1, d), full(d, LANES), full(1, N_EXPERTS)],
        out_specs=[row(d), row(TOP_K), row(TOP_K), full(1, N_EXPERTS)],
        out_shape=[jax.ShapeDtypeStruct((n, d), F32),
                   jax.ShapeDtypeStruct((n, TOP_K), I32), jax.ShapeDtypeStruct((n, TOP_K), F32),
                   jax.ShapeDtypeStruct((1, N_EXPERTS), F32)],
        compiler_params=_cp(("arbitrary",)),
        name="outproj",
    )(x_p, x_s, *mix_args, w_out_bf, g, b, w_router, b_router)


def _moe_kernel(blk_ref, exp_ref, lo_ref, hi_ref, first_ref, x_ref, wup_ref, bg_ref, bu_ref, wd_ref, bd_ref, o_ref,
                wg_scr, wu_scr, wd_scr):
    i = pl.program_id(0)
    dff = wg_scr.shape[1]

    @pl.when((i == 0) | (exp_ref[i] != exp_ref[jnp.maximum(i - 1, 0)]))
    def _():
        w2 = 2 * LANES
        r = lax.broadcasted_iota(I32, (w2, w2), 0)
        c = lax.broadcasted_iota(I32, (w2, w2), 1)
        perm = jnp.where(r == jnp.where(c < LANES, 2 * c, 2 * (c - LANES) + 1), 1.0, 0.0).astype(BF16)
        for k in range(2 * dff // w2):
            split = _dot(wup_ref[:, k * w2:(k + 1) * w2].astype(BF16), perm).astype(BF16)
            wg_scr[:, k * LANES:(k + 1) * LANES] = split[:, :LANES]
            wu_scr[:, k * LANES:(k + 1) * LANES] = split[:, LANES:]
        wd_scr[...] = wd_ref[...].astype(BF16)

    lo, hi = lo_ref[i], hi_ref[i]

    @pl.when(hi > lo)
    def _():
        x = x_ref[...].astype(BF16)
        g = jnp.minimum(_dot(x, wg_scr[...]) + bg_ref[...], SWIGLU_LIMIT)
        u = jnp.clip(_dot(x, wu_scr[...]) + bu_ref[...], -SWIGLU_LIMIT, SWIGLU_LIMIT)
        act = (u + 1.0) * g * _sigmoid(SWIGLU_ALPHA * g)
        y = _dot(act.astype(BF16), wd_scr[...]) + bd_ref[...]
        row = lax.broadcasted_iota(I32, y.shape, 0)
        mine = (row >= lo) & (row < hi)

        @pl.when(first_ref[i] == 1)
        def _():
            o_ref[...] = jnp.where(mine, y, 0.0)

        @pl.when(first_ref[i] == 0)
        def _():
            o_ref[...] = jnp.where(mine, y, o_ref[...])


def _moe_blocks(xs, items, w_up, bg, bu, w_down, bd, layer):
    r, d = xs.shape
    bm = MOE_TILE
    dff = w_down.shape[2]
    n_items = items[0].shape[0]
    wspec = lambda a, c: pl.BlockSpec((None, None, a, c), lambda i, blk, exp, *_: (layer, exp[i], 0, 0))
    bspec = lambda c: pl.BlockSpec((None, 1, c), lambda i, blk, exp, *_: (exp[i], 0, 0))
    grid_spec = pltpu.PrefetchScalarGridSpec(
        num_scalar_prefetch=5,
        grid=(n_items,),
        in_specs=[
            pl.BlockSpec((bm, d), lambda i, blk, *_: (blk[i], 0)),
            wspec(d, 2 * dff), bspec(dff), bspec(dff), wspec(dff, d), bspec(d),
        ],
        out_specs=pl.BlockSpec((bm, d), lambda i, blk, *_: (blk[i], 0)),
        scratch_shapes=[pltpu.VMEM((d, dff), BF16), pltpu.VMEM((d, dff), BF16), pltpu.VMEM((dff, d), BF16)],
    )
    return pl.pallas_call(
        _moe_kernel,
        grid_spec=grid_spec,
        out_shape=jax.ShapeDtypeStruct((r, d), F32),
        compiler_params=_cp(("arbitrary",)),
        name="moe_experts",
    )(*items, xs, w_up, bg, bu, w_down, bd)


def _moe(h, top_e, counts, w_up, bg, bu, w_down, bd, layer):
    n, d = h.shape
    bm = MOE_TILE
    nk = n * TOP_K
    assert nk % bm == 0
    n_blk = nk // bm
    experts = jnp.arange(N_EXPERTS, dtype=I32)
    flat_e = top_e.T.reshape(nk)
    order = jnp.argsort(flat_e).astype(I32)
    rank = jnp.argsort(order).astype(I32)
    ends = jnp.cumsum(counts)
    starts = ends - counts
    has = ends > starts
    first_blk = starts // bm
    per_e = jnp.where(has, (ends - 1) // bm - first_blk + 1, 0)
    it_end = jnp.cumsum(per_e)
    it_start = it_end - per_e
    n_items = n_blk + N_EXPERTS - 1
    t = jnp.arange(n_items, dtype=I32)
    live = t < it_end[-1]
    e_t = jnp.minimum(jnp.sum((it_end[None, :] <= t[:, None]).astype(I32), axis=1), N_EXPERTS - 1)
    blk_t = first_blk[e_t] + t - it_start[e_t]
    lo = jnp.clip(starts[e_t] - blk_t * bm, 0, bm)
    hi = jnp.clip(ends[e_t] - blk_t * bm, 0, bm)
    e_last = jnp.max(jnp.where(has, experts, 0))
    blk_t = jnp.where(live, blk_t, n_blk - 1)
    e_t = jnp.where(live, e_t, e_last)
    lo = jnp.where(live, lo, 0)
    hi = jnp.where(live, hi, 0)
    first = jnp.concatenate([jnp.ones((1,), bool), blk_t[1:] != blk_t[:-1]]) & live
    items = tuple(a.astype(I32) for a in (blk_t, e_t, lo, hi, first))
    xs = jnp.take(h, order % n, axis=0, mode='clip')
    yb = _moe_blocks(xs, items, w_up, bg, bu, w_down, bd, layer)
    return jnp.take(yb, rank, axis=0, mode='clip')


def _final_kernel(h_ref, *refs, alpha, n_prompt_tiles):
    f_refs = refs[:TOP_K]
    tg_ref, pp_ref, ps_ref, wple_ref, wplg_ref, g_ref, b_ref, yp_ref, ys_ref = refs[TOP_K:]
    h = h_ref[...]
    tg = tg_ref[...]
    ffn = tg[:, 0:1] * f_refs[0][...]
    for k in range(1, TOP_K):
        ffn = ffn + tg[:, k:k + 1] * f_refs[k][...]
    is_sample = pl.program_id(0) >= n_prompt_tiles
    p = jnp.where(is_sample, ps_ref[...], pp_ref[...]).astype(BF16)
    ple = _dot(p, wple_ref[...]) * _sigmoid(_dot(h.astype(BF16), wplg_ref[...]))
    y = _layer_norm(alpha * h + ffn + ple, g_ref[...], b_ref[...])

    @pl.when(jnp.logical_not(is_sample))
    def _():
        yp_ref[...] = y

    @pl.when(is_sample)
    def _():
        ys_ref[...] = y


def _final(h, expert_out, gate, p_prompt, p_sample, layer, w_ple_bf, w_plg_bf, g, b, alpha):
    n, d = h.shape
    tm = TOKEN_TILE
    pd = p_prompt.shape[2]
    npt = p_prompt.shape[1] // tm
    row = lambda c: pl.BlockSpec((tm, c), lambda i: (i, 0))
    full = lambda r, c: pl.BlockSpec((r, c), lambda i: (0, 0))
    kth = lambda k: pl.BlockSpec((tm, d), lambda i: (k * (n // tm) + i, 0))
    p_specs = [pl.BlockSpec((None, tm, pd), lambda i: (layer, jnp.minimum(i, npt - 1), 0)),
               pl.BlockSpec((None, tm, pd), lambda i: (layer, jnp.maximum(i - npt, 0), 0))]
    return pl.pallas_call(
        functools.partial(_final_kernel, alpha=alpha, n_prompt_tiles=npt),
        grid=(n // tm,),
        in_specs=[row(d)] + [kth(k) for k in range(TOP_K)]
        + [row(TOP_K)] + p_specs + [full(pd, d), full(d, d), full(1, d), full(1, d)],
        out_specs=[pl.BlockSpec((tm, d), lambda i: (jnp.minimum(i, npt - 1), 0)),
                   pl.BlockSpec((tm, d), lambda i: (jnp.maximum(i - npt, 0), 0))],
        out_shape=[jax.ShapeDtypeStruct((npt * tm, d), F32), jax.ShapeDtypeStruct((n - npt * tm, d), F32)],
        compiler_params=_cp(("arbitrary",)),
        name="final",
    )(h, *([expert_out] * TOP_K), gate, p_prompt, p_sample, w_ple_bf, w_plg_bf, g, b)


def kernel(x_prompt, x_sample, p_prompt, p_sample, cache_nsa_kv, cache_win_kv, state_ret, state_conv, page_table,
           w_in, w_out, conv_w, conv_b, conv_ln_g, conv_ln_b, ret_norm_g, ln1_g, ln1_b, w_router, b_router,
           w_up, b_up, w_down, b_down, w_ple, w_plg, ln2_g, ln2_b):
    bp, seq, d = x_prompt.shape
    bs, tdec, _ = x_sample.shape
    depth = w_in.shape[0]
    n_pages = page_table.shape[1]
    past_len = n_pages * PAGE
    wbuf = cache_win_kv.shape[2]
    n_p, n_s = bp * seq, bs * tdec
    assert wbuf == WINDOW and tdec <= BLK and tdec % 8 == 0 and past_len // BLK >= N_SEL
    assert seq % (NSA_TQ * NSA_KEY_TILES) == 0 and seq >= WINDOW + NSA_TQ
    assert n_p % TOKEN_TILE == 0 and n_s % TOKEN_TILE == 0
    alpha = (2 * depth) ** 0.25
    n_pool = cache_nsa_kv.shape[1]
    cache = jnp.transpose(cache_nsa_kv, (0, 1, 3, 4, 5, 2)).reshape(depth, n_pool, 4 * LANES, PAGE)
    win_t = jnp.transpose(cache_win_kv, (0, 1, 3, 4, 5, 2)).reshape(depth, bs, 2 * LANES, wbuf)
    page_table = page_table.astype(I32)

    c_conv = 2 * CONV_CH
    c_q = NSA_HEADS * HEAD_DIM
    c_kv = 6 * NSA_KV * HEAD_DIM
    c_gate = 3 * NSA_HEADS
    cuts = [0, c_conv, c_conv + c_q, c_conv + c_q + c_kv, c_conv + c_q + c_kv + c_gate, w_in.shape[2]]

    x_p, x_s = x_prompt.reshape(n_p, d), x_sample.reshape(n_s, d)
    outs = {k: [] for k in ("kvp", "kvs", "wp", "ws", "rp", "rs", "cp", "cs")}
    zeros_ret = jnp.zeros((bp, RET_HEADS, HEAD_DIM, HEAD_DIM), F32)
    zeros_conv = jnp.zeros((bp, CONV_K - 1, CONV_CH), F32)

    for i in range(depth):
        wi = w_in[i].astype(BF16)
        ws = [wi[:, cuts[j]:cuts[j + 1]] for j in range(5)]
        ws[3] = jnp.pad(ws[3], ((0, 0), (0, LANES - c_gate)))
        u_conv, u_q, u_kv, u_gate, u_ret, kv_t = _inproj(x_p, x_s, ws, bp, seq)

        cw = jnp.pad(conv_w[i], ((0, 32 - CONV_K), (0, 0)))
        cargs = (cw, conv_b[i][None], conv_ln_g[i][None], conv_ln_b[i][None])
        conv_p, conv_sp = _conv(u_conv, zeros_conv, *cargs, 0, bp, seq)
        conv_s, conv_ss = _conv(u_conv, state_conv[i], *cargs, n_p, bs, tdec)

        rgain = ret_norm_g[i][None]
        ret_p, ret_sp = _retention(u_ret, zeros_ret, rgain, 0, bp, seq)
        ret_s, ret_ss = _retention(u_ret, state_ret[i], rgain, n_p, bs, tdec)

        nsa_p = _nsa_prompt(u_q, u_gate, u_kv, bp, seq)
        o_cmp, blk_score = _nsa_sample_cmp(cache, i, page_table, u_q, n_p, bs, tdec)
        selm = _topk_blocks(blk_score.reshape(bs * NSA_KV * tdec, -1), N_SEL - 1).reshape(blk_score.shape)
        need = selm.reshape(bs, NSA_KV * tdec, n_pages, PAGE // BLK).max(axis=(1, 3)) > 0.5
        npos = jnp.cumsum(need.astype(I32), axis=1) - 1
        cnt = npos[:, -1] + 1
        pidx = jnp.arange(n_pages, dtype=I32)
        hit = need[:, :, None] & (npos[:, :, None] == pidx[None, None, :])
        order = jnp.sum(jnp.where(hit, pidx[None, :, None], 0), axis=1).astype(I32)
        pages = jnp.take_along_axis(page_table, order, axis=1)
        nsa_s = _nsa_sample_sel(cache, i, pages, order, cnt, selm, u_q, u_gate, u_kv, o_cmp, win_t,
                                n_p, bs, tdec, past_len)

        wr_hi = w_router[i].astype(BF16)
        wr_lo = (w_router[i] - wr_hi.astype(F32)).astype(BF16)
        wr_cat = jnp.pad(jnp.concatenate([wr_hi, wr_lo], axis=1), ((0, 0), (0, LANES - 2 * N_EXPERTS)))
        h, top_e, gate, counts = _outproj(x_p, x_s, ((conv_p, conv_s), (nsa_p, nsa_s), (ret_p, ret_s)),
                                          w_out[i].astype(BF16),
                                          ln1_g[i][None], ln1_b[i][None], wr_cat, b_router[i][None], alpha)

        expert_out = _moe(h, top_e, counts[0].astype(I32), w_up, b_up[i][:, None, 0::2], b_up[i][:, None, 1::2],
                          w_down, b_down[i][:, None, :], i)

        x_p, x_s = _final(h, expert_out, gate, p_prompt.reshape(depth, n_p, -1), p_sample.reshape(depth, n_s, -1), i,
                       w_ple[i].astype(BF16), w_plg[i].astype(BF16),
                       ln2_g[i][None], ln2_b[i][None], alpha)

        kv_shape = (4, NSA_KV, HEAD_DIM)
        outs["kvp"].append(jnp.transpose(kv_t[:, :4 * LANES].reshape(bp, *kv_shape, seq), (0, 4, 1, 2, 3)))
        outs["kvs"].append(u_kv[n_p:, :4 * LANES].reshape(bs, tdec, *kv_shape))
        n_keep = min(WINDOW, seq)
        win_t_p = kv_t[:, 4 * LANES:, seq - n_keep:].reshape(bp, 2, NSA_KV, HEAD_DIM, n_keep)
        outs["wp"].append(jnp.transpose(win_t_p, (0, 4, 1, 2, 3)))
        win_new_s = u_kv[n_p:, 4 * LANES:].reshape(bs, tdec, 2, NSA_KV, HEAD_DIM)
        outs["ws"].append(jnp.concatenate([cache_win_kv[i][:, tdec:], win_new_s], axis=1))
        outs["rp"].append(ret_sp)
        outs["rs"].append(ret_ss)
        outs["cp"].append(conv_sp)
        outs["cs"].append(conv_ss)

    y_p = x_p.reshape(bp, seq, d)
    y_s = x_s.reshape(bs, tdec, d)
    st = lambda k: jnp.stack(outs[k])
    return (y_p, y_s, st("kvp"), st("kvs"), st("wp"), st("ws"), st("rp"), st("rs"), st("cp"), st("cs"))
```

```python
import functools

import jax
import jax.numpy as jnp
from jax import lax
from jax.experimental import pallas as pl
from jax.experimental.pallas import tpu as pltpu

F32 = jnp.float32
BF16 = jnp.bfloat16
I32 = jnp.int32

HEAD_DIM = 64
CONV_CH = 256
CONV_K = 31
NSA_HEADS = 8
NSA_KV = 2
GQA = 4
BLK = 64
N_SEL = 16
WINDOW = 512
FORCE_SCORE = 1e4
RET_HEADS = 4
RET_CHUNK = 128
N_EXPERTS = 32
TOP_K = 4
SWIGLU_LIMIT = 7.0
SWIGLU_ALPHA = 1.702
PAGE = 128
LN_EPS = 1e-5
NEG_INF = -1e30
TINY = 1e-30
LANES = 128

TOKEN_TILE = 256
MOE_TILE = 256
NSA_TQ = 128
NSA_KEY_TILES = 4
SAMPLE_PAGES_PER_STEP = 16
VMEM_LIMIT = 56 * 1024 * 1024


def _cp(sem):
    return pltpu.CompilerParams(dimension_semantics=sem, vmem_limit_bytes=VMEM_LIMIT)


def _dot(a, b):
    return jnp.dot(a, b, preferred_element_type=F32)


def _dot_nt(a, b):
    return lax.dot_general(a, b, (((1,), (1,)), ((), ())), preferred_element_type=F32)


def _dot_tn(a, b):
    return lax.dot_general(a, b, (((0,), (0,)), ((), ())), preferred_element_type=F32)


def _layer_norm(x, g, b):
    mu = jnp.mean(x, axis=-1, keepdims=True)
    xc = x - mu
    var = jnp.mean(xc * xc, axis=-1, keepdims=True)
    return xc * lax.rsqrt(var + LN_EPS) * g + b


def _sigmoid(x):
    return 1.0 / (1.0 + jnp.exp(-x))


def _inproj_kernel(xp_ref, xs_ref, wc_ref, wq_ref, wkv_ref, wg_ref, wr_ref, oc_ref, oq_ref, okv_ref, og_ref, or_ref,
                   opgt_ref, owint_ref, *, n_prompt_tiles):
    x = jnp.where(pl.program_id(0) >= n_prompt_tiles, xs_ref[...], xp_ref[...]).astype(BF16)
    oc_ref[...] = _dot(x, wc_ref[...])
    oq_ref[...] = _dot(x, wq_ref[...])
    ukv = _dot(x, wkv_ref[...])
    okv_ref[...] = ukv
    og_ref[...] = _dot(x, wg_ref[...])
    or_ref[...] = _dot(x, wr_ref[...])

    @pl.when(pl.program_id(0) < n_prompt_tiles)
    def _():
        ukv_t = ukv.T
        n_paged = opgt_ref.shape[0]
        opgt_ref[...] = ukv_t[:n_paged]
        owint_ref[...] = ukv_t[n_paged:]


def _inproj(x_p, x_s, ws, bp, seq):
    d = x_p.shape[1]
    n = x_p.shape[0] + x_s.shape[0]
    tm = TOKEN_TILE
    widths = [w.shape[1] for w in ws]
    c_kv = widths[2]
    tiles_per_seq = seq // tm
    npt = bp * tiles_per_seq
    c_paged = 4 * LANES
    kvt_spec = lambda c: pl.BlockSpec((None, c, tm), lambda i: (jnp.minimum(i, npt - 1) // tiles_per_seq, 0,
                                                                jnp.minimum(i, npt - 1) % tiles_per_seq))
    return pl.pallas_call(
        functools.partial(_inproj_kernel, n_prompt_tiles=npt),
        grid=(n // tm,),
        in_specs=[pl.BlockSpec((tm, d), lambda i: (jnp.minimum(i, npt - 1), 0)),
                  pl.BlockSpec((tm, d), lambda i: (jnp.maximum(i - npt, 0), 0))]
        + [pl.BlockSpec((d, c), lambda i: (0, 0)) for c in widths],
        out_specs=[pl.BlockSpec((tm, c), lambda i: (i, 0)) for c in widths]
        + [kvt_spec(c_paged), kvt_spec(c_kv - c_paged)],
        out_shape=[jax.ShapeDtypeStruct((n, c), F32) for c in widths]
        + [jax.ShapeDtypeStruct((bp, c_paged, seq), F32), jax.ShapeDtypeStruct((bp, c_kv - c_paged, seq), F32)],
        compiler_params=_cp(("arbitrary",)),
        name="inproj",
    )(x_p, x_s, *ws)


def _conv_kernel(u_ref, s0_ref, w_ref, cb_ref, g_ref, b_ref, o_ref, st_ref, ext, *, tt):
    i = pl.program_id(1)
    pad = 32 - (CONV_K - 1)

    @pl.when(i == 0)
    def _():
        ext[0:pad, :] = jnp.zeros((pad, CONV_CH), F32)
        ext[pad:32, :] = s0_ref[0]

    u = u_ref[...]
    h = u[:, :CONV_CH] * _sigmoid(u[:, CONV_CH:])
    ext[32:32 + tt, :] = h
    acc = jnp.zeros((tt, CONV_CH), F32)
    for j in range(CONV_K):
        acc = acc + ext[pad + j:pad + j + tt, :] * w_ref[j:j + 1, :]
    y = _layer_norm(acc + cb_ref[...], g_ref[...], b_ref[...])
    o_ref[...] = y * _sigmoid(y)
    tail = ext[tt + pad:tt + 32, :]

    @pl.when(i == pl.num_programs(1) - 1)
    def _():
        st_ref[0] = tail

    ext[pad:32, :] = tail


def _conv(u_conv, s0, w, cb, g, b, row0, nb, t):
    tt = min(t, 256)
    nt = t // tt
    base = row0 // tt
    in_specs = [
        pl.BlockSpec((tt, 2 * CONV_CH), lambda bb, i: (base + bb * nt + i, 0)),
        pl.BlockSpec((1, CONV_K - 1, CONV_CH), lambda bb, i: (bb, 0, 0)),
        pl.BlockSpec((32, CONV_CH), lambda bb, i: (0, 0)),
        pl.BlockSpec((1, CONV_CH), lambda bb, i: (0, 0)),
        pl.BlockSpec((1, CONV_CH), lambda bb, i: (0, 0)),
        pl.BlockSpec((1, CONV_CH), lambda bb, i: (0, 0)),
    ]
    args = [u_conv, s0, w, cb, g, b]
    return pl.pallas_call(
        functools.partial(_conv_kernel, tt=tt),
        grid=(nb, nt),
        in_specs=in_specs,
        out_specs=[
            pl.BlockSpec((tt, CONV_CH), lambda bb, i: (bb * nt + i, 0)),
            pl.BlockSpec((1, CONV_K - 1, CONV_CH), lambda bb, i: (bb, 0, 0)),
        ],
        out_shape=[
            jax.ShapeDtypeStruct((nb * t, CONV_CH), F32),
            jax.ShapeDtypeStruct((nb, CONV_K - 1, CONV_CH), F32),
        ],
        scratch_shapes=[pltpu.VMEM((32 + tt, CONV_CH), F32)],
        compiler_params=_cp(("parallel", "arbitrary")),
        name="conv",
    )(*args)


def _ret_kernel(u_ref, s0_ref, dm_ref, qd_ref, kd_ref, cd_ref, g_ref, o_ref, st_ref, s_scr):
    i = pl.program_id(1)

    @pl.when(i == 0)
    def _():
        s_scr[...] = s0_ref[0]

    u = u_ref[...]
    w = RET_HEADS * HEAD_DIM
    outs = []
    for h in range(RET_HEADS):
        lo = h * HEAD_DIM
        q = u[:, lo:lo + HEAD_DIM]
        k = u[:, w + lo:w + lo + HEAD_DIM] * (HEAD_DIM ** -0.5)
        v = u[:, 2 * w + lo:2 * w + lo + HEAD_DIM]
        rg = u[:, 3 * w + lo:3 * w + lo + HEAD_DIM]
        qb, kb, vb = q.astype(BF16), k.astype(BF16), v.astype(BF16)
        att = _dot_nt(qb, kb) * dm_ref[h]
        s_h = s_scr[h]
        o = _dot(att.astype(BF16), vb) + _dot(qb, s_h.astype(BF16)) * qd_ref[h]
        kdec = (k * kd_ref[h]).astype(BF16)
        s_scr[h] = s_h * cd_ref[h] + _dot_tn(kdec, vb)
        mu = jnp.mean(o, axis=-1, keepdims=True)
        oc = o - mu
        var = jnp.mean(oc * oc, axis=-1, keepdims=True)
        on = oc * lax.rsqrt(var + LN_EPS) * g_ref[:, lo:lo + HEAD_DIM]
        outs.append(rg * _sigmoid(rg) * on)
    o_ref[...] = jnp.concatenate(outs, axis=1)

    @pl.when(i == pl.num_programs(1) - 1)
    def _():
        st_ref[0] = s_scr[...]


def _ret_tables(c):
    log_g = jnp.log1p(-jnp.exp2(-5.0 - jnp.arange(RET_HEADS, dtype=F32)))
    i = jnp.arange(c, dtype=F32)
    diff = i[:, None] - i[None, :]
    dmask = jnp.exp(jnp.where(diff >= 0, log_g[:, None, None] * diff, -jnp.inf))
    q_dec = jnp.exp(log_g[:, None] * (i[None, :] + 1.0))
    k_dec = jnp.exp(log_g[:, None] * (c - 1.0 - i[None, :]))
    c_dec = jnp.exp(log_g * c)
    bc = lambda a: jnp.broadcast_to(a[:, :, None], a.shape + (HEAD_DIM,))
    cd = jnp.broadcast_to(c_dec[:, None, None], (RET_HEADS, HEAD_DIM, HEAD_DIM))
    return dmask, bc(q_dec), bc(k_dec), cd


def _retention(u_ret, s0, gain, row0, nb, t):
    c = min(t, RET_CHUNK)
    while t % c:
        c -= 1
    nt = t // c
    base = row0 // c
    dm, qd, kd, cd = _ret_tables(c)
    w = RET_HEADS * HEAD_DIM
    full = lambda shape: pl.BlockSpec(shape, lambda bb, i: (0,) * len(shape))
    in_specs = [
        pl.BlockSpec((c, 4 * w), lambda bb, i: (base + bb * nt + i, 0)),
        pl.BlockSpec((1, RET_HEADS, HEAD_DIM, HEAD_DIM), lambda bb, i: (bb, 0, 0, 0)),
        full((RET_HEADS, c, c)),
        full((RET_HEADS, c, HEAD_DIM)),
        full((RET_HEADS, c, HEAD_DIM)),
        full((RET_HEADS, HEAD_DIM, HEAD_DIM)),
        full((1, w)),
    ]
    args = [u_ret, s0, dm, qd, kd, cd, gain]
    return pl.pallas_call(
        _ret_kernel,
        grid=(nb, nt),
        in_specs=in_specs,
        out_specs=[
            pl.BlockSpec((c, w), lambda bb, i: (bb * nt + i, 0)),
            pl.BlockSpec((1, RET_HEADS, HEAD_DIM, HEAD_DIM), lambda bb, i: (bb, 0, 0, 0)),
        ],
        out_shape=[
            jax.ShapeDtypeStruct((nb * t, w), F32),
            jax.ShapeDtypeStruct((nb, RET_HEADS, HEAD_DIM, HEAD_DIM), F32),
        ],
        scratch_shapes=[pltpu.VMEM((RET_HEADS, HEAD_DIM, HEAD_DIM), F32)],
        compiler_params=_cp(("parallel", "arbitrary")),
        name="retention",
    )(*args)


def _slope(h):
    return 2.0 ** -(h + 1)


def _build_q2(uq, tq):
    lane = lax.broadcasted_iota(I32, (tq, LANES), 1)
    parts = []
    for h in range(NSA_HEADS):
        kvh = h // GQA
        p = uq[:, (h // 2) * LANES:(h // 2 + 1) * LANES]
        if (h % 2) != kvh:
            p = pltpu.roll(p, HEAD_DIM, 1)
        keep = (lane >= HEAD_DIM) if kvh == 1 else (lane < HEAD_DIM)
        parts.append(jnp.where(keep, p * (HEAD_DIM ** -0.5), 0.0))
    return jnp.concatenate(parts, axis=0).astype(BF16)


def _masked_softmax_heads(s3, mask_of_head, dist):
    ps = []
    for h in range(NSA_HEADS):
        mk = mask_of_head(h)
        sh = jnp.where(mk, s3[h] - _slope(h) * dist, NEG_INF)
        mx = jnp.max(sh, axis=-1, keepdims=True)
        e = jnp.where(mk, jnp.exp(sh - mx), 0.0)
        den = jnp.maximum(jnp.sum(e, axis=-1, keepdims=True), TINY)
        ps.append(e / den)
    return ps


def _online_step(s3, mask_of_head, dist, pv, m, l, acc, tq):
    ps, alphas, ms, ls = [], [], [], []
    for h in range(NSA_HEADS):
        mk = mask_of_head(h)
        rows = slice(h * tq, (h + 1) * tq)
        sh = jnp.where(mk, s3[h] - _slope(h) * dist, NEG_INF)
        m_old = m[rows]
        m_new = jnp.maximum(m_old, jnp.max(sh, axis=-1, keepdims=True))
        p = jnp.where(mk, jnp.exp(sh - jnp.concatenate([m_new] * (sh.shape[1] // LANES), axis=1)), 0.0)
        alpha = jnp.exp(m_old - m_new)
        ls.append(alpha * l[rows] + jnp.sum(p, axis=-1, keepdims=True))
        ms.append(m_new)
        alphas.append(alpha)
        ps.append(p)
    p_all = jnp.concatenate(ps, axis=0).astype(BF16)
    alpha_all = jnp.concatenate(alphas, axis=0)
    acc = acc * alpha_all + pv(p_all)
    return jnp.concatenate(ms, axis=0), jnp.concatenate(ls, axis=0), acc


def _alibi_query_lanes(tq):
    lane = lax.broadcasted_iota(I32, (tq, LANES), 1)
    parts = [jnp.where(lane == 0, _slope(h), jnp.where(lane == 1, _slope(h) * LANES, 0.0))
             for h in range(NSA_HEADS)]
    return jnp.concatenate(parts, axis=0).astype(BF16)


def _alibi_key_lanes(pos):
    lane = lax.broadcasted_iota(I32, pos.shape, 1)
    return jnp.where(lane == 0, pos & (LANES - 1), jnp.where(lane == 1, pos >> 7, 0)).astype(F32).astype(BF16)


def _combine_heads(gl, o_cmp, o_sel, o_win, tq):
    gs = _sigmoid(gl)
    lane = lax.broadcasted_iota(I32, (tq, LANES), 1)
    pairs = []
    for mpair in range(NSA_HEADS // 2):
        halves = []
        for h in (2 * mpair, 2 * mpair + 1):
            rows = slice(h * tq, (h + 1) * tq)
            o = (gs[:, 3 * h:3 * h + 1] * o_cmp[rows] + gs[:, 3 * h + 1:3 * h + 2] * o_sel[rows]
                 + gs[:, 3 * h + 2:3 * h + 3] * o_win[rows])
            if (h // GQA) != (h % 2):
                o = pltpu.roll(o, HEAD_DIM, 1)
            halves.append(o)
        pairs.append(jnp.where(lane < HEAD_DIM, halves[0], halves[1]))
    return jnp.concatenate(pairs, axis=1)


def _cmp_probs(s, t0, tq, n_lanes):
    s3 = s.reshape(NSA_HEADS, tq, n_lanes)
    tpos = t0 + lax.broadcasted_iota(I32, (tq, n_lanes), 0)
    cend = lax.broadcasted_iota(I32, (tq, n_lanes), 1) * BLK + (BLK - 1)
    valid = cend <= tpos
    dist = (tpos - cend).astype(F32)
    return _masked_softmax_heads(s3, lambda h: valid, dist)


def _block_scores(ps, kv, t0, tq, n_lanes):
    tpos = t0 + lax.broadcasted_iota(I32, (tq, n_lanes), 0)
    nb = lax.broadcasted_iota(I32, (tq, n_lanes), 1)
    cur = tpos >> 6
    imp = ps[kv * GQA] + ps[kv * GQA + 1] + ps[kv * GQA + 2] + ps[kv * GQA + 3]
    forced = (nb == 0) | (nb == cur) | (nb == cur - 1)
    allowed = nb <= cur
    return jnp.where(allowed, jnp.where(forced, FORCE_SCORE, imp), -1.0), allowed, nb


def _nsa_prompt_kernel(q_ref, g_ref, kv_ref, o_ref, kb_ref, kc_ref, e_ref, sel_ref, q2e_ref, m_ref, l_ref, acc_ref,
                       *, t, tq):
    i = pl.program_id(1)
    n_blk = t // BLK
    rows_all = NSA_HEADS * tq

    @pl.when(i == 0)
    def _():
        kc_ref[...] = jnp.zeros(kc_ref.shape, F32)
        step = 256
        for c in range(t // step):
            sl = slice(c * step, (c + 1) * step)
            pos_lanes = _alibi_key_lanes(c * step + lax.broadcasted_iota(I32, (step, LANES), 0))
            kb_ref[sl, 0:LANES] = kv_ref[sl, 2 * LANES:3 * LANES].astype(BF16)
            kb_ref[sl, LANES:2 * LANES] = pos_lanes
            kb_ref[sl, 2 * LANES:3 * LANES] = kv_ref[sl, 4 * LANES:5 * LANES].astype(BF16)
            kb_ref[sl, 3 * LANES:4 * LANES] = pos_lanes
            kb_ref[sl, 4 * LANES:5 * LANES] = kv_ref[sl, 3 * LANES:4 * LANES].astype(BF16)
            kb_ref[sl, 5 * LANES:6 * LANES] = kv_ref[sl, 5 * LANES:6 * LANES].astype(BF16)
            kc_ref[c * (step // BLK):(c + 1) * (step // BLK), :] = (
                kv_ref[sl, 0:2 * LANES].reshape(step // BLK, BLK, 2 * LANES).sum(axis=1) * (1.0 / BLK))
        blk_of_key = lax.broadcasted_iota(I32, (LANES, t), 1) >> 6
        e_ref[...] = jnp.where(blk_of_key == lax.broadcasted_iota(I32, (LANES, t), 0), 1.0, 0.0).astype(BF16)

    t0 = i * tq
    q2 = _build_q2(q_ref[...], tq)

    kc = kc_ref[0:n_blk, :]
    s_t = _dot_nt(kc[:, :LANES].astype(BF16), q2)
    blk = lax.broadcasted_iota(I32, (n_blk, tq), 0)
    tpos_c = t0 + lax.broadcasted_iota(I32, (n_blk, tq), 1)
    cend = blk * BLK + (BLK - 1)
    valid = cend <= tpos_c
    dist_c = (tpos_c - cend).astype(F32)
    p_t = []
    for h in range(NSA_HEADS):
        sh = jnp.where(valid, s_t[:, h * tq:(h + 1) * tq] - _slope(h) * dist_c, NEG_INF)
        e = jnp.where(valid, jnp.exp(sh - jnp.max(sh, axis=0, keepdims=True)), 0.0)
        p_t.append(e / jnp.maximum(jnp.sum(e, axis=0, keepdims=True), TINY))
    o_cmp = _dot_tn(jnp.concatenate(p_t, axis=1).astype(BF16), kc[:, LANES:].astype(BF16))

    cur = tpos_c >> 6
    allowed = blk <= cur
    forced = (blk == 0) | (blk == cur) | (blk == cur - 1)
    eye = jnp.where(lax.broadcasted_iota(I32, (n_blk, LANES), 0) == lax.broadcasted_iota(I32, (n_blk, LANES), 1),
                    1.0, 0.0).astype(BF16)
    for kv in range(NSA_KV):
        imp = p_t[kv * GQA] + p_t[kv * GQA + 1] + p_t[kv * GQA + 2] + p_t[kv * GQA + 3]
        score = jnp.where(allowed, jnp.where(forced, FORCE_SCORE, imp), -1.0)
        rank = jnp.zeros((n_blk, tq), F32)
        for mblk in range(n_blk):
            c = score[mblk:mblk + 1, :]
            tie = jnp.where(blk > mblk, 1.0, 0.0)
            rank = rank + jnp.where(c > score, 1.0, jnp.where(c == score, tie, 0.0))
        sel_t = jnp.where(allowed, jnp.where(rank < N_SEL, 1.0, 0.0), 0.0).astype(BF16)
        sel_ref[kv] = _dot_tn(sel_t, eye).astype(BF16)

    q2e_ref[...] = jnp.concatenate([q2, _alibi_query_lanes(tq)], axis=1)

    m_ref[...] = jnp.full((rows_all, LANES), NEG_INF, F32)
    l_ref[...] = jnp.zeros((rows_all, LANES), F32)
    acc_ref[...] = jnp.zeros((rows_all, LANES), F32)
    tk = NSA_KEY_TILES * tq
    tpos = t0 + lax.broadcasted_iota(I32, (tq, tk), 0)

    def body(kt, carry):
        k0 = pl.multiple_of(kt * tk, tk)
        k_bf = kb_ref[pl.ds(k0, tk), 0:2 * LANES]
        v_bf = kb_ref[pl.ds(k0, tk), 4 * LANES:5 * LANES]
        causal = (k0 + lax.broadcasted_iota(I32, (tq, tk), 1)) <= tpos
        e_tile = e_ref[:, pl.ds(k0, tk)]
        biases = [jnp.where(causal, jnp.where(_dot(sel_ref[kv], e_tile) > 0.5, 0.0, NEG_INF), NEG_INF)
                  for kv in range(NSA_KV)]
        s3 = _dot_nt(q2e_ref[...], k_bf).reshape(NSA_HEADS, tq, tk)
        m_all, l_all = m_ref[...], l_ref[...]
        ps, alphas, ms, ls = [], [], [], []
        for h in range(NSA_HEADS):
            rows = slice(h * tq, (h + 1) * tq)
            sh = s3[h] + biases[h // GQA]
            m_old = m_all[rows]
            m_new = jnp.maximum(m_old, jnp.max(sh, axis=-1, keepdims=True))
            p = jnp.exp(sh - jnp.concatenate([m_new] * NSA_KEY_TILES, axis=1))
            alpha = jnp.exp(m_old - m_new)
            p_lanes = p[:, :LANES]
            for c in range(1, NSA_KEY_TILES):
                p_lanes = p_lanes + p[:, c * LANES:(c + 1) * LANES]
            ls.append(alpha * l_all[rows] + p_lanes)
            ms.append(m_new)
            alphas.append(alpha)
            ps.append(p.astype(BF16))
        acc_ref[...] = acc_ref[...] * jnp.concatenate(alphas, axis=0) + _dot(jnp.concatenate(ps, axis=0), v_bf)
        m_ref[...] = jnp.concatenate(ms, axis=0)
        l_ref[...] = jnp.concatenate(ls, axis=0)
        return carry

    lax.fori_loop(0, (i + NSA_KEY_TILES) // NSA_KEY_TILES, body, 0)
    o_sel = acc_ref[...] / jnp.sum(l_ref[...], axis=-1, keepdims=True)

    wl = WINDOW + tq
    ws = pl.multiple_of(jnp.maximum(t0 - WINDOW, 0), tq)
    kw = kb_ref[pl.ds(ws, wl), 2 * LANES:4 * LANES]
    vw = kb_ref[pl.ds(ws, wl), 5 * LANES:6 * LANES]
    dw = (t0 + lax.broadcasted_iota(I32, (tq, wl), 0)) - (ws + lax.broadcasted_iota(I32, (tq, wl), 1))
    bias_w = jnp.where(dw >= 0, jnp.where(dw < WINDOW, 0.0, NEG_INF), NEG_INF)
    s3 = _dot_nt(q2e_ref[...], kw).reshape(NSA_HEADS, tq, wl)
    es, sums = [], []
    for h in range(NSA_HEADS):
        sh = s3[h] + bias_w
        e = jnp.exp(sh - jnp.max(sh, axis=-1, keepdims=True))
        es.append(e.astype(BF16))
        sums.append(jnp.broadcast_to(jnp.sum(e, axis=-1, keepdims=True), (tq, LANES)))
    o_win = _dot(jnp.concatenate(es, axis=0), vw) / jnp.concatenate(sums, axis=0)

    o_ref[...] = _combine_heads(g_ref[...], o_cmp, o_sel, o_win, tq)


def _nsa_prompt(u_q, u_gate, u_kv, nb, t):
    tq = NSA_TQ
    nt = t // tq
    wq = NSA_HEADS * HEAD_DIM
    in_specs = [
        pl.BlockSpec((tq, wq), lambda bb, i: (bb * nt + i, 0)),
        pl.BlockSpec((tq, LANES), lambda bb, i: (bb * nt + i, 0)),
        pl.BlockSpec((t, 6 * LANES), lambda bb, i: (bb, 0)),
    ]
    args = [u_q, u_gate, u_kv]
    rows_all = NSA_HEADS * tq
    return pl.pallas_call(
        functools.partial(_nsa_prompt_kernel, t=t, tq=tq),
        grid=(nb, nt),
        in_specs=in_specs,
        out_specs=pl.BlockSpec((tq, wq), lambda bb, i: (bb * nt + i, 0)),
        out_shape=jax.ShapeDtypeStruct((nb * t, wq), F32),
        scratch_shapes=[
            pltpu.VMEM((t, 6 * LANES), BF16),
            pltpu.VMEM((LANES, 2 * LANES), F32),
            pltpu.VMEM((LANES, t), BF16),
            pltpu.VMEM((NSA_KV, tq, LANES), BF16),
            pltpu.VMEM((rows_all, 2 * LANES), BF16),
            pltpu.VMEM((rows_all, LANES), F32),
            pltpu.VMEM((rows_all, LANES), F32),
            pltpu.VMEM((rows_all, LANES), F32),
        ],
        compiler_params=_cp(("parallel", "arbitrary")),
        name="nsa_prompt",
    )(*args)


def _nsa_sample_cmp_kernel(pt_ref, *refs, pp, tq, past_len):
    page_refs = refs[:pp]
    q_ref, ocmp_ref, score_ref, cmp_scr = refs[pp:]
    j = pl.program_id(1)
    n_cmp = past_len // BLK
    per_step = pp * (PAGE // BLK)
    rows = []
    for k in range(pp):
        pg_t = page_refs[k][...].T
        rows.append(pg_t.reshape(PAGE // BLK, BLK, 2 * LANES).sum(axis=1) * (1.0 / BLK))
    cmp_scr[pl.ds(pl.multiple_of(j * per_step, per_step), per_step), :] = jnp.concatenate(rows, axis=0)

    @pl.when(j == pl.num_programs(1) - 1)
    def _():
        q2 = _build_q2(q_ref[...], tq)
        kc = cmp_scr[...]
        ps = _cmp_probs(_dot_nt(q2, kc[:, :LANES].astype(BF16)), past_len, tq, n_cmp)
        ocmp_ref[0] = _dot(jnp.concatenate(ps, axis=0).astype(BF16), kc[:, LANES:].astype(BF16))
        scores = [_block_scores(ps, kv, past_len, tq, n_cmp)[0] for kv in range(NSA_KV)]
        score_ref[0] = jnp.concatenate(scores, axis=0)


def _topk_blocks_kernel(sc_ref, sel_ref, *, k):
    sc = sc_ref[...]
    n = sc.shape[1]
    idx = lax.broadcasted_iota(I32, sc.shape, 1).astype(F32)
    sel = jnp.zeros(sc.shape, F32)
    for _ in range(k):
        mx = jnp.max(sc, axis=-1, keepdims=True)
        am = jnp.min(jnp.where(sc == mx, idx, float(n)), axis=-1, keepdims=True)
        hit = idx == am
        sel = jnp.where(hit, 1.0, sel)
        sc = jnp.where(hit, -2.0, sc)
    sel_ref[...] = sel


def _topk_blocks(scores, k):
    return pl.pallas_call(
        functools.partial(_topk_blocks_kernel, k=k),
        out_shape=jax.ShapeDtypeStruct(scores.shape, F32),
        compiler_params=pltpu.CompilerParams(vmem_limit_bytes=VMEM_LIMIT),
        name="topk_blocks",
    )(scores)


def _nsa_sample_cmp(cache, layer, page_table, u_q, row0, nb, tq):
    n_pages = page_table.shape[1]
    past_len = n_pages * PAGE
    n_cmp = past_len // BLK
    pp = min(SAMPLE_PAGES_PER_STEP, n_pages)
    steps = n_pages // pp
    wq = NSA_HEADS * HEAD_DIM
    base = row0 // tq

    def page_spec(k):
        return pl.BlockSpec((None, None, 2 * LANES, PAGE),
                            lambda bb, j, pt: (layer, pt[bb, j * pp + k], 0, 0))

    grid_spec = pltpu.PrefetchScalarGridSpec(
        num_scalar_prefetch=1,
        grid=(nb, steps),
        in_specs=[page_spec(k) for k in range(pp)]
        + [pl.BlockSpec((tq, wq), lambda bb, j, pt: (base + bb, 0))],
        out_specs=[
            pl.BlockSpec((1, NSA_HEADS * tq, LANES), lambda bb, j, pt: (bb, 0, 0)),
            pl.BlockSpec((1, NSA_KV * tq, n_cmp), lambda bb, j, pt: (bb, 0, 0)),
        ],
        scratch_shapes=[pltpu.VMEM((n_cmp, 2 * LANES), F32)],
    )
    return pl.pallas_call(
        functools.partial(_nsa_sample_cmp_kernel, pp=pp, tq=tq, past_len=past_len),
        grid_spec=grid_spec,
        out_shape=[
            jax.ShapeDtypeStruct((nb, NSA_HEADS * tq, LANES), F32),
            jax.ShapeDtypeStruct((nb, NSA_KV * tq, n_cmp), F32),
        ],
        compiler_params=_cp(("parallel", "arbitrary")),
        name="nsa_sample_cmp",
    )(page_table, *([cache] * pp), u_q)


def _nsa_sample_sel_kernel(pages_ref, lpage_ref, cnt_ref, cache_ref, selm_ref, q_ref, g_ref, kvn_ref, ocmp_ref,
                           win_ref, o_ref, buf, sem, *, layer, tq, past_len):
    b = pl.program_id(0)
    n = cnt_ref[b]
    n_cmp = past_len // BLK
    rows_all = NSA_HEADS * tq

    n_pairs = (n + 1) // 2

    def page_index(pair, which):
        return jnp.minimum(2 * pair + which, n - 1)

    def page_copy(slot, pair, which):
        return pltpu.make_async_copy(
            cache_ref.at[layer, pages_ref[b, page_index(pair, which)], pl.ds(2 * LANES, 2 * LANES), :],
            buf.at[slot, which], sem.at[slot, which])

    @pl.when(n > 0)
    def _():
        page_copy(0, 0, 0).start()
        page_copy(0, 0, 1).start()

    q2 = _build_q2(q_ref[...], tq)
    selm = selm_ref[0]
    blk_idx = lax.broadcasted_iota(I32, selm.shape, 1)
    key = lax.broadcasted_iota(I32, (tq, LANES), 1)
    tpos = past_len + lax.broadcasted_iota(I32, (tq, LANES), 0)
    key16 = lax.broadcasted_iota(I32, (NSA_KV * tq, LANES), 1)

    def body(it, carry):
        m, l, acc = carry
        slot = it & 1

        @pl.when(it + 1 < n_pairs)
        def _():
            page_copy(1 - slot, it + 1, 0).start()
            page_copy(1 - slot, it + 1, 1).start()

        dists, mks, scores, vts = [], [], [], []
        for which in range(2):
            page_copy(slot, it, which).wait()
            pg = buf[slot, which]
            lp = lpage_ref[b, page_index(it, which)]
            real = (2 * it + which) < n
            dists.append((tpos - (lp * PAGE + key)).astype(F32))
            sel_lo = jnp.sum(jnp.where(blk_idx == 2 * lp, selm, 0.0), axis=-1, keepdims=True)
            sel_hi = jnp.sum(jnp.where(blk_idx == 2 * lp + 1, selm, 0.0), axis=-1, keepdims=True)
            mks.append((jnp.where(key16 < BLK, sel_lo, sel_hi) > 0.5) & real)
            scores.append(_dot(q2, pg[:LANES, :].astype(BF16)))
            vts.append(pg[LANES:, :].astype(BF16))
        dist = jnp.concatenate(dists, axis=1)
        mk16 = jnp.concatenate(mks, axis=1)
        masks = [mk16[kv * tq:(kv + 1) * tq] for kv in range(NSA_KV)]
        s3 = jnp.concatenate(scores, axis=1).reshape(NSA_HEADS, tq, 2 * LANES)
        pv = lambda p: _dot_nt(p[:, :LANES], vts[0]) + _dot_nt(p[:, LANES:], vts[1])
        return _online_step(s3, lambda h: masks[h // GQA], dist, pv, m, l, acc, tq)

    init = (jnp.full((rows_all, LANES), NEG_INF, F32), jnp.zeros((rows_all, LANES), F32),
            jnp.zeros((rows_all, LANES), F32))
    m, l, acc = lax.fori_loop(0, n_pairs, body, init)

    kvn = kvn_ref[...]
    zpad = jnp.zeros((LANES - tq, LANES), F32)
    k_new = jnp.concatenate([kvn[:, 2 * LANES:3 * LANES], zpad], axis=0).astype(BF16)
    v_new = jnp.concatenate([kvn[:, 3 * LANES:4 * LANES], zpad], axis=0).astype(BF16)
    pos = past_len + key
    mk_new = (key < tq) & (pos <= tpos)
    s3 = _dot_nt(q2, k_new).reshape(NSA_HEADS, tq, LANES)
    m, l, acc = _online_step(s3, lambda h: mk_new, (tpos - pos).astype(F32), lambda p: _dot(p, v_new),
                             m, l, acc, tq)
    o_sel = acc / jnp.maximum(l, TINY)

    win = win_ref[0]
    wl = WINDOW + LANES
    kw_new = jnp.concatenate([kvn[:, 4 * LANES:5 * LANES], zpad], axis=0).astype(BF16)
    vw_new = jnp.concatenate([kvn[:, 5 * LANES:6 * LANES], zpad], axis=0).astype(BF16)
    widx = lax.broadcasted_iota(I32, (tq, wl), 1)
    pw = past_len - WINDOW + widx
    dw = (past_len + lax.broadcasted_iota(I32, (tq, wl), 0)) - pw
    mw = (dw >= 0) & (dw < WINDOW) & (pw >= 0) & (widx < WINDOW + tq)
    s = jnp.concatenate([_dot(q2, win[:LANES, :].astype(BF16)), _dot_nt(q2, kw_new)], axis=1)
    pws = _masked_softmax_heads(s.reshape(NSA_HEADS, tq, wl), lambda h: mw, dw.astype(F32))
    p_win = jnp.concatenate(pws, axis=0).astype(BF16)
    o_win = _dot_nt(p_win[:, :WINDOW], win[LANES:, :].astype(BF16)) + _dot(p_win[:, WINDOW:], vw_new)

    o_ref[...] = _combine_heads(g_ref[...], ocmp_ref[0], o_sel, o_win, tq)


def _nsa_sample_sel(cache, layer, pages, lpages, cnt, selm, u_q, u_gate, u_kv, o_cmp, win_cache,
                    row0, nb, tq, past_len):
    wq = NSA_HEADS * HEAD_DIM
    base = row0 // tq
    n_cmp = past_len // BLK
    grid_spec = pltpu.PrefetchScalarGridSpec(
        num_scalar_prefetch=3,
        grid=(nb,),
        in_specs=[
            pl.BlockSpec(memory_space=pl.ANY),
            pl.BlockSpec((1, NSA_KV * tq, n_cmp), lambda bb, *_: (bb, 0, 0)),
            pl.BlockSpec((tq, wq), lambda bb, *_: (base + bb, 0)),
            pl.BlockSpec((tq, LANES), lambda bb, *_: (base + bb, 0)),
            pl.BlockSpec((tq, 6 * LANES), lambda bb, *_: (base + bb, 0)),
            pl.BlockSpec((1, NSA_HEADS * tq, LANES), lambda bb, *_: (bb, 0, 0)),
            pl.BlockSpec((None, 1, 2 * LANES, WINDOW), lambda bb, *_: (layer, bb, 0, 0)),
        ],
        out_specs=pl.BlockSpec((tq, wq), lambda bb, *_: (bb, 0)),
        scratch_shapes=[pltpu.VMEM((2, 2, 2 * LANES, PAGE), F32), pltpu.SemaphoreType.DMA((2, 2))],
    )
    return pl.pallas_call(
        functools.partial(_nsa_sample_sel_kernel, layer=layer, tq=tq, past_len=past_len),
        grid_spec=grid_spec,
        out_shape=jax.ShapeDtypeStruct((nb * tq, wq), F32),
        compiler_params=_cp(("arbitrary",)),
        name="nsa_sample_sel",
    )(pages, lpages, cnt, cache, selm, u_q, u_gate, u_kv, o_cmp, win_cache)


def _outproj_kernel(xp_ref, xs_ref, cp_ref, cs_ref, ap_ref, as_ref, rp_ref, rs_ref, wo_ref, g_ref, b_ref, wr_ref, br_ref,
                    h_ref, te_ref, tg_ref, cnt_ref, *, alpha, n_prompt_tiles):
    is_sample = pl.program_id(0) >= n_prompt_tiles
    pick = lambda p_ref, s_ref: jnp.where(is_sample, s_ref[...], p_ref[...]).astype(BF16)
    mix = (_dot(pick(cp_ref, cs_ref), wo_ref[0:CONV_CH, :])
           + _dot(pick(ap_ref, as_ref), wo_ref[CONV_CH:CONV_CH + NSA_HEADS * HEAD_DIM, :])
           + _dot(pick(rp_ref, rs_ref), wo_ref[CONV_CH + NSA_HEADS * HEAD_DIM:, :]))
    x = jnp.where(is_sample, xs_ref[...], xp_ref[...])
    h = _layer_norm(alpha * x + mix, g_ref[...], b_ref[...])
    h_ref[...] = h
    h_hi = h.astype(BF16)
    h_lo = (h - h_hi.astype(F32)).astype(BF16)
    r = _dot(h_hi, wr_ref[...]) + _dot(h_lo, wr_ref[...])
    logits = r[:, 0:N_EXPERTS] + r[:, N_EXPERTS:2 * N_EXPERTS] + br_ref[...]
    idx = lax.broadcasted_iota(I32, logits.shape, 1).astype(F32)
    col = lax.broadcasted_iota(I32, (logits.shape[0], TOP_K), 1)
    vals = jnp.zeros((logits.shape[0], TOP_K), F32)
    ids = jnp.zeros((logits.shape[0], TOP_K), F32)
    cur = logits
    hist = jnp.zeros((1, N_EXPERTS), F32)
    for k in range(TOP_K):
        mx = jnp.max(cur, axis=-1, keepdims=True)
        am = jnp.min(jnp.where(cur == mx, idx, float(N_EXPERTS)), axis=-1, keepdims=True)
        vals = jnp.where(col == k, mx, vals)
        ids = jnp.where(col == k, am, ids)
        hit = idx == am
        hist = hist + jnp.sum(jnp.where(hit, 1.0, 0.0), axis=0, keepdims=True)
        cur = jnp.where(hit, -jnp.inf, cur)
    e = jnp.exp(vals - vals[:, 0:1])
    tg_ref[...] = e / jnp.sum(e, axis=-1, keepdims=True)
    te_ref[...] = ids.astype(I32)

    @pl.when(pl.program_id(0) == 0)
    def _():
        cnt_ref[...] = jnp.zeros(cnt_ref.shape, F32)

    cnt_ref[...] = cnt_ref[...] + hist


def _outproj(x_p, x_s, mixers, w_out_bf, g, b, w_router, b_router, alpha):
    d = x_p.shape[1]
    n = x_p.shape[0] + x_s.shape[0]
    tm = TOKEN_TILE
    npt = mixers[0][0].shape[0] // tm
    row = lambda c: pl.BlockSpec((tm, c), lambda i: (i, 0))
    row_p = lambda c: pl.BlockSpec((tm, c), lambda i: (jnp.minimum(i, npt - 1), 0))
    row_s = lambda c: pl.BlockSpec((tm, c), lambda i: (jnp.maximum(i - npt, 0), 0))
    full = lambda r, c: pl.BlockSpec((r, c), lambda i: (0, 0))
    mix_specs, mix_args = [], []
    for a_p, a_s in mixers:
        mix_specs += [row_p(a_p.shape[1]), row_s(a_s.shape[1])]
        mix_args += [a_p, a_s]
    return pl.pallas_call(
        functools.partial(_outproj_kernel, alpha=alpha, n_prompt_tiles=npt),
        grid=(n // tm,),
        in_specs=[row_p(d), row_s(d)] + mix_specs + [
                  full(d, d), full(1, d), full(1, d), full(d, LANES), full(1, N_EXPERTS)],
        out_specs=[row(d), row(TOP_K), row(TOP_K), full(1, N_EXPERTS)],
        out_shape=[jax.ShapeDtypeStruct((n, d), F32),
                   jax.ShapeDtypeStruct((n, TOP_K), I32), jax.ShapeDtypeStruct((n, TOP_K), F32),
                   jax.ShapeDtypeStruct((1, N_EXPERTS), F32)],
        compiler_params=_cp(("arbitrary",)),
        name="outproj",
    )(x_p, x_s, *mix_args, w_out_bf, g, b, w_router, b_router)


def _moe_kernel(blk_ref, exp_ref, lo_ref, hi_ref, first_ref, x_ref, wup_ref, bg_ref, bu_ref, wd_ref, bd_ref, o_ref,
                wg_scr, wu_scr, wd_scr):
    i = pl.program_id(0)
    dff = wg_scr.shape[1]

    @pl.when((i == 0) | (exp_ref[i] != exp_ref[jnp.maximum(i - 1, 0)]))
    def _():
        w2 = 2 * LANES
        r = lax.broadcasted_iota(I32, (w2, w2), 0)
        c = lax.broadcasted_iota(I32, (w2, w2), 1)
        perm = jnp.where(r == jnp.where(c < LANES, 2 * c, 2 * (c - LANES) + 1), 1.0, 0.0).astype(BF16)
        for k in range(2 * dff // w2):
            split = _dot(wup_ref[:, k * w2:(k + 1) * w2].astype(BF16), perm).astype(BF16)
            wg_scr[:, k * LANES:(k + 1) * LANES] = split[:, :LANES]
            wu_scr[:, k * LANES:(k + 1) * LANES] = split[:, LANES:]
        wd_scr[...] = wd_ref[...].astype(BF16)

    lo, hi = lo_ref[i], hi_ref[i]

    @pl.when(hi > lo)
    def _():
        x = x_ref[...].astype(BF16)
        g = jnp.minimum(_dot(x, wg_scr[...]) + bg_ref[...], SWIGLU_LIMIT)
        u = jnp.clip(_dot(x, wu_scr[...]) + bu_ref[...], -SWIGLU_LIMIT, SWIGLU_LIMIT)
        act = (u + 1.0) * g * _sigmoid(SWIGLU_ALPHA * g)
        y = _dot(act.astype(BF16), wd_scr[...]) + bd_ref[...]
        row = lax.broadcasted_iota(I32, y.shape, 0)
        mine = (row >= lo) & (row < hi)

        @pl.when(first_ref[i] == 1)
        def _():
            o_ref[...] = jnp.where(mine, y, 0.0)

        @pl.when(first_ref[i] == 0)
        def _():
            o_ref[...] = jnp.where(mine, y, o_ref[...])


def _moe_blocks(xs, items, w_up, bg, bu, w_down, bd, layer):
    r, d = xs.shape
    bm = MOE_TILE
    dff = w_down.shape[2]
    n_items = items[0].shape[0]
    wspec = lambda a, c: pl.BlockSpec((None, None, a, c), lambda i, blk, exp, *_: (layer, exp[i], 0, 0))
    bspec = lambda c: pl.BlockSpec((None, 1, c), lambda i, blk, exp, *_: (exp[i], 0, 0))
    grid_spec = pltpu.PrefetchScalarGridSpec(
        num_scalar_prefetch=5,
        grid=(n_items,),
        in_specs=[
            pl.BlockSpec((bm, d), lambda i, blk, *_: (blk[i], 0)),
            wspec(d, 2 * dff), bspec(dff), bspec(dff), wspec(dff, d), bspec(d),
        ],
        out_specs=pl.BlockSpec((bm, d), lambda i, blk, *_: (blk[i], 0)),
        scratch_shapes=[pltpu.VMEM((d, dff), BF16), pltpu.VMEM((d, dff), BF16), pltpu.VMEM((dff, d), BF16)],
    )
    return pl.pallas_call(
        _moe_kernel,
        grid_spec=grid_spec,
        out_shape=jax.ShapeDtypeStruct((r, d), F32),
        compiler_params=_cp(("arbitrary",)),
        name="moe_experts",
    )(*items, xs, w_up, bg, bu, w_down, bd)


def _moe(h, top_e, counts, w_up, bg, bu, w_down, bd, layer):
    n, d = h.shape
    bm = MOE_TILE
    nk = n * TOP_K
    assert nk % bm == 0
    n_blk = nk // bm
    experts = jnp.arange(N_EXPERTS, dtype=I32)
    flat_e = top_e.T.reshape(nk)
    order = jnp.argsort(flat_e).astype(I32)
    rank = jnp.argsort(order).astype(I32)
    ends = jnp.cumsum(counts)
    starts = ends - counts
    has = ends > starts
    first_blk = starts // bm
    per_e = jnp.where(has, (ends - 1) // bm - first_blk + 1, 0)
    it_end = jnp.cumsum(per_e)
    it_start = it_end - per_e
    n_items = n_blk + N_EXPERTS - 1
    t = jnp.arange(n_items, dtype=I32)
    live = t < it_end[-1]
    e_t = jnp.minimum(jnp.sum((it_end[None, :] <= t[:, None]).astype(I32), axis=1), N_EXPERTS - 1)
    blk_t = first_blk[e_t] + t - it_start[e_t]
    lo = jnp.clip(starts[e_t] - blk_t * bm, 0, bm)
    hi = jnp.clip(ends[e_t] - blk_t * bm, 0, bm)
    e_last = jnp.max(jnp.where(has, experts, 0))
    blk_t = jnp.where(live, blk_t, n_blk - 1)
    e_t = jnp.where(live, e_t, e_last)
    lo = jnp.where(live, lo, 0)
    hi = jnp.where(live, hi, 0)
    first = jnp.concatenate([jnp.ones((1,), bool), blk_t[1:] != blk_t[:-1]]) & live
    items = tuple(a.astype(I32) for a in (blk_t, e_t, lo, hi, first))
    xs = jnp.take(h, order % n, axis=0, mode='clip')
    yb = _moe_blocks(xs, items, w_up, bg, bu, w_down, bd, layer)
    return jnp.take(yb, rank, axis=0, mode='clip')


def _final_kernel(h_ref, *refs, alpha, n_prompt_tiles):
    f_refs = refs[:TOP_K]
    tg_ref, pp_ref, ps_ref, wple_ref, wplg_ref, g_ref, b_ref, yp_ref, ys_ref = refs[TOP_K:]
    h = h_ref[...]
    tg = tg_ref[...]
    ffn = tg[:, 0:1] * f_refs[0][...]
    for k in range(1, TOP_K):
        ffn = ffn + tg[:, k:k + 1] * f_refs[k][...]
    is_sample = pl.program_id(0) >= n_prompt_tiles
    p = jnp.where(is_sample, ps_ref[...], pp_ref[...]).astype(BF16)
    ple = _dot(p, wple_ref[...]) * _sigmoid(_dot(h.astype(BF16), wplg_ref[...]))
    y = _layer_norm(alpha * h + ffn + ple, g_ref[...], b_ref[...])

    @pl.when(jnp.logical_not(is_sample))
    def _():
        yp_ref[...] = y

    @pl.when(is_sample)
    def _():
        ys_ref[...] = y


def _final(h, expert_out, gate, p_prompt, p_sample, layer, w_ple_bf, w_plg_bf, g, b, alpha):
    n, d = h.shape
    tm = TOKEN_TILE
    pd = p_prompt.shape[2]
    npt = p_prompt.shape[1] // tm
    row = lambda c: pl.BlockSpec((tm, c), lambda i: (i, 0))
    full = lambda r, c: pl.BlockSpec((r, c), lambda i: (0, 0))
    kth = lambda k: pl.BlockSpec((tm, d), lambda i: (k * (n // tm) + i, 0))
    p_specs = [pl.BlockSpec((None, tm, pd), lambda i: (layer, jnp.minimum(i, npt - 1), 0)),
               pl.BlockSpec((None, tm, pd), lambda i: (layer, jnp.maximum(i - npt, 0), 0))]
    return pl.pallas_call(
        functools.partial(_final_kernel, alpha=alpha, n_prompt_tiles=npt),
        grid=(n // tm,),
        in_specs=[row(d)] + [kth(k) for k in range(TOP_K)]
        + [row(TOP_K)] + p_specs + [full(pd, d), full(d, d), full(1, d), full(1, d)],
        out_specs=[pl.BlockSpec((tm, d), lambda i: (jnp.minimum(i, npt - 1), 0)),
                   pl.BlockSpec((tm, d), lambda i: (jnp.maximum(i - npt, 0), 0))],
        out_shape=[jax.ShapeDtypeStruct((npt * tm, d), F32), jax.ShapeDtypeStruct((n - npt * tm, d), F32)],
        compiler_params=_cp(("arbitrary",)),
        name="final",
    )(h, *([expert_out] * TOP_K), gate, p_prompt, p_sample, w_ple_bf, w_plg_bf, g, b)


def kernel(x_prompt, x_sample, p_prompt, p_sample, cache_nsa_kv, cache_win_kv, state_ret, state_conv, page_table,
           w_in, w_out, conv_w, conv_b, conv_ln_g, conv_ln_b, ret_norm_g, ln1_g, ln1_b, w_router, b_router,
           w_up, b_up, w_down, b_down, w_ple, w_plg, ln2_g, ln2_b):
    bp, seq, d = x_prompt.shape
    bs, tdec, _ = x_sample.shape
    depth = w_in.shape[0]
    n_pages = page_table.shape[1]
    past_len = n_pages * PAGE
    wbuf = cache_win_kv.shape[2]
    n_p, n_s = bp * seq, bs * tdec
    assert wbuf == WINDOW and tdec <= BLK and tdec % 8 == 0 and past_len // BLK >= N_SEL
    assert seq % (NSA_TQ * NSA_KEY_TILES) == 0 and seq >= WINDOW + NSA_TQ
    assert n_p % TOKEN_TILE == 0 and n_s % TOKEN_TILE == 0
    alpha = (2 * depth) ** 0.25
    n_pool = cache_nsa_kv.shape[1]
    cache = jnp.transpose(cache_nsa_kv, (0, 1, 3, 4, 5, 2)).reshape(depth, n_pool, 4 * LANES, PAGE)
    win_t = jnp.transpose(cache_win_kv, (0, 1, 3, 4, 5, 2)).reshape(depth, bs, 2 * LANES, wbuf)
    page_table = page_table.astype(I32)

    c_conv = 2 * CONV_CH
    c_q = NSA_HEADS * HEAD_DIM
    c_kv = 6 * NSA_KV * HEAD_DIM
    c_gate = 3 * NSA_HEADS
    cuts = [0, c_conv, c_conv + c_q, c_conv + c_q + c_kv, c_conv + c_q + c_kv + c_gate, w_in.shape[2]]

    x_p, x_s = x_prompt.reshape(n_p, d), x_sample.reshape(n_s, d)
    outs = {k: [] for k in ("kvp", "kvs", "wp", "ws", "rp", "rs", "cp", "cs")}
    zeros_ret = jnp.zeros((bp, RET_HEADS, HEAD_DIM, HEAD_DIM), F32)
    zeros_conv = jnp.zeros((bp, CONV_K - 1, CONV_CH), F32)

    for i in range(depth):
        wi = w_in[i].astype(BF16)
        ws = [wi[:, cuts[j]:cuts[j + 1]] for j in range(5)]
        ws[3] = jnp.pad(ws[3], ((0, 0), (0, LANES - c_gate)))
        u_conv, u_q, u_kv, u_gate, u_ret, paged_t, win_new_t = _inproj(x_p, x_s, ws, bp, seq)

        cw = jnp.pad(conv_w[i], ((0, 32 - CONV_K), (0, 0)))
        cargs = (cw, conv_b[i][None], conv_ln_g[i][None], conv_ln_b[i][None])
        conv_p, conv_sp = _conv(u_conv, zeros_conv, *cargs, 0, bp, seq)
        conv_s, conv_ss = _conv(u_conv, state_conv[i], *cargs, n_p, bs, tdec)

        rgain = ret_norm_g[i][None]
        ret_p, ret_sp = _retention(u_ret, zeros_ret, rgain, 0, bp, seq)
        ret_s, ret_ss = _retention(u_ret, state_ret[i], rgain, n_p, bs, tdec)

        nsa_p = _nsa_prompt(u_q, u_gate, u_kv, bp, seq)
        o_cmp, blk_score = _nsa_sample_cmp(cache, i, page_table, u_q, n_p, bs, tdec)
        selm = _topk_blocks(blk_score.reshape(bs * NSA_KV * tdec, -1), N_SEL - 1).reshape(blk_score.shape)
        need = selm.reshape(bs, NSA_KV * tdec, n_pages, PAGE // BLK).max(axis=(1, 3)) > 0.5
        npos = jnp.cumsum(need.astype(I32), axis=1) - 1
        cnt = npos[:, -1] + 1
        pidx = jnp.arange(n_pages, dtype=I32)
        hit = need[:, :, None] & (npos[:, :, None] == pidx[None, None, :])
        order = jnp.sum(jnp.where(hit, pidx[None, :, None], 0), axis=1).astype(I32)
        pages = jnp.take_along_axis(page_table, order, axis=1)
        nsa_s = _nsa_sample_sel(cache, i, pages, order, cnt, selm, u_q, u_gate, u_kv, o_cmp, win_t,
                                n_p, bs, tdec, past_len)

        wr_hi = w_router[i].astype(BF16)
        wr_lo = (w_router[i] - wr_hi.astype(F32)).astype(BF16)
        wr_cat = jnp.pad(jnp.concatenate([wr_hi, wr_lo], axis=1), ((0, 0), (0, LANES - 2 * N_EXPERTS)))
        h, top_e, gate, counts = _outproj(x_p, x_s, ((conv_p, conv_s), (nsa_p, nsa_s), (ret_p, ret_s)),
                                          w_out[i].astype(BF16),
                                          ln1_g[i][None], ln1_b[i][None], wr_cat, b_router[i][None], alpha)

        expert_out = _moe(h, top_e, counts[0].astype(I32), w_up, b_up[i][:, None, 0::2], b_up[i][:, None, 1::2],
                          w_down, b_down[i][:, None, :], i)

        x_p, x_s = _final(h, expert_out, gate, p_prompt.reshape(depth, n_p, -1), p_sample.reshape(depth, n_s, -1), i,
                       w_ple[i].astype(BF16), w_plg[i].astype(BF16),
                       ln2_g[i][None], ln2_b[i][None], alpha)

        kv_shape = (4, NSA_KV, HEAD_DIM)
        outs["kvp"].append(jnp.transpose(paged_t.reshape(bp, *kv_shape, seq), (0, 4, 1, 2, 3)))
        outs["kvs"].append(u_kv[n_p:, :4 * LANES].reshape(bs, tdec, *kv_shape))
        n_keep = min(WINDOW, seq)
        win_t_p = win_new_t[:, :, seq - n_keep:].reshape(bp, 2, NSA_KV, HEAD_DIM, n_keep)
        outs["wp"].append(jnp.transpose(win_t_p, (0, 4, 1, 2, 3)))
        win_new_s = u_kv[n_p:, 4 * LANES:].reshape(bs, tdec, 2, NSA_KV, HEAD_DIM)
        outs["ws"].append(jnp.concatenate([cache_win_kv[i][:, tdec:], win_new_s], axis=1))
        outs["rp"].append(ret_sp)
        outs["rs"].append(ret_ss)
        outs["cp"].append(conv_sp)
        outs["cs"].append(conv_ss)

    y_p = x_p.reshape(bp, seq, d)
    y_s = x_s.reshape(bs, tdec, d)
    st = lambda k: jnp.stack(outs[k])
    return (y_p, y_s, st("kvp"), st("kvs"), st("wp"), st("ws"), st("rp"), st("rs"), st("cp"), st("cs"))
```

```python
import functools

import jax
import jax.numpy as jnp
from jax import lax
from jax.experimental import pallas as pl
from jax.experimental.pallas import tpu as pltpu

F32 = jnp.float32
BF16 = jnp.bfloat16
I32 = jnp.int32

HEAD_DIM = 64
CONV_CH = 256
CONV_K = 31
NSA_HEADS = 8
NSA_KV = 2
GQA = 4
BLK = 64
N_SEL = 16
WINDOW = 512
FORCE_SCORE = 1e4
RET_HEADS = 4
RET_CHUNK = 128
N_EXPERTS = 32
TOP_K = 4
SWIGLU_LIMIT = 7.0
SWIGLU_ALPHA = 1.702
PAGE = 128
LN_EPS = 1e-5
NEG_INF = -1e30
TINY = 1e-30
LANES = 128

TOKEN_TILE = 256
MOE_TILE = 256
NSA_TQ = 128
NSA_KEY_TILES = 4
SAMPLE_PAGES_PER_STEP = 32
SEL_PAGES_PER_TRIP = 4
VMEM_LIMIT = 56 * 1024 * 1024


def _cp(sem):
    return pltpu.CompilerParams(dimension_semantics=sem, vmem_limit_bytes=VMEM_LIMIT)


def _dot(a, b):
    return jnp.dot(a, b, preferred_element_type=F32)


def _dot_nt(a, b):
    return lax.dot_general(a, b, (((1,), (1,)), ((), ())), preferred_element_type=F32)


def _dot_tn(a, b):
    return lax.dot_general(a, b, (((0,), (0,)), ((), ())), preferred_element_type=F32)


def _layer_norm(x, g, b):
    mu = jnp.mean(x, axis=-1, keepdims=True)
    xc = x - mu
    var = jnp.mean(xc * xc, axis=-1, keepdims=True)
    return xc * lax.rsqrt(var + LN_EPS) * g + b


def _sigmoid(x):
    return 1.0 / (1.0 + jnp.exp(-x))


def _inproj_kernel(xp_ref, xs_ref, wc_ref, wq_ref, wkv_ref, wg_ref, wr_ref, oc_ref, oq_ref, okv_ref, og_ref, or_ref,
                   opgt_ref, owint_ref, *, n_prompt_tiles):
    x = jnp.where(pl.program_id(0) >= n_prompt_tiles, xs_ref[...], xp_ref[...]).astype(BF16)
    oc_ref[...] = _dot(x, wc_ref[...])
    oq_ref[...] = _dot(x, wq_ref[...])
    ukv = _dot(x, wkv_ref[...])
    okv_ref[...] = ukv
    og_ref[...] = _dot(x, wg_ref[...])
    or_ref[...] = _dot(x, wr_ref[...])

    @pl.when(pl.program_id(0) < n_prompt_tiles)
    def _():
        ukv_t = ukv.T
        n_paged = opgt_ref.shape[0]
        opgt_ref[...] = ukv_t[:n_paged]
        owint_ref[...] = ukv_t[n_paged:]


def _inproj(x_p, x_s, ws, bp, seq):
    d = x_p.shape[1]
    n = x_p.shape[0] + x_s.shape[0]
    tm = TOKEN_TILE
    widths = [w.shape[1] for w in ws]
    c_kv = widths[2]
    tiles_per_seq = seq // tm
    npt = bp * tiles_per_seq
    c_paged = 4 * LANES
    kvt_spec = lambda c: pl.BlockSpec((None, c, tm), lambda i: (jnp.minimum(i, npt - 1) // tiles_per_seq, 0,
                                                                jnp.minimum(i, npt - 1) % tiles_per_seq))
    return pl.pallas_call(
        functools.partial(_inproj_kernel, n_prompt_tiles=npt),
        grid=(n // tm,),
        in_specs=[pl.BlockSpec((tm, d), lambda i: (jnp.minimum(i, npt - 1), 0)),
                  pl.BlockSpec((tm, d), lambda i: (jnp.maximum(i - npt, 0), 0))]
        + [pl.BlockSpec((d, c), lambda i: (0, 0)) for c in widths],
        out_specs=[pl.BlockSpec((tm, c), lambda i: (i, 0)) for c in widths]
        + [kvt_spec(c_paged), kvt_spec(c_kv - c_paged)],
        out_shape=[jax.ShapeDtypeStruct((n, c), F32) for c in widths]
        + [jax.ShapeDtypeStruct((bp, c_paged, seq), F32), jax.ShapeDtypeStruct((bp, c_kv - c_paged, seq), F32)],
        compiler_params=_cp(("arbitrary",)),
        name="inproj",
    )(x_p, x_s, *ws)


def _conv_kernel(u_ref, s0_ref, w_ref, cb_ref, g_ref, b_ref, o_ref, st_ref, ext, *, tt):
    i = pl.program_id(1)
    pad = 32 - (CONV_K - 1)

    @pl.when(i == 0)
    def _():
        ext[0:pad, :] = jnp.zeros((pad, CONV_CH), F32)
        ext[pad:32, :] = s0_ref[0]

    u = u_ref[...]
    h = u[:, :CONV_CH] * _sigmoid(u[:, CONV_CH:])
    ext[32:32 + tt, :] = h
    acc = jnp.zeros((tt, CONV_CH), F32)
    for j in range(CONV_K):
        acc = acc + ext[pad + j:pad + j + tt, :] * w_ref[j:j + 1, :]
    y = _layer_norm(acc + cb_ref[...], g_ref[...], b_ref[...])
    o_ref[...] = y * _sigmoid(y)
    tail = ext[tt + pad:tt + 32, :]

    @pl.when(i == pl.num_programs(1) - 1)
    def _():
        st_ref[0] = tail

    ext[pad:32, :] = tail


def _conv(u_conv, s0, w, cb, g, b, row0, nb, t):
    tt = min(t, 256)
    nt = t // tt
    base = row0 // tt
    in_specs = [
        pl.BlockSpec((tt, 2 * CONV_CH), lambda bb, i: (base + bb * nt + i, 0)),
        pl.BlockSpec((1, CONV_K - 1, CONV_CH), lambda bb, i: (bb, 0, 0)),
        pl.BlockSpec((32, CONV_CH), lambda bb, i: (0, 0)),
        pl.BlockSpec((1, CONV_CH), lambda bb, i: (0, 0)),
        pl.BlockSpec((1, CONV_CH), lambda bb, i: (0, 0)),
        pl.BlockSpec((1, CONV_CH), lambda bb, i: (0, 0)),
    ]
    args = [u_conv, s0, w, cb, g, b]
    return pl.pallas_call(
        functools.partial(_conv_kernel, tt=tt),
        grid=(nb, nt),
        in_specs=in_specs,
        out_specs=[
            pl.BlockSpec((tt, CONV_CH), lambda bb, i: (bb * nt + i, 0)),
            pl.BlockSpec((1, CONV_K - 1, CONV_CH), lambda bb, i: (bb, 0, 0)),
        ],
        out_shape=[
            jax.ShapeDtypeStruct((nb * t, CONV_CH), F32),
            jax.ShapeDtypeStruct((nb, CONV_K - 1, CONV_CH), F32),
        ],
        scratch_shapes=[pltpu.VMEM((32 + tt, CONV_CH), F32)],
        compiler_params=_cp(("parallel", "arbitrary")),
        name="conv",
    )(*args)


def _ret_kernel(u_ref, s0_ref, dm_ref, qd_ref, kd_ref, cd_ref, g_ref, o_ref, st_ref, s_scr):
    i = pl.program_id(1)

    @pl.when(i == 0)
    def _():
        s_scr[...] = s0_ref[0]

    u = u_ref[...]
    w = RET_HEADS * HEAD_DIM
    outs = []
    for h in range(RET_HEADS):
        lo = h * HEAD_DIM
        q = u[:, lo:lo + HEAD_DIM]
        k = u[:, w + lo:w + lo + HEAD_DIM] * (HEAD_DIM ** -0.5)
        v = u[:, 2 * w + lo:2 * w + lo + HEAD_DIM]
        rg = u[:, 3 * w + lo:3 * w + lo + HEAD_DIM]
        qb, kb, vb = q.astype(BF16), k.astype(BF16), v.astype(BF16)
        att = _dot_nt(qb, kb) * dm_ref[h]
        s_h = s_scr[h]
        o = _dot(att.astype(BF16), vb) + _dot(qb, s_h.astype(BF16)) * qd_ref[h]
        kdec = (k * kd_ref[h]).astype(BF16)
        s_scr[h] = s_h * cd_ref[h] + _dot_tn(kdec, vb)
        mu = jnp.mean(o, axis=-1, keepdims=True)
        oc = o - mu
        var = jnp.mean(oc * oc, axis=-1, keepdims=True)
        on = oc * lax.rsqrt(var + LN_EPS) * g_ref[:, lo:lo + HEAD_DIM]
        outs.append(rg * _sigmoid(rg) * on)
    o_ref[...] = jnp.concatenate(outs, axis=1)

    @pl.when(i == pl.num_programs(1) - 1)
    def _():
        st_ref[0] = s_scr[...]


def _ret_tables(c):
    log_g = jnp.log1p(-jnp.exp2(-5.0 - jnp.arange(RET_HEADS, dtype=F32)))
    i = jnp.arange(c, dtype=F32)
    diff = i[:, None] - i[None, :]
    dmask = jnp.exp(jnp.where(diff >= 0, log_g[:, None, None] * diff, -jnp.inf))
    q_dec = jnp.exp(log_g[:, None] * (i[None, :] + 1.0))
    k_dec = jnp.exp(log_g[:, None] * (c - 1.0 - i[None, :]))
    c_dec = jnp.exp(log_g * c)
    bc = lambda a: jnp.broadcast_to(a[:, :, None], a.shape + (HEAD_DIM,))
    cd = jnp.broadcast_to(c_dec[:, None, None], (RET_HEADS, HEAD_DIM, HEAD_DIM))
    return dmask, bc(q_dec), bc(k_dec), cd


def _retention(u_ret, s0, gain, row0, nb, t):
    c = min(t, RET_CHUNK)
    while t % c:
        c -= 1
    nt = t // c
    base = row0 // c
    dm, qd, kd, cd = _ret_tables(c)
    w = RET_HEADS * HEAD_DIM
    full = lambda shape: pl.BlockSpec(shape, lambda bb, i: (0,) * len(shape))
    in_specs = [
        pl.BlockSpec((c, 4 * w), lambda bb, i: (base + bb * nt + i, 0)),
        pl.BlockSpec((1, RET_HEADS, HEAD_DIM, HEAD_DIM), lambda bb, i: (bb, 0, 0, 0)),
        full((RET_HEADS, c, c)),
        full((RET_HEADS, c, HEAD_DIM)),
        full((RET_HEADS, c, HEAD_DIM)),
        full((RET_HEADS, HEAD_DIM, HEAD_DIM)),
        full((1, w)),
    ]
    args = [u_ret, s0, dm, qd, kd, cd, gain]
    return pl.pallas_call(
        _ret_kernel,
        grid=(nb, nt),
        in_specs=in_specs,
        out_specs=[
            pl.BlockSpec((c, w), lambda bb, i: (bb * nt + i, 0)),
            pl.BlockSpec((1, RET_HEADS, HEAD_DIM, HEAD_DIM), lambda bb, i: (bb, 0, 0, 0)),
        ],
        out_shape=[
            jax.ShapeDtypeStruct((nb * t, w), F32),
            jax.ShapeDtypeStruct((nb, RET_HEADS, HEAD_DIM, HEAD_DIM), F32),
        ],
        scratch_shapes=[pltpu.VMEM((RET_HEADS, HEAD_DIM, HEAD_DIM), F32)],
        compiler_params=_cp(("parallel", "arbitrary")),
        name="retention",
    )(*args)


def _slope(h):
    return 2.0 ** -(h + 1)


def _build_q2(uq, tq):
    lane = lax.broadcasted_iota(I32, (tq, LANES), 1)
    parts = []
    for h in range(NSA_HEADS):
        kvh = h // GQA
        p = uq[:, (h // 2) * LANES:(h // 2 + 1) * LANES]
        if (h % 2) != kvh:
            p = pltpu.roll(p, HEAD_DIM, 1)
        keep = (lane >= HEAD_DIM) if kvh == 1 else (lane < HEAD_DIM)
        parts.append(jnp.where(keep, p * (HEAD_DIM ** -0.5), 0.0))
    return jnp.concatenate(parts, axis=0).astype(BF16)


def _masked_softmax_heads(s3, mask_of_head, dist):
    ps = []
    for h in range(NSA_HEADS):
        mk = mask_of_head(h)
        sh = jnp.where(mk, s3[h] - _slope(h) * dist, NEG_INF)
        mx = jnp.max(sh, axis=-1, keepdims=True)
        e = jnp.where(mk, jnp.exp(sh - mx), 0.0)
        den = jnp.maximum(jnp.sum(e, axis=-1, keepdims=True), TINY)
        ps.append(e / den)
    return ps


def _online_step(s3, mask_of_head, dist, pv, m, l, acc, tq):
    ps, alphas, ms, ls = [], [], [], []
    for h in range(NSA_HEADS):
        mk = mask_of_head(h)
        rows = slice(h * tq, (h + 1) * tq)
        sh = jnp.where(mk, s3[h] - _slope(h) * dist, NEG_INF)
        m_old = m[rows]
        m_new = jnp.maximum(m_old, jnp.max(sh, axis=-1, keepdims=True))
        p = jnp.where(mk, jnp.exp(sh - jnp.concatenate([m_new] * (sh.shape[1] // LANES), axis=1)), 0.0)
        alpha = jnp.exp(m_old - m_new)
        ls.append(alpha * l[rows] + jnp.sum(p, axis=-1, keepdims=True))
        ms.append(m_new)
        alphas.append(alpha)
        ps.append(p)
    p_all = jnp.concatenate(ps, axis=0).astype(BF16)
    alpha_all = jnp.concatenate(alphas, axis=0)
    acc = acc * alpha_all + pv(p_all)
    return jnp.concatenate(ms, axis=0), jnp.concatenate(ls, axis=0), acc


def _alibi_query_lanes(tq):
    lane = lax.broadcasted_iota(I32, (tq, LANES), 1)
    parts = [jnp.where(lane == 0, _slope(h), jnp.where(lane == 1, _slope(h) * LANES, 0.0))
             for h in range(NSA_HEADS)]
    return jnp.concatenate(parts, axis=0).astype(BF16)


def _alibi_key_lanes(pos):
    lane = lax.broadcasted_iota(I32, pos.shape, 1)
    return jnp.where(lane == 0, pos & (LANES - 1), jnp.where(lane == 1, pos >> 7, 0)).astype(F32).astype(BF16)


def _combine_heads(gl, o_cmp, o_sel, o_win, tq):
    gs = _sigmoid(gl)
    lane = lax.broadcasted_iota(I32, (tq, LANES), 1)
    pairs = []
    for mpair in range(NSA_HEADS // 2):
        halves = []
        for h in (2 * mpair, 2 * mpair + 1):
            rows = slice(h * tq, (h + 1) * tq)
            o = (gs[:, 3 * h:3 * h + 1] * o_cmp[rows] + gs[:, 3 * h + 1:3 * h + 2] * o_sel[rows]
                 + gs[:, 3 * h + 2:3 * h + 3] * o_win[rows])
            if (h // GQA) != (h % 2):
                o = pltpu.roll(o, HEAD_DIM, 1)
            halves.append(o)
        pairs.append(jnp.where(lane < HEAD_DIM, halves[0], halves[1]))
    return jnp.concatenate(pairs, axis=1)


def _cmp_probs(s, t0, tq, n_lanes):
    s3 = s.reshape(NSA_HEADS, tq, n_lanes)
    tpos = t0 + lax.broadcasted_iota(I32, (tq, n_lanes), 0)
    cend = lax.broadcasted_iota(I32, (tq, n_lanes), 1) * BLK + (BLK - 1)
    valid = cend <= tpos
    dist = (tpos - cend).astype(F32)
    return _masked_softmax_heads(s3, lambda h: valid, dist)


def _block_scores(ps, kv, t0, tq, n_lanes):
    tpos = t0 + lax.broadcasted_iota(I32, (tq, n_lanes), 0)
    nb = lax.broadcasted_iota(I32, (tq, n_lanes), 1)
    cur = tpos >> 6
    imp = ps[kv * GQA] + ps[kv * GQA + 1] + ps[kv * GQA + 2] + ps[kv * GQA + 3]
    forced = (nb == 0) | (nb == cur) | (nb == cur - 1)
    allowed = nb <= cur
    return jnp.where(allowed, jnp.where(forced, FORCE_SCORE, imp), -1.0), allowed, nb


def _nsa_prompt_kernel(q_ref, g_ref, kv_ref, o_ref, kb_ref, kc_ref, e_ref, sel_ref, q2e_ref, m_ref, l_ref, acc_ref,
                       *, t, tq):
    i = pl.program_id(1)
    n_blk = t // BLK
    rows_all = NSA_HEADS * tq

    @pl.when(i == 0)
    def _():
        kc_ref[...] = jnp.zeros(kc_ref.shape, F32)
        step = 256
        for c in range(t // step):
            sl = slice(c * step, (c + 1) * step)
            pos_lanes = _alibi_key_lanes(c * step + lax.broadcasted_iota(I32, (step, LANES), 0))
            kb_ref[sl, 0:LANES] = kv_ref[sl, 2 * LANES:3 * LANES].astype(BF16)
            kb_ref[sl, LANES:2 * LANES] = pos_lanes
            kb_ref[sl, 2 * LANES:3 * LANES] = kv_ref[sl, 4 * LANES:5 * LANES].astype(BF16)
            kb_ref[sl, 3 * LANES:4 * LANES] = pos_lanes
            kb_ref[sl, 4 * LANES:5 * LANES] = kv_ref[sl, 3 * LANES:4 * LANES].astype(BF16)
            kb_ref[sl, 5 * LANES:6 * LANES] = kv_ref[sl, 5 * LANES:6 * LANES].astype(BF16)
            kc_ref[c * (step // BLK):(c + 1) * (step // BLK), :] = (
                kv_ref[sl, 0:2 * LANES].reshape(step // BLK, BLK, 2 * LANES).sum(axis=1) * (1.0 / BLK))
        blk_of_key = lax.broadcasted_iota(I32, (LANES, t), 1) >> 6
        e_ref[...] = jnp.where(blk_of_key == lax.broadcasted_iota(I32, (LANES, t), 0), 1.0, 0.0).astype(BF16)

    t0 = i * tq
    q2 = _build_q2(q_ref[...], tq)

    kc = kc_ref[0:n_blk, :]
    s_t = _dot_nt(kc[:, :LANES].astype(BF16), q2)
    blk = lax.broadcasted_iota(I32, (n_blk, tq), 0)
    tpos_c = t0 + lax.broadcasted_iota(I32, (n_blk, tq), 1)
    cend = blk * BLK + (BLK - 1)
    valid = cend <= tpos_c
    dist_c = (tpos_c - cend).astype(F32)
    p_t = []
    for h in range(NSA_HEADS):
        sh = jnp.where(valid, s_t[:, h * tq:(h + 1) * tq] - _slope(h) * dist_c, NEG_INF)
        e = jnp.where(valid, jnp.exp(sh - jnp.max(sh, axis=0, keepdims=True)), 0.0)
        p_t.append(e / jnp.maximum(jnp.sum(e, axis=0, keepdims=True), TINY))
    o_cmp = _dot_tn(jnp.concatenate(p_t, axis=1).astype(BF16), kc[:, LANES:].astype(BF16))

    cur = tpos_c >> 6
    allowed = blk <= cur
    forced = (blk == 0) | (blk == cur) | (blk == cur - 1)
    eye = jnp.where(lax.broadcasted_iota(I32, (n_blk, LANES), 0) == lax.broadcasted_iota(I32, (n_blk, LANES), 1),
                    1.0, 0.0).astype(BF16)
    for kv in range(NSA_KV):
        imp = p_t[kv * GQA] + p_t[kv * GQA + 1] + p_t[kv * GQA + 2] + p_t[kv * GQA + 3]
        score = jnp.where(allowed, jnp.where(forced, FORCE_SCORE, imp), -1.0)
        rank = jnp.zeros((n_blk, tq), F32)
        for mblk in range(n_blk):
            c = score[mblk:mblk + 1, :]
            tie = jnp.where(blk > mblk, 1.0, 0.0)
            rank = rank + jnp.where(c > score, 1.0, jnp.where(c == score, tie, 0.0))
        sel_t = jnp.where(allowed, jnp.where(rank < N_SEL, 1.0, 0.0), 0.0).astype(BF16)
        sel_ref[kv] = _dot_tn(sel_t, eye).astype(BF16)

    q2e_ref[...] = jnp.concatenate([q2, _alibi_query_lanes(tq)], axis=1)

    m_ref[...] = jnp.full((rows_all, LANES), NEG_INF, F32)
    l_ref[...] = jnp.zeros((rows_all, LANES), F32)
    acc_ref[...] = jnp.zeros((rows_all, LANES), F32)
    tk = NSA_KEY_TILES * tq
    tpos = t0 + lax.broadcasted_iota(I32, (tq, tk), 0)

    def body(kt, carry):
        k0 = pl.multiple_of(kt * tk, tk)
        k_bf = kb_ref[pl.ds(k0, tk), 0:2 * LANES]
        v_bf = kb_ref[pl.ds(k0, tk), 4 * LANES:5 * LANES]
        causal = (k0 + lax.broadcasted_iota(I32, (tq, tk), 1)) <= tpos
        e_tile = e_ref[:, pl.ds(k0, tk)]
        biases = [jnp.where(causal, jnp.where(_dot(sel_ref[kv], e_tile) > 0.5, 0.0, NEG_INF), NEG_INF)
                  for kv in range(NSA_KV)]
        s3 = _dot_nt(q2e_ref[...], k_bf).reshape(NSA_HEADS, tq, tk)
        m_all, l_all = m_ref[...], l_ref[...]
        ps, alphas, ms, ls = [], [], [], []
        for h in range(NSA_HEADS):
            rows = slice(h * tq, (h + 1) * tq)
            sh = s3[h] + biases[h // GQA]
            m_old = m_all[rows]
            m_new = jnp.maximum(m_old, jnp.max(sh, axis=-1, keepdims=True))
            p = jnp.exp(sh - jnp.concatenate([m_new] * NSA_KEY_TILES, axis=1))
            alpha = jnp.exp(m_old - m_new)
            p_lanes = p[:, :LANES]
            for c in range(1, NSA_KEY_TILES):
                p_lanes = p_lanes + p[:, c * LANES:(c + 1) * LANES]
            ls.append(alpha * l_all[rows] + p_lanes)
            ms.append(m_new)
            alphas.append(alpha)
            ps.append(p.astype(BF16))
        acc_ref[...] = acc_ref[...] * jnp.concatenate(alphas, axis=0) + _dot(jnp.concatenate(ps, axis=0), v_bf)
        m_ref[...] = jnp.concatenate(ms, axis=0)
        l_ref[...] = jnp.concatenate(ls, axis=0)
        return carry

    lax.fori_loop(0, (i + NSA_KEY_TILES) // NSA_KEY_TILES, body, 0)
    o_sel = acc_ref[...] / jnp.sum(l_ref[...], axis=-1, keepdims=True)

    wl = WINDOW + tq
    ws = pl.multiple_of(jnp.maximum(t0 - WINDOW, 0), tq)
    kw = kb_ref[pl.ds(ws, wl), 2 * LANES:4 * LANES]
    vw = kb_ref[pl.ds(ws, wl), 5 * LANES:6 * LANES]
    dw = (t0 + lax.broadcasted_iota(I32, (tq, wl), 0)) - (ws + lax.broadcasted_iota(I32, (tq, wl), 1))
    bias_w = jnp.where(dw >= 0, jnp.where(dw < WINDOW, 0.0, NEG_INF), NEG_INF)
    s3 = _dot_nt(q2e_ref[...], kw).reshape(NSA_HEADS, tq, wl)
    es, sums = [], []
    for h in range(NSA_HEADS):
        sh = s3[h] + bias_w
        e = jnp.exp(sh - jnp.max(sh, axis=-1, keepdims=True))
        es.append(e.astype(BF16))
        sums.append(jnp.broadcast_to(jnp.sum(e, axis=-1, keepdims=True), (tq, LANES)))
    o_win = _dot(jnp.concatenate(es, axis=0), vw) / jnp.concatenate(sums, axis=0)

    o_ref[...] = _combine_heads(g_ref[...], o_cmp, o_sel, o_win, tq)


def _nsa_prompt(u_q, u_gate, u_kv, nb, t):
    tq = NSA_TQ
    nt = t // tq
    wq = NSA_HEADS * HEAD_DIM
    in_specs = [
        pl.BlockSpec((tq, wq), lambda bb, i: (bb * nt + i, 0)),
        pl.BlockSpec((tq, LANES), lambda bb, i: (bb * nt + i, 0)),
        pl.BlockSpec((t, 6 * LANES), lambda bb, i: (bb, 0)),
    ]
    args = [u_q, u_gate, u_kv]
    rows_all = NSA_HEADS * tq
    return pl.pallas_call(
        functools.partial(_nsa_prompt_kernel, t=t, tq=tq),
        grid=(nb, nt),
        in_specs=in_specs,
        out_specs=pl.BlockSpec((tq, wq), lambda bb, i: (bb * nt + i, 0)),
        out_shape=jax.ShapeDtypeStruct((nb * t, wq), F32),
        scratch_shapes=[
            pltpu.VMEM((t, 6 * LANES), BF16),
            pltpu.VMEM((LANES, 2 * LANES), F32),
            pltpu.VMEM((LANES, t), BF16),
            pltpu.VMEM((NSA_KV, tq, LANES), BF16),
            pltpu.VMEM((rows_all, 2 * LANES), BF16),
            pltpu.VMEM((rows_all, LANES), F32),
            pltpu.VMEM((rows_all, LANES), F32),
            pltpu.VMEM((rows_all, LANES), F32),
        ],
        compiler_params=_cp(("parallel", "arbitrary")),
        name="nsa_prompt",
    )(*args)


def _nsa_sample_cmp_kernel(pt_ref, *refs, pp, tq, past_len):
    page_refs = refs[:pp]
    q_ref, ocmp_ref, score_ref, cmp_scr = refs[pp:]
    j = pl.program_id(1)
    n_cmp = past_len // BLK
    per_step = pp * (PAGE // BLK)
    rows = []
    for k in range(pp):
        pg_t = page_refs[k][...].T
        rows.append(pg_t.reshape(PAGE // BLK, BLK, 2 * LANES).sum(axis=1) * (1.0 / BLK))
    cmp_scr[pl.ds(pl.multiple_of(j * per_step, per_step), per_step), :] = jnp.concatenate(rows, axis=0)

    @pl.when(j == pl.num_programs(1) - 1)
    def _():
        q2 = _build_q2(q_ref[...], tq)
        kc = cmp_scr[...]
        ps = _cmp_probs(_dot_nt(q2, kc[:, :LANES].astype(BF16)), past_len, tq, n_cmp)
        ocmp_ref[0] = _dot(jnp.concatenate(ps, axis=0).astype(BF16), kc[:, LANES:].astype(BF16))
        scores = [_block_scores(ps, kv, past_len, tq, n_cmp)[0] for kv in range(NSA_KV)]
        score_ref[0] = jnp.concatenate(scores, axis=0)


def _topk_blocks_kernel(sc_ref, sel_ref, *, k):
    sc = sc_ref[...]
    n = sc.shape[1]
    idx = lax.broadcasted_iota(I32, sc.shape, 1).astype(F32)
    sel = jnp.zeros(sc.shape, F32)
    for _ in range(k):
        mx = jnp.max(sc, axis=-1, keepdims=True)
        am = jnp.min(jnp.where(sc == mx, idx, float(n)), axis=-1, keepdims=True)
        hit = idx == am
        sel = jnp.where(hit, 1.0, sel)
        sc = jnp.where(hit, -2.0, sc)
    sel_ref[...] = sel


def _topk_blocks(scores, k):
    return pl.pallas_call(
        functools.partial(_topk_blocks_kernel, k=k),
        out_shape=jax.ShapeDtypeStruct(scores.shape, F32),
        compiler_params=pltpu.CompilerParams(vmem_limit_bytes=VMEM_LIMIT),
        name="topk_blocks",
    )(scores)


def _nsa_sample_cmp(cache, layer, page_table, u_q, row0, nb, tq):
    n_pages = page_table.shape[1]
    past_len = n_pages * PAGE
    n_cmp = past_len // BLK
    pp = min(SAMPLE_PAGES_PER_STEP, n_pages)
    steps = n_pages // pp
    wq = NSA_HEADS * HEAD_DIM
    base = row0 // tq

    def page_spec(k):
        return pl.BlockSpec((None, None, 2 * LANES, PAGE),
                            lambda bb, j, pt: (layer, pt[bb, j * pp + k], 0, 0))

    grid_spec = pltpu.PrefetchScalarGridSpec(
        num_scalar_prefetch=1,
        grid=(nb, steps),
        in_specs=[page_spec(k) for k in range(pp)]
        + [pl.BlockSpec((tq, wq), lambda bb, j, pt: (base + bb, 0))],
        out_specs=[
            pl.BlockSpec((1, NSA_HEADS * tq, LANES), lambda bb, j, pt: (bb, 0, 0)),
            pl.BlockSpec((1, NSA_KV * tq, n_cmp), lambda bb, j, pt: (bb, 0, 0)),
        ],
        scratch_shapes=[pltpu.VMEM((n_cmp, 2 * LANES), F32)],
    )
    return pl.pallas_call(
        functools.partial(_nsa_sample_cmp_kernel, pp=pp, tq=tq, past_len=past_len),
        grid_spec=grid_spec,
        out_shape=[
            jax.ShapeDtypeStruct((nb, NSA_HEADS * tq, LANES), F32),
            jax.ShapeDtypeStruct((nb, NSA_KV * tq, n_cmp), F32),
        ],
        compiler_params=_cp(("parallel", "arbitrary")),
        name="nsa_sample_cmp",
    )(page_table, *([cache] * pp), u_q)


def _nsa_sample_sel_kernel(pages_ref, lpage_ref, cnt_ref, cache_ref, selm_ref, q_ref, g_ref, kvn_ref, ocmp_ref,
                           win_ref, o_ref, buf, sem, *, layer, tq, past_len):
    b = pl.program_id(0)
    n = cnt_ref[b]
    n_cmp = past_len // BLK
    rows_all = NSA_HEADS * tq

    ppt = SEL_PAGES_PER_TRIP
    n_trips = (n + ppt - 1) // ppt

    def page_index(trip, which):
        return jnp.minimum(ppt * trip + which, n - 1)

    def page_copy(slot, trip, which):
        return pltpu.make_async_copy(
            cache_ref.at[layer, pages_ref[b, page_index(trip, which)], pl.ds(2 * LANES, 2 * LANES), :],
            buf.at[slot, which], sem.at[slot, which])

    @pl.when(n > 0)
    def _():
        for which in range(ppt):
            page_copy(0, 0, which).start()

    q2 = _build_q2(q_ref[...], tq)
    selm = selm_ref[0]
    blk_idx = lax.broadcasted_iota(I32, selm.shape, 1)
    key = lax.broadcasted_iota(I32, (tq, LANES), 1)
    tpos = past_len + lax.broadcasted_iota(I32, (tq, LANES), 0)
    key16 = lax.broadcasted_iota(I32, (NSA_KV * tq, LANES), 1)

    def body(it, carry):
        m, l, acc = carry
        slot = it & 1

        @pl.when(it + 1 < n_trips)
        def _():
            for which in range(ppt):
                page_copy(1 - slot, it + 1, which).start()

        dists, mks, scores, vts = [], [], [], []
        for which in range(ppt):
            page_copy(slot, it, which).wait()
            pg = buf[slot, which]
            lp = lpage_ref[b, page_index(it, which)]
            real = (ppt * it + which) < n
            dists.append((tpos - (lp * PAGE + key)).astype(F32))
            sel_lo = jnp.sum(jnp.where(blk_idx == 2 * lp, selm, 0.0), axis=-1, keepdims=True)
            sel_hi = jnp.sum(jnp.where(blk_idx == 2 * lp + 1, selm, 0.0), axis=-1, keepdims=True)
            mks.append((jnp.where(key16 < BLK, sel_lo, sel_hi) > 0.5) & real)
            scores.append(_dot(q2, pg[:LANES, :].astype(BF16)))
            vts.append(pg[LANES:, :].astype(BF16))
        dist = jnp.concatenate(dists, axis=1)
        mk16 = jnp.concatenate(mks, axis=1)
        masks = [mk16[kv * tq:(kv + 1) * tq] for kv in range(NSA_KV)]
        s3 = jnp.concatenate(scores, axis=1).reshape(NSA_HEADS, tq, ppt * LANES)

        def pv(p):
            out = _dot_nt(p[:, :LANES], vts[0])
            for which in range(1, ppt):
                out = out + _dot_nt(p[:, which * LANES:(which + 1) * LANES], vts[which])
            return out

        return _online_step(s3, lambda h: masks[h // GQA], dist, pv, m, l, acc, tq)

    init = (jnp.full((rows_all, LANES), NEG_INF, F32), jnp.zeros((rows_all, LANES), F32),
            jnp.zeros((rows_all, LANES), F32))
    m, l, acc = lax.fori_loop(0, n_trips, body, init)

    kvn = kvn_ref[...]
    zpad = jnp.zeros((LANES - tq, LANES), F32)
    k_new = jnp.concatenate([kvn[:, 2 * LANES:3 * LANES], zpad], axis=0).astype(BF16)
    v_new = jnp.concatenate([kvn[:, 3 * LANES:4 * LANES], zpad], axis=0).astype(BF16)
    pos = past_len + key
    mk_new = (key < tq) & (pos <= tpos)
    s3 = _dot_nt(q2, k_new).reshape(NSA_HEADS, tq, LANES)
    m, l, acc = _online_step(s3, lambda h: mk_new, (tpos - pos).astype(F32), lambda p: _dot(p, v_new),
                             m, l, acc, tq)
    o_sel = acc / jnp.maximum(l, TINY)

    win = win_ref[0]
    wl = WINDOW + LANES
    kw_new = jnp.concatenate([kvn[:, 4 * LANES:5 * LANES], zpad], axis=0).astype(BF16)
    vw_new = jnp.concatenate([kvn[:, 5 * LANES:6 * LANES], zpad], axis=0).astype(BF16)
    widx = lax.broadcasted_iota(I32, (tq, wl), 1)
    pw = past_len - WINDOW + widx
    dw = (past_len + lax.broadcasted_iota(I32, (tq, wl), 0)) - pw
    mw = (dw >= 0) & (dw < WINDOW) & (pw >= 0) & (widx < WINDOW + tq)
    s = jnp.concatenate([_dot(q2, win[:LANES, :].astype(BF16)), _dot_nt(q2, kw_new)], axis=1)
    pws = _masked_softmax_heads(s.reshape(NSA_HEADS, tq, wl), lambda h: mw, dw.astype(F32))
    p_win = jnp.concatenate(pws, axis=0).astype(BF16)
    o_win = _dot_nt(p_win[:, :WINDOW], win[LANES:, :].astype(BF16)) + _dot(p_win[:, WINDOW:], vw_new)

    o_ref[...] = _combine_heads(g_ref[...], ocmp_ref[0], o_sel, o_win, tq)


def _nsa_sample_sel(cache, layer, pages, lpages, cnt, selm, u_q, u_gate, u_kv, o_cmp, win_cache,
                    row0, nb, tq, past_len):
    wq = NSA_HEADS * HEAD_DIM
    base = row0 // tq
    n_cmp = past_len // BLK
    grid_spec = pltpu.PrefetchScalarGridSpec(
        num_scalar_prefetch=3,
        grid=(nb,),
        in_specs=[
            pl.BlockSpec(memory_space=pl.ANY),
            pl.BlockSpec((1, NSA_KV * tq, n_cmp), lambda bb, *_: (bb, 0, 0)),
            pl.BlockSpec((tq, wq), lambda bb, *_: (base + bb, 0)),
            pl.BlockSpec((tq, LANES), lambda bb, *_: (base + bb, 0)),
            pl.BlockSpec((tq, 6 * LANES), lambda bb, *_: (base + bb, 0)),
            pl.BlockSpec((1, NSA_HEADS * tq, LANES), lambda bb, *_: (bb, 0, 0)),
            pl.BlockSpec((None, 1, 2 * LANES, WINDOW), lambda bb, *_: (layer, bb, 0, 0)),
        ],
        out_specs=pl.BlockSpec((tq, wq), lambda bb, *_: (bb, 0)),
        scratch_shapes=[pltpu.VMEM((2, SEL_PAGES_PER_TRIP, 2 * LANES, PAGE), F32),
                        pltpu.SemaphoreType.DMA((2, SEL_PAGES_PER_TRIP))],
    )
    return pl.pallas_call(
        functools.partial(_nsa_sample_sel_kernel, layer=layer, tq=tq, past_len=past_len),
        grid_spec=grid_spec,
        out_shape=jax.ShapeDtypeStruct((nb * tq, wq), F32),
        compiler_params=_cp(("arbitrary",)),
        name="nsa_sample_sel",
    )(pages, lpages, cnt, cache, selm, u_q, u_gate, u_kv, o_cmp, win_cache)


def _outproj_kernel(xp_ref, xs_ref, cp_ref, cs_ref, ap_ref, as_ref, rp_ref, rs_ref, wo_ref, g_ref, b_ref, wr_ref, br_ref,
                    h_ref, te_ref, tg_ref, cnt_ref, *, alpha, n_prompt_tiles):
    is_sample = pl.program_id(0) >= n_prompt_tiles
    pick = lambda p_ref, s_ref: jnp.where(is_sample, s_ref[...], p_ref[...]).astype(BF16)
    mix = (_dot(pick(cp_ref, cs_ref), wo_ref[0:CONV_CH, :])
           + _dot(pick(ap_ref, as_ref), wo_ref[CONV_CH:CONV_CH + NSA_HEADS * HEAD_DIM, :])
           + _dot(pick(rp_ref, rs_ref), wo_ref[CONV_CH + NSA_HEADS * HEAD_DIM:, :]))
    x = jnp.where(is_sample, xs_ref[...], xp_ref[...])
    h = _layer_norm(alpha * x + mix, g_ref[...], b_ref[...])
    h_ref[...] = h
    h_hi = h.astype(BF16)
    h_lo = (h - h_hi.astype(F32)).astype(BF16)
    r = _dot(h_hi, wr_ref[...]) + _dot(h_lo, wr_ref[...])
    logits = r[:, 0:N_EXPERTS] + r[:, N_EXPERTS:2 * N_EXPERTS] + br_ref[...]
    idx = lax.broadcasted_iota(I32, logits.shape, 1).astype(F32)
    col = lax.broadcasted_iota(I32, (logits.shape[0], TOP_K), 1)
    vals = jnp.zeros((logits.shape[0], TOP_K), F32)
    ids = jnp.zeros((logits.shape[0], TOP_K), F32)
    cur = logits
    hist = jnp.zeros((1, N_EXPERTS), F32)
    for k in range(TOP_K):
        mx = jnp.max(cur, axis=-1, keepdims=True)
        am = jnp.min(jnp.where(cur == mx, idx, float(N_EXPERTS)), axis=-1, keepdims=True)
        vals = jnp.where(col == k, mx, vals)
        ids = jnp.where(col == k, am, ids)
        hit = idx == am
        hist = hist + jnp.sum(jnp.where(hit, 1.0, 0.0), axis=0, keepdims=True)
        cur = jnp.where(hit, -jnp.inf, cur)
    e = jnp.exp(vals - vals[:, 0:1])
    tg_ref[...] = e / jnp.sum(e, axis=-1, keepdims=True)
    te_ref[...] = ids.astype(I32)

    @pl.when(pl.program_id(0) == 0)
    def _():
        cnt_ref[...] = jnp.zeros(cnt_ref.shape, F32)

    cnt_ref[...] = cnt_ref[...] + hist


def _outproj(x_p, x_s, mixers, w_out_bf, g, b, w_router, b_router, alpha):
    d = x_p.shape[1]
    n = x_p.shape[0] + x_s.shape[0]
    tm = TOKEN_TILE
    npt = mixers[0][0].shape[0] // tm
    row = lambda c: pl.BlockSpec((tm, c), lambda i: (i, 0))
    row_p = lambda c: pl.BlockSpec((tm, c), lambda i: (jnp.minimum(i, npt - 1), 0))
    row_s = lambda c: pl.BlockSpec((tm, c), lambda i: (jnp.maximum(i - npt, 0), 0))
    full = lambda r, c: pl.BlockSpec((r, c), lambda i: (0, 0))
    mix_specs, mix_args = [], []
    for a_p, a_s in mixers:
        mix_specs += [row_p(a_p.shape[1]), row_s(a_s.shape[1])]
        mix_args += [a_p, a_s]
    return pl.pallas_call(
        functools.partial(_outproj_kernel, alpha=alpha, n_prompt_tiles=npt),
        grid=(n // tm,),
        in_specs=[row_p(d), row_s(d)] + mix_specs + [
                  full(d, d), full(1, d), full(1, d), full(d, LANES), full(1, N_EXPERTS)],
        out_specs=[row(d), row(TOP_K), row(TOP_K), full(1, N_EXPERTS)],
        out_shape=[jax.ShapeDtypeStruct((n, d), F32),
                   jax.ShapeDtypeStruct((n, TOP_K), I32), jax.ShapeDtypeStruct((n, TOP_K), F32),
                   jax.ShapeDtypeStruct((1, N_EXPERTS), F32)],
        compiler_params=_cp(("arbitrary",)),
        name="outproj",
    )(x_p, x_s, *mix_args, w_out_bf, g, b, w_router, b_router)


def _moe_kernel(blk_ref, exp_ref, lo_ref, hi_ref, first_ref, x_ref, wup_ref, bg_ref, bu_ref, wd_ref, bd_ref, o_ref,
                wg_scr, wu_scr, wd_scr):
    i = pl.program_id(0)
    dff = wg_scr.shape[1]

    @pl.when((i == 0) | (exp_ref[i] != exp_ref[jnp.maximum(i - 1, 0)]))
    def _():
        w2 = 2 * LANES
        r = lax.broadcasted_iota(I32, (w2, w2), 0)
        c = lax.broadcasted_iota(I32, (w2, w2), 1)
        perm = jnp.where(r == jnp.where(c < LANES, 2 * c, 2 * (c - LANES) + 1), 1.0, 0.0).astype(BF16)
        for k in range(2 * dff // w2):
            split = _dot(wup_ref[:, k * w2:(k + 1) * w2].astype(BF16), perm).astype(BF16)
            wg_scr[:, k * LANES:(k + 1) * LANES] = split[:, :LANES]
            wu_scr[:, k * LANES:(k + 1) * LANES] = split[:, LANES:]
        wd_scr[...] = wd_ref[...].astype(BF16)

    lo, hi = lo_ref[i], hi_ref[i]

    @pl.when(hi > lo)
    def _():
        x = x_ref[...].astype(BF16)
        g = jnp.minimum(_dot(x, wg_scr[...]) + bg_ref[...], SWIGLU_LIMIT)
        u = jnp.clip(_dot(x, wu_scr[...]) + bu_ref[...], -SWIGLU_LIMIT, SWIGLU_LIMIT)
        act = (u + 1.0) * g * _sigmoid(SWIGLU_ALPHA * g)
        y = _dot(act.astype(BF16), wd_scr[...]) + bd_ref[...]
        row = lax.broadcasted_iota(I32, y.shape, 0)
        mine = (row >= lo) & (row < hi)

        @pl.when(first_ref[i] == 1)
        def _():
            o_ref[...] = jnp.where(mine, y, 0.0)

        @pl.when(first_ref[i] == 0)
        def _():
            o_ref[...] = jnp.where(mine, y, o_ref[...])


def _moe_blocks(xs, items, w_up, bg, bu, w_down, bd, layer):
    r, d = xs.shape
    bm = MOE_TILE
    dff = w_down.shape[2]
    n_items = items[0].shape[0]
    wspec = lambda a, c: pl.BlockSpec((None, None, a, c), lambda i, blk, exp, *_: (layer, exp[i], 0, 0))
    bspec = lambda c: pl.BlockSpec((None, 1, c), lambda i, blk, exp, *_: (exp[i], 0, 0))
    grid_spec = pltpu.PrefetchScalarGridSpec(
        num_scalar_prefetch=5,
        grid=(n_items,),
        in_specs=[
            pl.BlockSpec((bm, d), lambda i, blk, *_: (blk[i], 0)),
            wspec(d, 2 * dff), bspec(dff), bspec(dff), wspec(dff, d), bspec(d),
        ],
        out_specs=pl.BlockSpec((bm, d), lambda i, blk, *_: (blk[i], 0)),
        scratch_shapes=[pltpu.VMEM((d, dff), BF16), pltpu.VMEM((d, dff), BF16), pltpu.VMEM((dff, d), BF16)],
    )
    return pl.pallas_call(
        _moe_kernel,
        grid_spec=grid_spec,
        out_shape=jax.ShapeDtypeStruct((r, d), F32),
        compiler_params=_cp(("arbitrary",)),
        name="moe_experts",
    )(*items, xs, w_up, bg, bu, w_down, bd)


def _moe(h, top_e, counts, w_up, bg, bu, w_down, bd, layer):
    n, d = h.shape
    bm = MOE_TILE
    nk = n * TOP_K
    assert nk % bm == 0
    n_blk = nk // bm
    experts = jnp.arange(N_EXPERTS, dtype=I32)
    flat_e = top_e.T.reshape(nk)
    order = jnp.argsort(flat_e).astype(I32)
    rank = jnp.argsort(order).astype(I32)
    ends = jnp.cumsum(counts)
    starts = ends - counts
    has = ends > starts
    first_blk = starts // bm
    per_e = jnp.where(has, (ends - 1) // bm - first_blk + 1, 0)
    it_end = jnp.cumsum(per_e)
    it_start = it_end - per_e
    n_items = n_blk + N_EXPERTS - 1
    t = jnp.arange(n_items, dtype=I32)
    live = t < it_end[-1]
    e_t = jnp.minimum(jnp.sum((it_end[None, :] <= t[:, None]).astype(I32), axis=1), N_EXPERTS - 1)
    blk_t = first_blk[e_t] + t - it_start[e_t]
    lo = jnp.clip(starts[e_t] - blk_t * bm, 0, bm)
    hi = jnp.clip(ends[e_t] - blk_t * bm, 0, bm)
    e_last = jnp.max(jnp.where(has, experts, 0))
    blk_t = jnp.where(live, blk_t, n_blk - 1)
    e_t = jnp.where(live, e_t, e_last)
    lo = jnp.where(live, lo, 0)
    hi = jnp.where(live, hi, 0)
    first = jnp.concatenate([jnp.ones((1,), bool), blk_t[1:] != blk_t[:-1]]) & live
    items = tuple(a.astype(I32) for a in (blk_t, e_t, lo, hi, first))
    xs = jnp.take(h, order % n, axis=0, mode='clip')
    yb = _moe_blocks(xs, items, w_up, bg, bu, w_down, bd, layer)
    return jnp.take(yb, rank, axis=0, mode='clip')


def _final_kernel(h_ref, *refs, alpha, n_prompt_tiles):
    f_refs = refs[:TOP_K]
    tg_ref, pp_ref, ps_ref, wple_ref, wplg_ref, g_ref, b_ref, yp_ref, ys_ref = refs[TOP_K:]
    h = h_ref[...]
    tg = tg_ref[...]
    ffn = tg[:, 0:1] * f_refs[0][...]
    for k in range(1, TOP_K):
        ffn = ffn + tg[:, k:k + 1] * f_refs[k][...]
    is_sample = pl.program_id(0) >= n_prompt_tiles
    p = jnp.where(is_sample, ps_ref[...], pp_ref[...]).astype(BF16)
    ple = _dot(p, wple_ref[...]) * _sigmoid(_dot(h.astype(BF16), wplg_ref[...]))
    y = _layer_norm(alpha * h + ffn + ple, g_ref[...], b_ref[...])

    @pl.when(jnp.logical_not(is_sample))
    def _():
        yp_ref[...] = y

    @pl.when(is_sample)
    def _():
        ys_ref[...] = y


def _final(h, expert_out, gate, p_prompt, p_sample, layer, w_ple_bf, w_plg_bf, g, b, alpha):
    n, d = h.shape
    tm = TOKEN_TILE
    pd = p_prompt.shape[2]
    npt = p_prompt.shape[1] // tm
    row = lambda c: pl.BlockSpec((tm, c), lambda i: (i, 0))
    full = lambda r, c: pl.BlockSpec((r, c), lambda i: (0, 0))
    kth = lambda k: pl.BlockSpec((tm, d), lambda i: (k * (n // tm) + i, 0))
    p_specs = [pl.BlockSpec((None, tm, pd), lambda i: (layer, jnp.minimum(i, npt - 1), 0)),
               pl.BlockSpec((None, tm, pd), lambda i: (layer, jnp.maximum(i - npt, 0), 0))]
    return pl.pallas_call(
        functools.partial(_final_kernel, alpha=alpha, n_prompt_tiles=npt),
        grid=(n // tm,),
        in_specs=[row(d)] + [kth(k) for k in range(TOP_K)]
        + [row(TOP_K)] + p_specs + [full(pd, d), full(d, d), full(1, d), full(1, d)],
        out_specs=[pl.BlockSpec((tm, d), lambda i: (jnp.minimum(i, npt - 1), 0)),
                   pl.BlockSpec((tm, d), lambda i: (jnp.maximum(i - npt, 0), 0))],
        out_shape=[jax.ShapeDtypeStruct((npt * tm, d), F32), jax.ShapeDtypeStruct((n - npt * tm, d), F32)],
        compiler_params=_cp(("arbitrary",)),
        name="final",
    )(h, *([expert_out] * TOP_K), gate, p_prompt, p_sample, w_ple_bf, w_plg_bf, g, b)


def kernel(x_prompt, x_sample, p_prompt, p_sample, cache_nsa_kv, cache_win_kv, state_ret, state_conv, page_table,
           w_in, w_out, conv_w, conv_b, conv_ln_g, conv_ln_b, ret_norm_g, ln1_g, ln1_b, w_router, b_router,
           w_up, b_up, w_down, b_down, w_ple, w_plg, ln2_g, ln2_b):
    bp, seq, d = x_prompt.shape
    bs, tdec, _ = x_sample.shape
    depth = w_in.shape[0]
    n_pages = page_table.shape[1]
    past_len = n_pages * PAGE
    wbuf = cache_win_kv.shape[2]
    n_p, n_s = bp * seq, bs * tdec
    assert wbuf == WINDOW and tdec <= BLK and tdec % 8 == 0 and past_len // BLK >= N_SEL
    assert seq % (NSA_TQ * NSA_KEY_TILES) == 0 and seq >= WINDOW + NSA_TQ
    assert n_p % TOKEN_TILE == 0 and n_s % TOKEN_TILE == 0
    alpha = (2 * depth) ** 0.25
    n_pool = cache_nsa_kv.shape[1]
    cache = jnp.transpose(cache_nsa_kv, (0, 1, 3, 4, 5, 2)).reshape(depth, n_pool, 4 * LANES, PAGE)
    win_t = jnp.transpose(cache_win_kv, (0, 1, 3, 4, 5, 2)).reshape(depth, bs, 2 * LANES, wbuf)
    page_table = page_table.astype(I32)

    c_conv = 2 * CONV_CH
    c_q = NSA_HEADS * HEAD_DIM
    c_kv = 6 * NSA_KV * HEAD_DIM
    c_gate = 3 * NSA_HEADS
    cuts = [0, c_conv, c_conv + c_q, c_conv + c_q + c_kv, c_conv + c_q + c_kv + c_gate, w_in.shape[2]]

    x_p, x_s = x_prompt.reshape(n_p, d), x_sample.reshape(n_s, d)
    outs = {k: [] for k in ("kvp", "kvs", "wp", "ws", "rp", "rs", "cp", "cs")}
    zeros_ret = jnp.zeros((bp, RET_HEADS, HEAD_DIM, HEAD_DIM), F32)
    zeros_conv = jnp.zeros((bp, CONV_K - 1, CONV_CH), F32)

    for i in range(depth):
        wi = w_in[i].astype(BF16)
        ws = [wi[:, cuts[j]:cuts[j + 1]] for j in range(5)]
        ws[3] = jnp.pad(ws[3], ((0, 0), (0, LANES - c_gate)))
        u_conv, u_q, u_kv, u_gate, u_ret, paged_t, win_new_t = _inproj(x_p, x_s, ws, bp, seq)

        cw = jnp.pad(conv_w[i], ((0, 32 - CONV_K), (0, 0)))
        cargs = (cw, conv_b[i][None], conv_ln_g[i][None], conv_ln_b[i][None])
        conv_p, conv_sp = _conv(u_conv, zeros_conv, *cargs, 0, bp, seq)
        conv_s, conv_ss = _conv(u_conv, state_conv[i], *cargs, n_p, bs, tdec)

        rgain = ret_norm_g[i][None]
        ret_p, ret_sp = _retention(u_ret, zeros_ret, rgain, 0, bp, seq)
        ret_s, ret_ss = _retention(u_ret, state_ret[i], rgain, n_p, bs, tdec)

        nsa_p = _nsa_prompt(u_q, u_gate, u_kv, bp, seq)
        o_cmp, blk_score = _nsa_sample_cmp(cache, i, page_table, u_q, n_p, bs, tdec)
        selm = _topk_blocks(blk_score.reshape(bs * NSA_KV * tdec, -1), N_SEL - 1).reshape(blk_score.shape)
        need = selm.reshape(bs, NSA_KV * tdec, n_pages, PAGE // BLK).max(axis=(1, 3)) > 0.5
        npos = jnp.cumsum(need.astype(I32), axis=1) - 1
        cnt = npos[:, -1] + 1
        pidx = jnp.arange(n_pages, dtype=I32)
        hit = need[:, :, None] & (npos[:, :, None] == pidx[None, None, :])
        order = jnp.sum(jnp.where(hit, pidx[None, :, None], 0), axis=1).astype(I32)
        pages = jnp.take_along_axis(page_table, order, axis=1)
        nsa_s = _nsa_sample_sel(cache, i, pages, order, cnt, selm, u_q, u_gate, u_kv, o_cmp, win_t,
                                n_p, bs, tdec, past_len)

        wr_hi = w_router[i].astype(BF16)
        wr_lo = (w_router[i] - wr_hi.astype(F32)).astype(BF16)
        wr_cat = jnp.pad(jnp.concatenate([wr_hi, wr_lo], axis=1), ((0, 0), (0, LANES - 2 * N_EXPERTS)))
        h, top_e, gate, counts = _outproj(x_p, x_s, ((conv_p, conv_s), (nsa_p, nsa_s), (ret_p, ret_s)),
                                          w_out[i].astype(BF16),
                                          ln1_g[i][None], ln1_b[i][None], wr_cat, b_router[i][None], alpha)

        expert_out = _moe(h, top_e, counts[0].astype(I32), w_up, b_up[i][:, None, 0::2], b_up[i][:, None, 1::2],
                          w_down, b_down[i][:, None, :], i)

        x_p, x_s = _final(h, expert_out, gate, p_prompt.reshape(depth, n_p, -1), p_sample.reshape(depth, n_s, -1), i,
                       w_ple[i].astype(BF16), w_plg[i].astype(BF16),
                       ln2_g[i][None], ln2_b[i][None], alpha)

        kv_shape = (4, NSA_KV, HEAD_DIM)
        outs["kvp"].append(jnp.transpose(paged_t.reshape(bp, *kv_shape, seq), (0, 4, 1, 2, 3)))
        outs["kvs"].append(u_kv[n_p:, :4 * LANES].reshape(bs, tdec, *kv_shape))
        n_keep = min(WINDOW, seq)
        win_t_p = win_new_t[:, :, seq - n_keep:].reshape(bp, 2, NSA_KV, HEAD_DIM, n_keep)
        outs["wp"].append(jnp.transpose(win_t_p, (0, 4, 1, 2, 3)))
        win_new_s = u_kv[n_p:, 4 * LANES:].reshape(bs, tdec, 2, NSA_KV, HEAD_DIM)
        outs["ws"].append(jnp.concatenate([cache_win_kv[i][:, tdec:], win_new_s], axis=1))
        outs["rp"].append(ret_sp)
        outs["rs"].append(ret_ss)
        outs["cp"].append(conv_sp)
        outs["cs"].append(conv_ss)

    y_p = x_p.reshape(bp, seq, d)
    y_s = x_s.reshape(bs, tdec, d)
    st = lambda k: jnp.stack(outs[k])
    return (y_p, y_s, st("kvp"), st("kvs"), st("wp"), st("ws"), st("rp"), st("rs"), st("cp"), st("cs"))
```
